```python
import jax, jax.numpy as jnp
from jax import lax
import numpy as np

D_MODEL = 2048
BATCH = 16
SEQ = 256
DEPTH = 2
DEC_BATCH = 2
DEC_SEQ = 1024
PAST_LEN = 512

GRID_W = 64
CHUNK_MLP = 128
A_WIDTH = D_MODEL // 4
A_GROUPS = 4
A_GDIM = A_WIDTH // A_GROUPS
B_HDIM = 64
B_WIDTH = 3 * D_MODEL // 8
B_HEADS = B_WIDTH // B_HDIM
DECAY_LORA = 64
AAA_LORA = 64
GATE_LORA = 128
C_KDIM = 128
C_WIDTH = D_MODEL - A_WIDTH - B_WIDTH
C_HEADS = C_WIDTH // C_KDIM
C_VDIM = C_WIDTH // C_HEADS
C_CHUNK = 16
D_FF = 5632
A_COLS = 2 * A_WIDTH
B_COLS = 3 * B_WIDTH + 2 * DECAY_LORA + 2 * AAA_LORA + GATE_LORA
C_COLS = 5 * C_WIDTH
IN_COLS = A_COLS + B_COLS + C_COLS
MIX_WIDTH = A_WIDTH + B_WIDTH + C_WIDTH
B_SPLITS = (B_WIDTH, 2 * B_WIDTH, 3 * B_WIDTH, 3 * B_WIDTH + 2 * DECAY_LORA,
            3 * B_WIDTH + 2 * DECAY_LORA + 2 * AAA_LORA)
N_MOD = 9
EPS = 1e-6
LNX_EPS = 64e-5

kernel_name = 'hybrid_gmlp_rwkv7_hgrn2_diffusion_step'


def rmsnorm(x, g):
    xf = x.astype(jnp.float32)
    y = xf * lax.rsqrt(jnp.mean(xf * xf, axis=-1, keepdims=True) + EPS)
    return (y * g.astype(jnp.float32)).astype(x.dtype)


def swiglu(h, w1, w3, w2):
    return (jax.nn.silu(h @ w1) * (h @ w3)) @ w2


def centred_shift(x):
    prev = jnp.pad(x[:, :-1], ((0, 0), (1, 0), (0, 0)))
    nxt = jnp.pad(x[:, 1:], ((0, 0), (0, 1), (0, 0)))
    return 0.5 * (prev + nxt)


def chunk_mlp_mix(za, n_chunks, ng, ws, bs):
    bsz, t = za.shape[0], za.shape[1]
    u, v = jnp.split(jax.nn.gelu(za), 2, axis=-1)
    v = rmsnorm(v.reshape(bsz, n_chunks, CHUNK_MLP, A_GROUPS, A_GDIM), ng)
    v = jnp.einsum('gpq,bnqgc->bnpgc', ws, v) + bs.T[:, :, None]
    return u * v.reshape(bsz, t, A_WIDTH)


def rwkv_scan(r, w, k, v, a, b, s0):
    def step(S, inp):
        r_t, w_t, k_t, v_t, a_t, b_t = inp
        sa = jnp.einsum('bhvk,bhk->bhv', S, a_t)
        S = S * w_t[:, :, None, :] + sa[..., None] * b_t[:, :, None, :] + v_t[..., None] * k_t[:, :, None, :]
        return S, jnp.einsum('bhvk,bhk->bhv', S, r_t)
    xs = tuple(jnp.moveaxis(z, 1, 0) for z in (r, w, k, v, a, b))
    s_fin, y = lax.scan(step, s0.astype(jnp.float32), xs)
    return jnp.moveaxis(y, 0, 1), s_fin


def rwkv7_mix(zb, s0, mu, w0, w2, a0, a2, g2, kk_w, ka_w, r_k, lnx_g, lnx_b):
    zb = zb + (centred_shift(zb) - zb) * mu
    bsz, t = zb.shape[0], zb.shape[1]
    r, k, v, wd, ad, gd = jnp.split(zb, B_SPLITS, axis=-1)
    wd = wd.reshape(bsz, t, 2, DECAY_LORA)
    ad = ad.reshape(bsz, t, 2, AAA_LORA)
    w_raw = w0 + jnp.einsum('btdr,drc->btdc', jnp.tanh(wd), w2)
    decay = jnp.exp(-jnp.exp(-jax.nn.softplus(-w_raw) - 0.5))
    a = jax.nn.sigmoid(a0 + jnp.einsum('btdr,drc->btdc', ad, a2))
    gate = jax.nn.sigmoid(gd) @ g2
    kd = k[:, :, None] * (1.0 + (a - 1.0) * ka_w)
    hd = lambda z: z.reshape(z.shape[:-1] + (B_HEADS, B_HDIM))
    kk = hd(k * kk_w)
    kk = kk / jnp.maximum(jnp.sqrt(jnp.sum(kk * kk, axis=-1, keepdims=True)), 1e-12)
    r_h, v_h = hd(r), hd(v)
    ys, states, bonus = [], [], []
    for d in range(2):
        k_d = hd(kd[:, :, d])
        seq = (r_h, hd(decay[:, :, d]), k_d, v_h, -kk, kk * hd(a[:, :, d]))
        if d == 1:
            seq = tuple(jnp.flip(s_, axis=1) for s_ in seq)
        y, s_fin = rwkv_scan(*seq, s0[:, d])
        ys.append(jnp.flip(y, axis=1) if d == 1 else y)
        states.append(s_fin)
        bonus.append(jnp.sum(r_h * k_d * r_k, axis=-1, keepdims=True) * v_h)
    y = ys[0] + ys[1]
    mean = jnp.mean(y, axis=-1, keepdims=True)
    var = jnp.mean(jnp.square(y - mean), axis=-1, keepdims=True)
    y = (y - mean) * lax.rsqrt(var + LNX_EPS) * lnx_g + lnx_b + bonus[0] + bonus[1]
    return y.reshape(bsz, t, B_WIDTH) * gate, jnp.stack(states, axis=1)


def gla_chunked(q, k, v, logf, s0):
    bsz, t, nh, dk = q.shape
    dv = v.shape[-1]
    nc = t // C_CHUNK
    q, k, logf = (z.reshape(bsz, nc, C_CHUNK, nh, dk) for z in (q, k, logf))
    v = v.reshape(bsz, nc, C_CHUNK, nh, dv)
    b = jnp.cumsum(logf, axis=2)
    mask = jnp.tril(jnp.ones((C_CHUNK, C_CHUNK), dtype=bool))[None, None, :, :, None, None]
    dec = jnp.exp(jnp.where(mask, b[:, :, :, None] - b[:, :, None, :], -jnp.inf))
    att = jnp.einsum('bnihk,bnjhk,bnijhk->bnhij', q, k, dec)
    o_intra = jnp.einsum('bnhij,bnjhv->bnihv', att, v)
    b_end = b[:, :, -1]
    kv = jnp.einsum('bnjhk,bnjhv->bnhkv', k * jnp.exp(b_end[:, :, None] - b), v)

    def step(S, inp):
        cdec, kv_c = inp
        return cdec[..., None] * S + kv_c, S
    s_fin, s_start = lax.scan(step, s0.astype(jnp.float32),
                              (jnp.moveaxis(jnp.exp(b_end), 1, 0), jnp.moveaxis(kv, 1, 0)))
    s_start = jnp.moveaxis(s_start, 0, 1)
    o_inter = jnp.einsum('bnihk,bnhkv->bnihv', q * jnp.exp(b), s_start)
    return (o_intra + o_inter).reshape(bsz, t, nh, dv), s_fin


def hgrn2_mix(zc, s0, lb, gn_g):
    bsz, t = zc.shape[0], zc.shape[1]
    q, f2, iv, g = jnp.split(zc, (C_WIDTH, 3 * C_WIDTH, 4 * C_WIDTH), axis=-1)
    hk = lambda z: z.reshape(bsz, t, C_HEADS, C_KDIM)
    hv = lambda z: z.reshape(bsz, t, C_HEADS, C_VDIM)
    q = hk(jax.nn.silu(q))
    v = hv(iv)
    fg = lb + (1.0 - lb) * jax.nn.sigmoid(f2.reshape(bsz, t, 2, C_WIDTH))
    logf = jnp.log(fg)
    kg = 1.0 - fg
    outs, states = [], []
    for d in range(2):
        seq = (q, hk(kg[:, :, d]), v, hk(logf[:, :, d]))
        if d == 1:
            seq = tuple(jnp.flip(s_, axis=1) for s_ in seq)
        o, s_fin = gla_chunked(*seq, s0[:, d])
        outs.append(jnp.flip(o, axis=1) if d == 1 else o)
        states.append(s_fin)
    o = rmsnorm(outs[0] + outs[1], gn_g) * jax.nn.silu(hv(g))
    return o.reshape(bsz, t, C_WIDTH), jnp.stack(states, axis=1)


def trunk_layer(x, cond, s_rw0, s_hg0, n_chunks, l, P):
    mod = jax.nn.silu(cond) @ P['w_mod'][l] + P['b_mod'][l]
    mod = mod.reshape(cond.shape[0], 1, N_MOD, D_MODEL).astype(x.dtype)
    sh1, sc1, gt1, sh2, sc2, gt2, sh3, sc3, gt3 = [mod[:, :, i] for i in range(N_MOD)]
    h = rmsnorm(x, P['norm_g'][l, 0]) * (1 + sc1) + sh1
    x = x + 0.5 * gt1 * swiglu(h, P['ffn_w1'][l, 0], P['ffn_w3'][l, 0], P['ffn_w2'][l, 0])
    h = rmsnorm(x, P['norm_g'][l, 1]) * (1 + sc2) + sh2
    z = (h @ P['w_in'][l]).astype(jnp.float32)
    za, zb, zc = jnp.split(z, (A_COLS, A_COLS + B_COLS), axis=-1)
    oa = chunk_mlp_mix(za, n_chunks, P['mlp_norm_g'][l], P['mlp_ws'][l], P['mlp_bs'][l])
    ob, s_rw = rwkv7_mix(zb, s_rw0, P['rwkv_mu'][l], P['rwkv_w0'][l], P['rwkv_w2'][l],
                         P['rwkv_a0'][l], P['rwkv_a2'][l], P['rwkv_g2'][l], P['rwkv_kk'][l],
                         P['rwkv_ka'][l], P['rwkv_rk'][l], P['rwkv_lnx_g'][l], P['rwkv_lnx_b'][l])
    oc, s_hg = hgrn2_mix(zc, s_hg0, P['hgrn_lb'][l], P['hgrn_gn'][l])
    o = jnp.concatenate([oa, ob, oc], axis=-1).astype(x.dtype)
    x = x + gt2 * (o @ P['w_out'][l])
    h = rmsnorm(x, P['norm_g'][l, 2]) * (1 + sc3) + sh3
    x = x + 0.5 * gt3 * swiglu(h, P['ffn_w1'][l, 1], P['ffn_w3'][l, 1], P['ffn_w2'][l, 1])
    return x, s_rw, s_hg


def setup_inputs(seed: int = 0) -> dict:
    key = jax.random.key(seed)
    ks = iter(jax.random.split(key, 40))
    f32 = jnp.float32
    nrm = lambda shape, s: jax.random.normal(next(ks), shape, f32) * s
    uni = lambda shape, lo, hi: jax.random.uniform(next(ks), shape, f32, lo, hi)
    return {
        'x_prompt': nrm((BATCH, SEQ, D_MODEL), 1.0),
        'x_sample': nrm((DEC_BATCH, DEC_SEQ, D_MODEL), 1.0),
        'state_rwkv': nrm((DEC_BATCH, DEPTH, 2, B_HEADS, B_HDIM, B_HDIM), 0.5),
        'state_hgrn': nrm((DEC_BATCH, DEPTH, 2, C_HEADS, C_KDIM, C_VDIM), 0.5),
        'c': nrm((DEC_BATCH, D_MODEL), 1.0),
        'c_ctx': nrm((D_MODEL,), 1.0),
        'norm_g': 1.0 + nrm((DEPTH, 3, D_MODEL), 0.05),
        'w_mod': nrm((DEPTH, D_MODEL, N_MOD * D_MODEL), 0.5 * D_MODEL ** -0.5),
        'b_mod': nrm((DEPTH, N_MOD * D_MODEL), 0.02),
        'ffn_w1': nrm((DEPTH, 2, D_MODEL, D_FF), D_MODEL ** -0.5),
        'ffn_w3': nrm((DEPTH, 2, D_MODEL, D_FF), D_MODEL ** -0.5),
        'ffn_w2': nrm((DEPTH, 2, D_FF, D_MODEL), D_FF ** -0.5),
        'w_in': nrm((DEPTH, D_MODEL, IN_COLS), D_MODEL ** -0.5),
        'w_out': nrm((DEPTH, MIX_WIDTH, D_MODEL), MIX_WIDTH ** -0.5),
        'mlp_norm_g': 1.0 + nrm((DEPTH, A_GROUPS, A_GDIM), 0.05),
        'mlp_ws': nrm((DEPTH, A_GROUPS, CHUNK_MLP, CHUNK_MLP), CHUNK_MLP ** -0.5),
        'mlp_bs': 1.0 + nrm((DEPTH, A_GROUPS, CHUNK_MLP), 0.1),
        'rwkv_mu': uni((DEPTH, B_COLS), 0.0, 1.0),
        'rwkv_w0': uni((DEPTH, 2, B_WIDTH), -6.0, -1.0),
        'rwkv_w2': nrm((DEPTH, 2, DECAY_LORA, B_WIDTH), 0.1 * DECAY_LORA ** -0.5),
        'rwkv_a0': nrm((DEPTH, 2, B_WIDTH), 0.1),
        'rwkv_a2': nrm((DEPTH, 2, AAA_LORA, B_WIDTH), 0.5 * AAA_LORA ** -0.5),
        'rwkv_g2': nrm((DEPTH, GATE_LORA, B_WIDTH), GATE_LORA ** -0.5),
        'rwkv_kk': 0.85 + nrm((DEPTH, B_WIDTH), 0.05),
        'rwkv_ka': 1.0 + nrm((DEPTH, B_WIDTH), 0.05),
        'rwkv_rk': nrm((DEPTH, B_HEADS, B_HDIM), 0.1),
        'rwkv_lnx_g': 1.0 + nrm((DEPTH, B_HEADS, B_HDIM), 0.05),
        'rwkv_lnx_b': nrm((DEPTH, B_HEADS, B_HDIM), 0.02),
        'hgrn_lb': nrm((DEPTH, 2, C_WIDTH), 1.0),
        'hgrn_gn': 1.0 + nrm((DEPTH, C_VDIM), 0.05),
        'final_g': 1.0 + nrm((D_MODEL,), 0.05),
    }


def reference(x_prompt, x_sample, state_rwkv, state_hgrn, c, c_ctx, norm_g, w_mod, b_mod,
              ffn_w1, ffn_w3, ffn_w2, w_in, w_out, mlp_norm_g, mlp_ws, mlp_bs, rwkv_mu,
              rwkv_w0, rwkv_w2, rwkv_a0, rwkv_a2, rwkv_g2, rwkv_kk, rwkv_ka, rwkv_rk,
              rwkv_lnx_g, rwkv_lnx_b, hgrn_lb, hgrn_gn, final_g):
    sm = jax.nn.softmax(hgrn_lb.astype(jnp.float32), axis=0)
    lower_bounds = jnp.cumsum(sm, axis=0) - sm[0]
    P = {'w_mod': w_mod, 'b_mod': b_mod, 'norm_g': norm_g, 'ffn_w1': ffn_w1, 'ffn_w3': ffn_w3,
         'ffn_w2': ffn_w2, 'w_in': w_in, 'w_out': w_out, 'mlp_norm_g': mlp_norm_g,
         'mlp_ws': mlp_ws, 'mlp_bs': mlp_bs, 'rwkv_mu': rwkv_mu, 'rwkv_w0': rwkv_w0,
         'rwkv_w2': rwkv_w2, 'rwkv_a0': rwkv_a0, 'rwkv_a2': rwkv_a2, 'rwkv_g2': rwkv_g2,
         'rwkv_kk': rwkv_kk, 'rwkv_ka': rwkv_ka, 'rwkv_rk': rwkv_rk, 'rwkv_lnx_g': rwkv_lnx_g,
         'rwkv_lnx_b': rwkv_lnx_b, 'hgrn_lb': lower_bounds, 'hgrn_gn': hgrn_gn}

    bsz, ctx_len = x_prompt.shape[0], x_prompt.shape[1]
    x = x_prompt
    rw_states, hg_states = [], []
    for l in range(DEPTH):
        s_rw0 = jnp.zeros((bsz, 2, B_HEADS, B_HDIM, B_HDIM), jnp.float32)
        s_hg0 = jnp.zeros((bsz, 2, C_HEADS, C_KDIM, C_VDIM), jnp.float32)
        x, s_rw, s_hg = trunk_layer(x, c_ctx[None], s_rw0, s_hg0, ctx_len // CHUNK_MLP, l, P)
        rw_states.append(s_rw)
        hg_states.append(s_hg)
    y_prompt = rmsnorm(x, final_g)
    new_state_rwkv = jnp.stack(rw_states, axis=1)
    new_state_hgrn = jnp.stack(hg_states, axis=1)

    rows = x_sample.shape[1] // GRID_W
    n_chunks = rows * GRID_W // CHUNK_MLP
    x = x_sample
    for l in range(DEPTH):
        x, _, _ = trunk_layer(x, c, state_rwkv[:, l], state_hgrn[:, l], n_chunks, l, P)
    y_sample = rmsnorm(x, final_g)
    return (y_prompt, y_sample, new_state_rwkv, new_state_hgrn)
```

```python
import functools

import jax
import jax.numpy as jnp
from jax import lax
from jax.experimental import pallas as pl
from jax.experimental.pallas import tpu as pltpu

F32 = jnp.float32
BF16 = jnp.bfloat16

D_MODEL = 2048
D_FF = 5632
N_MOD = 9
EPS = 1e-6
LNX_EPS = 64e-5

CHUNK_MLP = 128
A_WIDTH = 512
A_GROUPS = 4
A_GDIM = 128
B_HDIM = 64
B_WIDTH = 768
B_HEADS = 12
LORA = 64
GATE_LORA = 128
B_COLS = 3 * B_WIDTH + 4 * LORA + GATE_LORA
C_KDIM = 128
C_WIDTH = 768
C_HEADS = 6
C_COLS = 5 * C_WIDTH
A_COLS = 2 * A_WIDTH

LANES = 128
VMEM_LIMIT = 56 * 1024 * 1024

TOKEN_TILE = 512
FF_TILE = 512
PREP_TILE = 256
RW_CHUNK = 64
HG_BLOCK = 64
HG_SUB = 16


def _cparams(*sem):
    return pltpu.CompilerParams(dimension_semantics=sem, vmem_limit_bytes=VMEM_LIMIT)


def _dot(a, b):
    return jnp.dot(a, b, preferred_element_type=F32)


def _dot_nt(a, b):
    return lax.dot_general(a, b, (((1,), (1,)), ((), ())), preferred_element_type=F32)


def _dot_tn(a, b):
    return lax.dot_general(a, b, (((0,), (0,)), ((), ())), preferred_element_type=F32)


def _split(x, parts):
    out = []
    for _ in range(parts - 1):
        hi = x.astype(BF16)
        out.append(hi)
        x = x - hi.astype(F32)
    out.append(x.astype(BF16))
    return out


def _exact_left(m, x, parts):
    acc = None
    for p in _split(x, parts):
        t = _dot(m, p)
        acc = t if acc is None else acc + t
    return acc


def _exact_right(x, m, parts):
    acc = None
    for p in _split(x, parts):
        t = _dot(p, m)
        acc = t if acc is None else acc + t
    return acc


def _iota(shape, dim):
    return lax.broadcasted_iota(jnp.int32, shape, dim)


def _sigmoid(x):
    return jax.nn.sigmoid(x)


def _silu(x):
    return x * jax.nn.sigmoid(x)


def _norm_mod(x, g, sc, sh):
    ms = jnp.mean(x * x, axis=-1, keepdims=True)
    return (x * lax.rsqrt(ms + EPS) * g) * (1.0 + sc) + sh


def _mod_kernel(cond_ref, w_ref, b_ref, o_ref):
    s = _silu(cond_ref[...]).astype(BF16)
    o_ref[...] = _dot(s, w_ref[...].astype(BF16)) + b_ref[...]


def _mod_rows(cond8, w_mod, b_mod):
    depth, _, n = w_mod.shape
    tn = 1024
    return pl.pallas_call(
        _mod_kernel,
        grid=(depth, n // tn),
        in_specs=[
            pl.BlockSpec((8, D_MODEL), lambda l, j: (0, 0)),
            pl.BlockSpec((None, D_MODEL, tn), lambda l, j: (l, 0, j)),
            pl.BlockSpec((None, 1, tn), lambda l, j: (l, 0, j)),
        ],
        out_specs=pl.BlockSpec((None, 8, tn), lambda l, j: (l, 0, j)),
        out_shape=jax.ShapeDtypeStruct((depth, 8, n), F32),
        compiler_params=_cparams("parallel", "parallel"),
        name="adaln_rows",
    )(cond8, w_mod, b_mod.reshape(depth, 1, n))


class _Tokens:
    def __init__(self, n_ctx, ctx_len, n_lat, lat_len):
        self.n_ctx, self.ctx_len, self.n_lat, self.lat_len = n_ctx, ctx_len, n_lat, lat_len
        self.ctx_tokens = n_ctx * ctx_len
        self.total = self.ctx_tokens + n_lat * lat_len

    def mod_row(self, tile, i):
        nct = self.ctx_tokens // tile
        per_lat = self.lat_len // tile
        return jnp.where(i < nct, 0, 1 + (i - nct) // per_lat)


def _mod_spec(tok, tile, layer, which, grid_rank):
    def imap(*idx):
        return (layer, which, tok.mod_row(tile, idx[0]), 0, 0)
    del grid_rank
    return pl.BlockSpec((None, None, None, 1, D_MODEL), imap)


def _normmod_kernel(x_ref, g_ref, sc_ref, sh_ref, h_ref):
    h_ref[...] = _norm_mod(x_ref[...], g_ref[...], sc_ref[...], sh_ref[...]).astype(h_ref.dtype)


def _normmod(tok, x, norm_g_row, modr, layer, sc_i, sh_i):
    tm = TOKEN_TILE
    return pl.pallas_call(
        _normmod_kernel,
        grid=(tok.total // tm,),
        in_specs=[
            pl.BlockSpec((tm, D_MODEL), lambda i: (i, 0)),
            pl.BlockSpec((1, D_MODEL), lambda i: (0, 0)),
            _mod_spec(tok, tm, layer, sc_i, 1),
            _mod_spec(tok, tm, layer, sh_i, 1),
        ],
        out_specs=pl.BlockSpec((tm, D_MODEL), lambda i: (i, 0)),
        out_shape=jax.ShapeDtypeStruct((tok.total, D_MODEL), BF16),
        compiler_params=_cparams("parallel"),
        name="first_norm",
    )(x, norm_g_row, modr, modr)


def _ffn_kernel(h_ref, x_ref, gt_ref, w1_ref, w3_ref, w2_ref, gn_ref, scn_ref, shn_ref,
                xo_ref, ho_ref, acc_ref, *, n_ff):
    j = pl.program_id(1)

    @pl.when(j == 0)
    def _():
        acc_ref[...] = jnp.zeros_like(acc_ref)

    h = h_ref[...]
    a = _dot(h, w1_ref[...])
    b = _dot(h, w3_ref[...])
    p = (_silu(a) * b).astype(BF16)
    acc_ref[...] += _dot(p, w2_ref[...])

    @pl.when(j == n_ff - 1)
    def _():
        xn = x_ref[...] + 0.5 * gt_ref[...] * acc_ref[...]
        xo_ref[...] = xn
        ho_ref[...] = _norm_mod(xn, gn_ref[...], scn_ref[...], shn_ref[...]).astype(ho_ref.dtype)


def _ffn(tok, h, x, modr, layer, gate_i, w1, w3, w2, next_g, next_sc, next_sh, next_dtype):
    tm, tf = TOKEN_TILE, FF_TILE
    n_ff = D_FF // tf
    row = pl.BlockSpec((tm, D_MODEL), lambda i, j: (i, 0))
    return pl.pallas_call(
        functools.partial(_ffn_kernel, n_ff=n_ff),
        grid=(tok.total // tm, n_ff),
        in_specs=[
            row, row,
            _mod_spec(tok, tm, layer, gate_i, 2),
            pl.BlockSpec((D_MODEL, tf), lambda i, j: (0, j)),
            pl.BlockSpec((D_MODEL, tf), lambda i, j: (0, j)),
            pl.BlockSpec((tf, D_MODEL), lambda i, j: (j, 0)),
            pl.BlockSpec((1, D_MODEL), lambda i, j: (0, 0)),
            next_sc[1], next_sh[1],
        ],
        out_specs=[row, row],
        out_shape=[jax.ShapeDtypeStruct((tok.total, D_MODEL), F32),
                   jax.ShapeDtypeStruct((tok.total, D_MODEL), next_dtype)],
        scratch_shapes=[pltpu.VMEM((tm, D_MODEL), F32)],
        compiler_params=_cparams("parallel", "arbitrary"),
        name="swiglu_half_step",
    )(h, x, modr, w1, w3, w2, next_g, next_sc[0], next_sh[0])


def _matmul_kernel(h_ref, w_ref, o_ref):
    o_ref[...] = _dot(h_ref[...], w_ref[...])


def _project(h, w, tn):
    t, k = h.shape
    n = w.shape[1]
    tm = TOKEN_TILE
    return pl.pallas_call(
        _matmul_kernel,
        grid=(n // tn, t // tm),
        in_specs=[pl.BlockSpec((tm, k), lambda j, i: (i, 0)),
                  pl.BlockSpec((k, tn), lambda j, i: (0, j))],
        out_specs=pl.BlockSpec((tm, tn), lambda j, i: (i, j)),
        out_shape=jax.ShapeDtypeStruct((t, n), F32),
        compiler_params=_cparams("parallel", "parallel"),
        name="mix_in_proj",
    )(h, w)


def _gmlp_kernel(za_ref, ng_ref, ws_ref, bs_ref, o_ref):
    z = jax.nn.gelu(za_ref[...])
    outs = []
    for g in range(A_GROUPS):
        u = z[:, g * A_GDIM:(g + 1) * A_GDIM]
        v = z[:, A_WIDTH + g * A_GDIM:A_WIDTH + (g + 1) * A_GDIM]
        ms = jnp.mean(v * v, axis=-1, keepdims=True)
        v = v * lax.rsqrt(ms + EPS) * ng_ref[g:g + 1, :]
        mixed = _dot(ws_ref[g], v.astype(BF16)) + bs_ref[g]
        outs.append(u * mixed)
    o_ref[...] = jnp.concatenate(outs, axis=-1).astype(o_ref.dtype)


def _gmlp(za, ng, ws, bs):
    t = za.shape[0]
    return pl.pallas_call(
        _gmlp_kernel,
        grid=(t // CHUNK_MLP,),
        in_specs=[
            pl.BlockSpec((CHUNK_MLP, A_COLS), lambda i: (i, 0)),
            pl.BlockSpec((A_GROUPS, A_GDIM), lambda i: (0, 0)),
            pl.BlockSpec((A_GROUPS, CHUNK_MLP, CHUNK_MLP), lambda i: (0, 0, 0)),
            pl.BlockSpec((A_GROUPS, CHUNK_MLP, 1), lambda i: (0, 0, 0)),
        ],
        out_specs=pl.BlockSpec((CHUNK_MLP, A_WIDTH), lambda i: (i, 0)),
        out_shape=jax.ShapeDtypeStruct((t, A_WIDTH), BF16),
        compiler_params=_cparams("parallel"),
        name="gmlp_chunk_mix",
    )(za, ng, ws.astype(BF16), bs[:, :, None])


def _head_ones():
    r = _iota((LANES, LANES), 0) // B_HDIM
    c = _iota((LANES, LANES), 1) // B_HDIM
    return (r == c).astype(BF16)


def _head_sum(x, parts=2):
    ones = _head_ones()
    outs = []
    for s in range(x.shape[-1] // LANES):
        outs.append(_exact_right(x[:, s * LANES:(s + 1) * LANES], ones, parts))
    return jnp.concatenate(outs, axis=-1)


def _rwkv_prep_kernel(z_ref, zp_ref, zn_ref, mu_ref, w0_ref, w2_ref, a0_ref, a2_ref, g2_ref,
                      kkw_ref, kaw_ref, rk_ref,
                      r_ref, v_ref, kk_ref, lw_ref, kd_ref, b_ref, gate_ref, bonus_ref,
                      *, n_ctx_tiles, ctx_tiles_per_seq, lat_tiles_per_seq):
    i = pl.program_id(0)
    pos = jnp.where(i < n_ctx_tiles, i % ctx_tiles_per_seq, (i - n_ctx_tiles) % lat_tiles_per_seq)
    last = jnp.where(i < n_ctx_tiles, ctx_tiles_per_seq - 1, lat_tiles_per_seq - 1)
    z = z_ref[...]
    t = z.shape[0]
    row = _iota((t, 1), 0)
    halo_prev = jnp.where(pos != 0, zp_ref[7:8, :], 0.0)
    halo_next = jnp.where(pos != last, zn_ref[0:1, :], 0.0)
    prev = jnp.where(row == 0, halo_prev, pltpu.roll(z, 1, 0))
    nxt = jnp.where(row == t - 1, halo_next, pltpu.roll(z, t - 1, 0))
    z = z + (0.5 * (prev + nxt) - z) * mu_ref[...]

    w = B_WIDTH
    r, k, v = z[:, 0:w], z[:, w:2 * w], z[:, 2 * w:3 * w]
    wd = z[:, 3 * w:3 * w + 2 * LORA]
    ad = z[:, 3 * w + 2 * LORA:3 * w + 4 * LORA]
    gd = z[:, 3 * w + 4 * LORA:]
    twd = jnp.tanh(wd).astype(BF16)
    adb = ad.astype(BF16)

    kk = k * kkw_ref[...]
    nrm = jnp.sqrt(_head_sum(kk * kk))
    kk = kk / jnp.maximum(nrm, 1e-12)

    r_ref[...] = r
    v_ref[...] = v
    kk_ref[...] = kk
    gate_ref[...] = _dot(_sigmoid(gd).astype(BF16), g2_ref[...])
    ksum = jnp.zeros_like(k)
    for d in range(2):
        w_raw = w0_ref[d:d + 1, :] + _dot(twd[:, d * LORA:(d + 1) * LORA], w2_ref[d])
        lw_ref[d] = -jnp.exp(-jax.nn.softplus(-w_raw) - 0.5)
        a = _sigmoid(a0_ref[d:d + 1, :] + _dot(adb[:, d * LORA:(d + 1) * LORA], a2_ref[d]))
        kd = k * (1.0 + (a - 1.0) * kaw_ref[...])
        kd_ref[d] = kd
        b_ref[d] = kk * a
        ksum = ksum + kd
    bonus_ref[...] = _head_sum(r * ksum * rk_ref[...]) * v


def _rwkv_prep(tok, zb, mu, w0, w2, a0, a2, g2, kkw, kaw, rk):
    tp = PREP_TILE
    n_tiles = tok.total // tp
    rows8 = tok.total // 8
    per = tp // 8
    full = lambda shape: pl.BlockSpec(shape, lambda i: (0,) * len(shape))
    tile = pl.BlockSpec((tp, B_WIDTH), lambda i: (i, 0))
    tile2 = pl.BlockSpec((2, tp, B_WIDTH), lambda i: (0, i, 0))
    o1 = jax.ShapeDtypeStruct((tok.total, B_WIDTH), F32)
    o2 = jax.ShapeDtypeStruct((2, tok.total, B_WIDTH), F32)
    kern = functools.partial(_rwkv_prep_kernel, n_ctx_tiles=tok.ctx_tokens // tp,
                             ctx_tiles_per_seq=tok.ctx_len // tp, lat_tiles_per_seq=tok.lat_len // tp)
    return pl.pallas_call(
        kern,
        grid=(n_tiles,),
        in_specs=[
            pl.BlockSpec((tp, B_COLS), lambda i: (i, 0)),
            pl.BlockSpec((8, B_COLS), lambda i: (jnp.maximum(i * per - 1, 0), 0)),
            pl.BlockSpec((8, B_COLS), lambda i: (jnp.minimum((i + 1) * per, rows8 - 1), 0)),
            full((1, B_COLS)), full((2, B_WIDTH)), full((2, LORA, B_WIDTH)), full((2, B_WIDTH)),
            full((2, LORA, B_WIDTH)), full((GATE_LORA, B_WIDTH)),
            full((1, B_WIDTH)), full((1, B_WIDTH)), full((1, B_WIDTH)),
        ],
        out_specs=[tile, tile, tile, tile2, tile2, tile2, tile, tile],
        out_shape=[o1, o1, o1, o2, o2, o2, o1, o1],
        compiler_params=_cparams("parallel"),
        name="rwkv_prep",
    )(zb, zb, zb, mu, w0, w2.astype(BF16), a0, a2.astype(BF16), g2.astype(BF16), kkw, kaw, rk)


def _rwkv_chunk(r, v, kk, lw, kd, b, s_ref, reverse):
    c = r.shape[0]
    ri, ci = _iota((c, c), 0), _iota((c, c), 1)
    before = (ci > ri) if reverse else (ci < ri)
    upto = (ci >= ri) if reverse else (ci <= ri)
    g = _exact_left(upto.astype(BF16), lw, 3)
    g_last = g[0:1, :] if reverse else g[c - 1:c, :]
    eg = jnp.exp(g)
    ieg = jnp.exp(-g)
    a_t = (-kk * jnp.exp(g - lw)).astype(BF16)
    r_t = (r * eg).astype(BF16)
    b_t = (b * ieg).astype(BF16)
    k_t = (kd * ieg).astype(BF16)
    vb = v.astype(BF16)
    gl = jnp.exp(g_last)
    eye = (ri == ci).astype(F32)
    pair_masks = []
    n = 1
    while n < c:
        pair_masks.append((n, ((ri // (2 * n)) == (ci // (2 * n))) & ((ri // n) != (ci // n))))
        n *= 2
    ys = []
    for h in range(B_HEADS):
        sl = slice(h * B_HDIM, (h + 1) * B_HDIM)
        ah, rh, bh, kh, vh = a_t[:, sl], r_t[:, sl], b_t[:, sl], k_t[:, sl], vb[:, sl]
        s0 = s_ref[h]
        s0b = s0.astype(BF16)
        lmat = jnp.where(before, _dot_nt(ah, bh), 0.0)
        aak = jnp.where(before, _dot_nt(ah, kh), 0.0).astype(BF16)
        rb = jnp.where(upto, _dot_nt(rh, bh), 0.0).astype(BF16)
        rkm = jnp.where(upto, _dot_nt(rh, kh), 0.0).astype(BF16)
        x = _dot_nt(ah, s0b) + _dot(aak, vh)
        tinv = None
        for n, pair in pair_masks:
            m = jnp.where(pair, lmat, 0.0)
            if tinv is None:
                tinv = eye + m
            else:
                tb = tinv.astype(BF16)
                tinv = tinv + _dot(tb, _dot(m.astype(BF16), tb).astype(BF16))
        ub = _dot(tinv.astype(BF16), x.astype(BF16)).astype(BF16)
        ys.append(_dot_nt(rh, s0b) + _dot(rb, ub) + _dot(rkm, vh))
        s_new = s0 + _dot_tn(ub, bh) + _dot_tn(vh, kh)
        s_ref[h] = s_new * gl[:, sl]
    return jnp.concatenate(ys, axis=-1)


def _rwkv_scan_kernel(rf_ref, vf_ref, kkf_ref, lwf_ref, kdf_ref, bf_ref,
                      rb_ref, vb_ref, kkb_ref, lwb_ref, kdb_ref, bb_ref, s0_ref,
                      yf_ref, yb_ref, sfin_ref, s_ref, *, n_chunks):
    c = pl.program_id(1)

    @pl.when(c == 0)
    def _():
        s_ref[...] = s0_ref[...]

    yf_ref[...] = _rwkv_chunk(rf_ref[...], vf_ref[...], kkf_ref[...], lwf_ref[...], kdf_ref[...],
                              bf_ref[...], s_ref.at[0], False)
    yb_ref[...] = _rwkv_chunk(rb_ref[...], vb_ref[...], kkb_ref[...], lwb_ref[...], kdb_ref[...],
                              bb_ref[...], s_ref.at[1], True)

    @pl.when(c == n_chunks - 1)
    def _():
        sfin_ref[...] = s_ref[...]


def _rwkv_scan(r, v, kk, lw, kd, b, s0, tok_off, n_seq, seq_len, total):
    cl = RW_CHUNK
    nc = seq_len // cl
    off = tok_off // cl
    fwd = lambda s, c: (off + s * nc + c, 0)
    bwd = lambda s, c: (off + s * nc + (nc - 1 - c), 0)
    fwd2 = lambda d: (lambda s, c: (d, off + s * nc + c, 0))
    bwd2 = lambda d: (lambda s, c: (d, off + s * nc + (nc - 1 - c), 0))
    blk = lambda im: pl.BlockSpec((cl, B_WIDTH), im)
    blk2 = lambda im: pl.BlockSpec((None, cl, B_WIDTH), im)
    st = pl.BlockSpec((None, 2, B_HEADS, B_HDIM, B_HDIM), lambda s, c: (s, 0, 0, 0, 0))
    n_rows = n_seq * seq_len
    del total
    yshape = jax.ShapeDtypeStruct((n_rows, B_WIDTH), F32)
    local_f = lambda s, c: (s * nc + c, 0)
    local_b = lambda s, c: (s * nc + (nc - 1 - c), 0)
    return pl.pallas_call(
        functools.partial(_rwkv_scan_kernel, n_chunks=nc),
        grid=(n_seq, nc),
        in_specs=[blk(fwd), blk(fwd), blk(fwd), blk2(fwd2(0)), blk2(fwd2(0)), blk2(fwd2(0)),
                  blk(bwd), blk(bwd), blk(bwd), blk2(bwd2(1)), blk2(bwd2(1)), blk2(bwd2(1)), st],
        out_specs=[pl.BlockSpec((cl, B_WIDTH), local_f), pl.BlockSpec((cl, B_WIDTH), local_b), st],
        out_shape=[yshape, yshape, jax.ShapeDtypeStruct(s0.shape, F32)],
        scratch_shapes=[pltpu.VMEM((2, B_HEADS, B_HDIM, B_HDIM), F32)],
        compiler_params=_cparams("parallel", "arbitrary"),
        name="rwkv7_scan",
    )(r, v, kk, lw, kd, b, r, v, kk, lw, kd, b, s0)


def _rwkv_post_kernel(yf_ref, yb_ref, bonus_ref, gate_ref, g_ref, b_ref, o_ref):
    y = yf_ref[...] + yb_ref[...]
    mean = _head_sum(y, 3) * (1.0 / B_HDIM)
    yc = y - mean
    var = _head_sum(yc * yc) * (1.0 / B_HDIM)
    y = yc * lax.rsqrt(var + LNX_EPS) * g_ref[...] + b_ref[...] + bonus_ref[...]
    o_ref[...] = (y * gate_ref[...]).astype(o_ref.dtype)


def _rwkv_post(yf, yb, bonus, gate, lnx_g, lnx_b):
    t = yf.shape[0]
    tm = TOKEN_TILE
    tile = pl.BlockSpec((tm, B_WIDTH), lambda i: (i, 0))
    vec = pl.BlockSpec((1, B_WIDTH), lambda i: (0, 0))
    return pl.pallas_call(
        _rwkv_post_kernel,
        grid=(t // tm,),
        in_specs=[tile, tile, tile, tile, vec, vec],
        out_specs=tile,
        out_shape=jax.ShapeDtypeStruct((t, B_WIDTH), BF16),
        compiler_params=_cparams("parallel"),
        name="rwkv_post",
    )(yf, yb, bonus, gate, lnx_g, lnx_b)


def _hgrn_block(qraw, fraw, v, lb, st_ref, reverse):
    n = qraw.shape[0]
    sub = HG_SUB
    fg = lb + (1.0 - lb) * _sigmoid(fraw)
    logf = jnp.log(fg)
    kg = 1.0 - fg
    q = _silu(qraw)
    ri, ci = _iota((n, n), 0), _iota((n, n), 1)
    same = (ri // sub) == (ci // sub)
    upto = same & ((ci >= ri) if reverse else (ci <= ri))
    bcum = _exact_left(upto.astype(BF16), logf, 3)
    ones = jnp.ones((LANES, LANES), BF16)
    si = _iota((sub, 1), 0)
    order = range(n // sub - 1, -1, -1) if reverse else range(n // sub)
    out_rows = [None] * (n // sub)
    for sc in order:
        rows = slice(sc * sub, (sc + 1) * sub)
        last = sc * sub if reverse else (sc + 1) * sub - 1
        outs = []
        for h in range(C_HEADS):
            cols = slice(h * C_KDIM, (h + 1) * C_KDIM)
            qs, ks, vs, bs = q[rows, cols], kg[rows, cols], v[rows, cols], bcum[rows, cols]
            b_end = bcum[last:last + 1, cols]
            slabs = []
            for j in range(sub):
                live = (si <= j) if reverse else (si >= j)
                e = jnp.exp(jnp.where(live, bs - bs[j:j + 1, :], -1e30))
                slabs.append(qs * e * ks[j:j + 1, :])
            att = _dot(jnp.concatenate(slabs, axis=0).astype(BF16), ones)
            o = _dot_nt((qs * jnp.exp(bs)).astype(BF16), st_ref[h].astype(BF16))
            for j in range(sub):
                o = o + att[j * sub:(j + 1) * sub, :] * vs[j:j + 1, :]
            outs.append(o)
            ke = (ks * jnp.exp(b_end - bs)).astype(BF16)
            st_ref[h] = st_ref[h] * jnp.exp(b_end) + _dot_tn(vs.astype(BF16), ke)
        out_rows[sc] = jnp.concatenate(outs, axis=-1)
    return jnp.concatenate(out_rows, axis=0)


def _hgrn_scan_kernel(qf_ref, ff_ref, vf_ref, qb_ref, fb_ref, vb_ref, lb_ref, s0_ref,
                      of_ref, ob_ref, sfin_ref, st_ref, *, n_chunks):
    c = pl.program_id(1)

    @pl.when(c == 0)
    def _():
        for d in range(2):
            for h in range(C_HEADS):
                st_ref[d, h] = s0_ref[d, h].T

    of_ref[...] = _hgrn_block(qf_ref[...], ff_ref[...], vf_ref[...], lb_ref[0:1, :], st_ref.at[0], False)
    ob_ref[...] = _hgrn_block(qb_ref[...], fb_ref[...], vb_ref[...], lb_ref[1:2, :], st_ref.at[1], True)

    @pl.when(c == n_chunks - 1)
    def _():
        for d in range(2):
            for h in range(C_HEADS):
                sfin_ref[d, h] = st_ref[d, h].T


def _hgrn_scan(zc, lb, s0, tok_off, n_seq, seq_len):
    cl = HG_BLOCK
    nc = seq_len // cl
    off = tok_off // cl
    fwd = lambda col: (lambda s, c: (off + s * nc + c, col))
    bwd = lambda col: (lambda s, c: (off + s * nc + (nc - 1 - c), col))
    blk = lambda im: pl.BlockSpec((cl, C_WIDTH), im)
    st = pl.BlockSpec((None, 2, C_HEADS, C_KDIM, C_KDIM), lambda s, c: (s, 0, 0, 0, 0))
    n_rows = n_seq * seq_len
    oshape = jax.ShapeDtypeStruct((n_rows, C_WIDTH), F32)
    return pl.pallas_call(
        functools.partial(_hgrn_scan_kernel, n_chunks=nc),
        grid=(n_seq, nc),
        in_specs=[blk(fwd(0)), blk(fwd(1)), blk(fwd(3)), blk(bwd(0)), blk(bwd(2)), blk(bwd(3)),
                  pl.BlockSpec((2, C_WIDTH), lambda s, c: (0, 0)), st],
        out_specs=[pl.BlockSpec((cl, C_WIDTH), lambda s, c: (s * nc + c, 0)),
                   pl.BlockSpec((cl, C_WIDTH), lambda s, c: (s * nc + (nc - 1 - c), 0)), st],
        out_shape=[oshape, oshape, jax.ShapeDtypeStruct(s0.shape, F32)],
        scratch_shapes=[pltpu.VMEM((2, C_HEADS, C_KDIM, C_KDIM), F32)],
        compiler_params=_cparams("parallel", "arbitrary"),
        name="hgrn2_scan",
    )(zc, zc, zc, zc, zc, zc, lb, s0)


def _hgrn_post_kernel(of_ref, ob_ref, g_ref, gn_ref, o_ref):
    o = of_ref[...] + ob_ref[...]
    gate = _silu(g_ref[...])
    outs = []
    for h in range(C_HEADS):
        cols = slice(h * C_KDIM, (h + 1) * C_KDIM)
        oh = o[:, cols]
        ms = jnp.mean(oh * oh, axis=-1, keepdims=True)
        outs.append(oh * lax.rsqrt(ms + EPS) * gn_ref[...] * gate[:, cols])
    o_ref[...] = jnp.concatenate(outs, axis=-1).astype(o_ref.dtype)


def _hgrn_post(of, ob, zc, gn):
    t = of.shape[0]
    tm = TOKEN_TILE
    tile = pl.BlockSpec((tm, C_WIDTH), lambda i: (i, 0))
    return pl.pallas_call(
        _hgrn_post_kernel,
        grid=(t // tm,),
        in_specs=[tile, tile, pl.BlockSpec((tm, C_WIDTH), lambda i: (i, 4)),
                  pl.BlockSpec((1, C_KDIM), lambda i: (0, 0))],
        out_specs=tile,
        out_shape=jax.ShapeDtypeStruct((t, C_WIDTH), BF16),
        compiler_params=_cparams("parallel"),
        name="hgrn2_post",
    )(of, ob, zc, gn)


def _mix_out_kernel(oa_ref, ob_ref, oc_ref, wa_ref, wb_ref, wc_ref, x_ref, gt_ref,
                    gn_ref, scn_ref, shn_ref, xo_ref, ho_ref):
    y = _dot(oa_ref[...], wa_ref[...]) + _dot(ob_ref[...], wb_ref[...]) + _dot(oc_ref[...], wc_ref[...])
    xn = x_ref[...] + gt_ref[...] * y
    xo_ref[...] = xn
    ho_ref[...] = _norm_mod(xn, gn_ref[...], scn_ref[...], shn_ref[...]).astype(ho_ref.dtype)


def _mix_out(tok, oa, ob, oc, wa, wb, wc, x, modr, layer, next_g):
    tm = TOKEN_TILE
    row = pl.BlockSpec((tm, D_MODEL), lambda i: (i, 0))
    full = lambda a: pl.BlockSpec(a.shape, lambda i: (0, 0))
    return pl.pallas_call(
        _mix_out_kernel,
        grid=(tok.total // tm,),
        in_specs=[pl.BlockSpec((tm, A_WIDTH), lambda i: (i, 0)),
                  pl.BlockSpec((tm, B_WIDTH), lambda i: (i, 0)),
                  pl.BlockSpec((tm, C_WIDTH), lambda i: (i, 0)),
                  full(wa), full(wb), full(wc), row,
                  _mod_spec(tok, tm, layer, 5, 1),
                  pl.BlockSpec((1, D_MODEL), lambda i: (0, 0)),
                  _mod_spec(tok, tm, layer, 7, 1), _mod_spec(tok, tm, layer, 6, 1)],
        out_specs=[row, row],
        out_shape=[jax.ShapeDtypeStruct((tok.total, D_MODEL), F32),
                   jax.ShapeDtypeStruct((tok.total, D_MODEL), BF16)],
        compiler_params=_cparams("parallel"),
        name="mix_out_proj",
    )(oa, ob, oc, wa, wb, wc, x, modr, next_g, modr, modr)


def kernel(x_prompt, x_sample, state_rwkv, state_hgrn, c, c_ctx, norm_g, w_mod, b_mod, ffn_w1, ffn_w3,
           ffn_w2, w_in, w_out, mlp_norm_g, mlp_ws, mlp_bs, rwkv_mu, rwkv_w0, rwkv_w2, rwkv_a0, rwkv_a2,
           rwkv_g2, rwkv_kk, rwkv_ka, rwkv_rk, rwkv_lnx_g, rwkv_lnx_b, hgrn_lb, hgrn_gn, final_g):
    n_ctx, ctx_len, _ = x_prompt.shape
    n_lat, lat_len, _ = x_sample.shape
    depth = w_mod.shape[0]
    tok = _Tokens(n_ctx, ctx_len, n_lat, lat_len)
    assert n_lat + 1 <= 8 and ctx_len % PREP_TILE == 0 and lat_len % TOKEN_TILE == 0

    cond8 = jnp.zeros((8, D_MODEL), F32).at[0].set(c_ctx).at[1:1 + n_lat].set(c)
    mod = _mod_rows(cond8, w_mod, b_mod)
    modr = mod.reshape(depth, 8, N_MOD, D_MODEL).transpose(0, 2, 1, 3)[:, :, :, None, :]

    sm = jax.nn.softmax(hgrn_lb.astype(F32), axis=0)
    lower = jnp.cumsum(sm, axis=0) - sm[0]

    x = jnp.concatenate([x_prompt.reshape(-1, D_MODEL), x_sample.reshape(-1, D_MODEL)], axis=0)
    zero_sc = jnp.zeros((1, D_MODEL), F32)
    zero_spec = pl.BlockSpec((1, D_MODEL), lambda i, j: (0, 0))
    tm = TOKEN_TILE

    h = _normmod(tok, x, norm_g[0, 0][None], modr, 0, 1, 0)
    rw_states, hg_states = [], []
    for l in range(depth):
        bw = lambda a: a.astype(BF16)
        x, h = _ffn(tok, h, x, modr, l, 2, bw(ffn_w1[l, 0]), bw(ffn_w3[l, 0]), bw(ffn_w2[l, 0]),
                    norm_g[l, 1][None], (modr, _mod_spec(tok, tm, l, 4, 2)),
                    (modr, _mod_spec(tok, tm, l, 3, 2)), BF16)

        w_in_l = bw(w_in[l])
        za = _project(h, w_in_l[:, :A_COLS], 512)
        zb = _project(h, w_in_l[:, A_COLS:A_COLS + B_COLS], 896)
        zc = _project(h, w_in_l[:, A_COLS + B_COLS:], 768)

        oa = _gmlp(za, mlp_norm_g[l], mlp_ws[l], mlp_bs[l])

        r, v, kk, lw, kd, b, gate, bonus = _rwkv_prep(
            tok, zb, rwkv_mu[l][None], rwkv_w0[l], rwkv_w2[l], rwkv_a0[l], rwkv_a2[l], rwkv_g2[l],
            rwkv_kk[l][None], rwkv_ka[l][None], rwkv_rk[l].reshape(1, B_WIDTH))
        s0c = jnp.zeros((n_ctx, 2, B_HEADS, B_HDIM, B_HDIM), F32)
        yfc, ybc, s_rw = _rwkv_scan(r, v, kk, lw, kd, b, s0c, 0, n_ctx, ctx_len, tok.total)
        yfl, ybl, _ = _rwkv_scan(r, v, kk, lw, kd, b, state_rwkv[:, l], tok.ctx_tokens, n_lat, lat_len,
                                 tok.total)
        ob = _rwkv_post(jnp.concatenate([yfc, yfl], 0), jnp.concatenate([ybc, ybl], 0), bonus, gate,
                        rwkv_lnx_g[l].reshape(1, B_WIDTH), rwkv_lnx_b[l].reshape(1, B_WIDTH))
        rw_states.append(s_rw)

        s0h = jnp.zeros((n_ctx, 2, C_HEADS, C_KDIM, C_KDIM), F32)
        ofc, obc, s_hg = _hgrn_scan(zc, lower[l], s0h, 0, n_ctx, ctx_len)
        ofl, obl, _ = _hgrn_scan(zc, lower[l], state_hgrn[:, l], tok.ctx_tokens, n_lat, lat_len)
        oc = _hgrn_post(jnp.concatenate([ofc, ofl], 0), jnp.concatenate([obc, obl], 0), zc,
                        hgrn_gn[l][None])
        hg_states.append(s_hg)

        w_out_l = bw(w_out[l])
        x, h = _mix_out(tok, oa, ob, oc, w_out_l[:A_WIDTH], w_out_l[A_WIDTH:A_WIDTH + B_WIDTH],
                        w_out_l[A_WIDTH + B_WIDTH:], x, modr, l, norm_g[l, 2][None])

        if l + 1 < depth:
            x, h = _ffn(tok, h, x, modr, l, 8, bw(ffn_w1[l, 1]), bw(ffn_w3[l, 1]), bw(ffn_w2[l, 1]),
                        norm_g[l + 1, 0][None], (modr, _mod_spec(tok, tm, l + 1, 1, 2)),
                        (modr, _mod_spec(tok, tm, l + 1, 0, 2)), BF16)
        else:
            x, h = _ffn(tok, h, x, modr, l, 8, bw(ffn_w1[l, 1]), bw(ffn_w3[l, 1]), bw(ffn_w2[l, 1]),
                        final_g[None], (zero_sc, zero_spec), (zero_sc, zero_spec), F32)

    y_prompt = h[:tok.ctx_tokens].reshape(x_prompt.shape)
    y_sample = h[tok.ctx_tokens:].reshape(x_sample.shape)
    return (y_prompt, y_sample, jnp.stack(rw_states, axis=1), jnp.stack(hg_states, axis=1))
```

```python
import functools

import jax
import jax.numpy as jnp
from jax import lax
from jax.experimental import pallas as pl
from jax.experimental.pallas import tpu as pltpu

F32 = jnp.float32
BF16 = jnp.bfloat16

D_MODEL = 2048
D_FF = 5632
N_MOD = 9
EPS = 1e-6
LNX_EPS = 64e-5

CHUNK_MLP = 128
A_WIDTH = 512
A_GROUPS = 4
A_GDIM = 128
B_HDIM = 64
B_WIDTH = 768
B_HEADS = 12
LORA = 64
GATE_LORA = 128
B_COLS = 3 * B_WIDTH + 4 * LORA + GATE_LORA
C_KDIM = 128
C_WIDTH = 768
C_HEADS = 6
C_COLS = 5 * C_WIDTH
A_COLS = 2 * A_WIDTH

LANES = 128
VMEM_LIMIT = 56 * 1024 * 1024

TOKEN_TILE = 512
FF_TILE = 512
PREP_TILE = 256
RW_CHUNK = 64
HG_BLOCK = 64
HG_SUB = 16


def _cparams(*sem):
    return pltpu.CompilerParams(dimension_semantics=sem, vmem_limit_bytes=VMEM_LIMIT)


def _dot(a, b):
    return jnp.dot(a, b, preferred_element_type=F32)


def _dot_nt(a, b):
    return lax.dot_general(a, b, (((1,), (1,)), ((), ())), preferred_element_type=F32)


def _dot_tn(a, b):
    return lax.dot_general(a, b, (((0,), (0,)), ((), ())), preferred_element_type=F32)


def _split(x, parts):
    out = []
    for _ in range(parts - 1):
        hi = x.astype(BF16)
        out.append(hi)
        x = x - hi.astype(F32)
    out.append(x.astype(BF16))
    return out


def _exact_left(m, x, parts):
    acc = None
    for p in _split(x, parts):
        t = _dot(m, p)
        acc = t if acc is None else acc + t
    return acc


def _exact_right(x, m, parts):
    acc = None
    for p in _split(x, parts):
        t = _dot(p, m)
        acc = t if acc is None else acc + t
    return acc


def _iota(shape, dim):
    return lax.broadcasted_iota(jnp.int32, shape, dim)


def _sigmoid(x):
    return jax.nn.sigmoid(x)


def _silu(x):
    return x * jax.nn.sigmoid(x)


def _norm_mod(x, g, sc, sh):
    ms = jnp.mean(x * x, axis=-1, keepdims=True)
    return (x * lax.rsqrt(ms + EPS) * g) * (1.0 + sc) + sh


def _mod_kernel(cond_ref, w_ref, b_ref, o_ref):
    s = _silu(cond_ref[...]).astype(BF16)
    o_ref[...] = _dot(s, w_ref[...].astype(BF16)) + b_ref[...]


def _mod_rows(cond8, w_mod, b_mod):
    depth, _, n = w_mod.shape
    tn = 1024
    return pl.pallas_call(
        _mod_kernel,
        grid=(depth, n // tn),
        in_specs=[
            pl.BlockSpec((8, D_MODEL), lambda l, j: (0, 0)),
            pl.BlockSpec((None, D_MODEL, tn), lambda l, j: (l, 0, j)),
            pl.BlockSpec((None, 1, tn), lambda l, j: (l, 0, j)),
        ],
        out_specs=pl.BlockSpec((None, 8, tn), lambda l, j: (l, 0, j)),
        out_shape=jax.ShapeDtypeStruct((depth, 8, n), F32),
        compiler_params=_cparams("parallel", "parallel"),
        name="adaln_rows",
    )(cond8, w_mod, b_mod.reshape(depth, 1, n))


class _Tokens:
    def __init__(self, n_ctx, ctx_len, n_lat, lat_len):
        self.n_ctx, self.ctx_len, self.n_lat, self.lat_len = n_ctx, ctx_len, n_lat, lat_len
        self.ctx_tokens = n_ctx * ctx_len
        self.total = self.ctx_tokens + n_lat * lat_len

    def mod_row(self, tile, i):
        nct = self.ctx_tokens // tile
        per_lat = self.lat_len // tile
        return jnp.where(i < nct, 0, 1 + (i - nct) // per_lat)


def _mod_spec(tok, tile, layer, which, grid_rank):
    def imap(*idx):
        return (layer, which, tok.mod_row(tile, idx[0]), 0, 0)
    del grid_rank
    return pl.BlockSpec((None, None, None, 1, D_MODEL), imap)


def _normmod_kernel(x_ref, g_ref, sc_ref, sh_ref, h_ref):
    h_ref[...] = _norm_mod(x_ref[...], g_ref[...], sc_ref[...], sh_ref[...]).astype(h_ref.dtype)


def _normmod(tok, x, norm_g_row, modr, layer, sc_i, sh_i):
    tm = TOKEN_TILE
    return pl.pallas_call(
        _normmod_kernel,
        grid=(tok.total // tm,),
        in_specs=[
            pl.BlockSpec((tm, D_MODEL), lambda i: (i, 0)),
            pl.BlockSpec((1, D_MODEL), lambda i: (0, 0)),
            _mod_spec(tok, tm, layer, sc_i, 1),
            _mod_spec(tok, tm, layer, sh_i, 1),
        ],
        out_specs=pl.BlockSpec((tm, D_MODEL), lambda i: (i, 0)),
        out_shape=jax.ShapeDtypeStruct((tok.total, D_MODEL), BF16),
        compiler_params=_cparams("parallel"),
        name="first_norm",
    )(x, norm_g_row, modr, modr)


def _ffn_kernel(h_ref, x_ref, gt_ref, w1_ref, w3_ref, w2_ref, gn_ref, scn_ref, shn_ref,
                xo_ref, ho_ref, acc_ref, *, n_ff):
    j = pl.program_id(1)

    @pl.when(j == 0)
    def _():
        acc_ref[...] = jnp.zeros_like(acc_ref)

    h = h_ref[...]
    a = _dot(h, w1_ref[...])
    b = _dot(h, w3_ref[...])
    p = (_silu(a) * b).astype(BF16)
    acc_ref[...] += _dot(p, w2_ref[...])

    @pl.when(j == n_ff - 1)
    def _():
        xn = x_ref[...] + 0.5 * gt_ref[...] * acc_ref[...]
        xo_ref[...] = xn
        ho_ref[...] = _norm_mod(xn, gn_ref[...], scn_ref[...], shn_ref[...]).astype(ho_ref.dtype)


def _ffn(tok, h, x, modr, layer, gate_i, w1, w3, w2, next_g, next_sc, next_sh, next_dtype):
    tm, tf = TOKEN_TILE, FF_TILE
    n_ff = D_FF // tf
    row = pl.BlockSpec((tm, D_MODEL), lambda i, j: (i, 0))
    return pl.pallas_call(
        functools.partial(_ffn_kernel, n_ff=n_ff),
        grid=(tok.total // tm, n_ff),
        in_specs=[
            row, row,
            _mod_spec(tok, tm, layer, gate_i, 2),
            pl.BlockSpec((D_MODEL, tf), lambda i, j: (0, j)),
            pl.BlockSpec((D_MODEL, tf), lambda i, j: (0, j)),
            pl.BlockSpec((tf, D_MODEL), lambda i, j: (j, 0)),
            pl.BlockSpec((1, D_MODEL), lambda i, j: (0, 0)),
            next_sc[1], next_sh[1],
        ],
        out_specs=[row, row],
        out_shape=[jax.ShapeDtypeStruct((tok.total, D_MODEL), F32),
                   jax.ShapeDtypeStruct((tok.total, D_MODEL), next_dtype)],
        scratch_shapes=[pltpu.VMEM((tm, D_MODEL), F32)],
        compiler_params=_cparams("parallel", "arbitrary"),
        name="swiglu_half_step",
    )(h, x, modr, w1, w3, w2, next_g, next_sc[0], next_sh[0])


def _matmul_kernel(h_ref, w_ref, o_ref):
    o_ref[...] = _dot(h_ref[...], w_ref[...])


def _project(h, w, tn):
    t, k = h.shape
    n = w.shape[1]
    tm = TOKEN_TILE
    return pl.pallas_call(
        _matmul_kernel,
        grid=(n // tn, t // tm),
        in_specs=[pl.BlockSpec((tm, k), lambda j, i: (i, 0)),
                  pl.BlockSpec((k, tn), lambda j, i: (0, j))],
        out_specs=pl.BlockSpec((tm, tn), lambda j, i: (i, j)),
        out_shape=jax.ShapeDtypeStruct((t, n), F32),
        compiler_params=_cparams("parallel", "parallel"),
        name="mix_in_proj",
    )(h, w)


def _gmlp_kernel(za_ref, ng_ref, ws_ref, bs_ref, o_ref):
    z = jax.nn.gelu(za_ref[...])
    outs = []
    for g in range(A_GROUPS):
        u = z[:, g * A_GDIM:(g + 1) * A_GDIM]
        v = z[:, A_WIDTH + g * A_GDIM:A_WIDTH + (g + 1) * A_GDIM]
        ms = jnp.mean(v * v, axis=-1, keepdims=True)
        v = v * lax.rsqrt(ms + EPS) * ng_ref[g:g + 1, :]
        mixed = _dot(ws_ref[g], v.astype(BF16)) + bs_ref[g]
        outs.append(u * mixed)
    o_ref[...] = jnp.concatenate(outs, axis=-1).astype(o_ref.dtype)


def _gmlp(za, ng, ws, bs):
    t = za.shape[0]
    return pl.pallas_call(
        _gmlp_kernel,
        grid=(t // CHUNK_MLP,),
        in_specs=[
            pl.BlockSpec((CHUNK_MLP, A_COLS), lambda i: (i, 0)),
            pl.BlockSpec((A_GROUPS, A_GDIM), lambda i: (0, 0)),
            pl.BlockSpec((A_GROUPS, CHUNK_MLP, CHUNK_MLP), lambda i: (0, 0, 0)),
            pl.BlockSpec((A_GROUPS, CHUNK_MLP, 1), lambda i: (0, 0, 0)),
        ],
        out_specs=pl.BlockSpec((CHUNK_MLP, A_WIDTH), lambda i: (i, 0)),
        out_shape=jax.ShapeDtypeStruct((t, A_WIDTH), BF16),
        compiler_params=_cparams("parallel"),
        name="gmlp_chunk_mix",
    )(za, ng, ws.astype(BF16), bs[:, :, None])


def _head_ones():
    r = _iota((LANES, LANES), 0) // B_HDIM
    c = _iota((LANES, LANES), 1) // B_HDIM
    return (r == c).astype(BF16)


def _head_sum(x, parts=2):
    ones = _head_ones()
    outs = []
    for s in range(x.shape[-1] // LANES):
        outs.append(_exact_right(x[:, s * LANES:(s + 1) * LANES], ones, parts))
    return jnp.concatenate(outs, axis=-1)


def _rwkv_prep_kernel(z_ref, zp_ref, zn_ref, mu_ref, w0_ref, w2_ref, a0_ref, a2_ref, g2_ref,
                      kkw_ref, kaw_ref, rk_ref,
                      r_ref, v_ref, kk_ref, lw_ref, kd_ref, b_ref, gate_ref, bonus_ref,
                      *, n_ctx_tiles, ctx_tiles_per_seq, lat_tiles_per_seq):
    i = pl.program_id(0)
    pos = jnp.where(i < n_ctx_tiles, i % ctx_tiles_per_seq, (i - n_ctx_tiles) % lat_tiles_per_seq)
    last = jnp.where(i < n_ctx_tiles, ctx_tiles_per_seq - 1, lat_tiles_per_seq - 1)
    z = z_ref[...]
    t = z.shape[0]
    row = _iota((t, 1), 0)
    halo_prev = jnp.where(pos != 0, zp_ref[7:8, :], 0.0)
    halo_next = jnp.where(pos != last, zn_ref[0:1, :], 0.0)
    prev = jnp.where(row == 0, halo_prev, pltpu.roll(z, 1, 0))
    nxt = jnp.where(row == t - 1, halo_next, pltpu.roll(z, t - 1, 0))
    z = z + (0.5 * (prev + nxt) - z) * mu_ref[...]

    w = B_WIDTH
    r, k, v = z[:, 0:w], z[:, w:2 * w], z[:, 2 * w:3 * w]
    wd = z[:, 3 * w:3 * w + 2 * LORA]
    ad = z[:, 3 * w + 2 * LORA:3 * w + 4 * LORA]
    gd = z[:, 3 * w + 4 * LORA:]
    twd = jnp.tanh(wd).astype(BF16)
    adb = ad.astype(BF16)

    kk = k * kkw_ref[...]
    nrm = jnp.sqrt(_head_sum(kk * kk))
    kk = kk / jnp.maximum(nrm, 1e-12)

    r_ref[...] = r
    v_ref[...] = v
    kk_ref[...] = kk
    gate_ref[...] = _dot(_sigmoid(gd).astype(BF16), g2_ref[...])
    ksum = jnp.zeros_like(k)
    for d in range(2):
        w_raw = w0_ref[d:d + 1, :] + _dot(twd[:, d * LORA:(d + 1) * LORA], w2_ref[d])
        lw_ref[d] = -jnp.exp(-jax.nn.softplus(-w_raw) - 0.5)
        a = _sigmoid(a0_ref[d:d + 1, :] + _dot(adb[:, d * LORA:(d + 1) * LORA], a2_ref[d]))
        kd = k * (1.0 + (a - 1.0) * kaw_ref[...])
        kd_ref[d] = kd
        b_ref[d] = kk * a
        ksum = ksum + kd
    bonus_ref[...] = _head_sum(r * ksum * rk_ref[...]) * v


def _rwkv_prep(tok, zb, mu, w0, w2, a0, a2, g2, kkw, kaw, rk):
    tp = PREP_TILE
    n_tiles = tok.total // tp
    rows8 = tok.total // 8
    per = tp // 8
    full = lambda shape: pl.BlockSpec(shape, lambda i: (0,) * len(shape))
    tile = pl.BlockSpec((tp, B_WIDTH), lambda i: (i, 0))
    tile2 = pl.BlockSpec((2, tp, B_WIDTH), lambda i: (0, i, 0))
    o1 = jax.ShapeDtypeStruct((tok.total, B_WIDTH), F32)
    o2 = jax.ShapeDtypeStruct((2, tok.total, B_WIDTH), F32)
    kern = functools.partial(_rwkv_prep_kernel, n_ctx_tiles=tok.ctx_tokens // tp,
                             ctx_tiles_per_seq=tok.ctx_len // tp, lat_tiles_per_seq=tok.lat_len // tp)
    return pl.pallas_call(
        kern,
        grid=(n_tiles,),
        in_specs=[
            pl.BlockSpec((tp, B_COLS), lambda i: (i, 0)),
            pl.BlockSpec((8, B_COLS), lambda i: (jnp.maximum(i * per - 1, 0), 0)),
            pl.BlockSpec((8, B_COLS), lambda i: (jnp.minimum((i + 1) * per, rows8 - 1), 0)),
            full((1, B_COLS)), full((2, B_WIDTH)), full((2, LORA, B_WIDTH)), full((2, B_WIDTH)),
            full((2, LORA, B_WIDTH)), full((GATE_LORA, B_WIDTH)),
            full((1, B_WIDTH)), full((1, B_WIDTH)), full((1, B_WIDTH)),
        ],
        out_specs=[tile, tile, tile, tile2, tile2, tile2, tile, tile],
        out_shape=[o1, o1, o1, o2, o2, o2, o1, o1],
        compiler_params=_cparams("parallel"),
        name="rwkv_prep",
    )(zb, zb, zb, mu, w0, w2.astype(BF16), a0, a2.astype(BF16), g2.astype(BF16), kkw, kaw, rk)


def _pair_blockdiag(x):
    lane = _iota(x.shape, 1)
    zero = jnp.zeros_like(x)
    return jnp.concatenate([jnp.where(lane < B_HDIM, x, zero), jnp.where(lane >= B_HDIM, x, zero)], axis=0)


def _rwkv_chunks(streams, s_ref):
    c = streams[0][0].shape[0]
    n_pairs = B_HEADS // 2
    ri, ci = _iota((c, c), 0), _iota((c, c), 1)
    rr, cc = _iota((2 * c, 2 * c), 0), _iota((2 * c, 2 * c), 1)
    same = (rr // c) == (cc // c)
    tt, ss = rr % c, cc % c
    eye = (rr == cc).astype(F32)
    pair_masks = []
    n = 1
    while n < c:
        pair_masks.append(same & ((tt // (2 * n)) == (ss // (2 * n))) & ((tt // n) != (ss // n)))
        n *= 2

    units = []
    for d, (r, v, kk, lw, kd, b, reverse) in enumerate(streams):
        upto_c = (ci >= ri) if reverse else (ci <= ri)
        g = _exact_left(upto_c.astype(BF16), lw, 3)
        g_last = g[0:1, :] if reverse else g[c - 1:c, :]
        eg = jnp.exp(g)
        ieg = jnp.exp(-g)
        a_t = (-kk * jnp.exp(g - lw)).astype(BF16)
        r_t = (r * eg).astype(BF16)
        b_t = (b * ieg).astype(BF16)
        k_t = (kd * ieg).astype(BF16)
        vb = v.astype(BF16)
        gl = jnp.exp(g_last)
        before = same & ((ss > tt) if reverse else (ss < tt))
        upto = same & ((ss >= tt) if reverse else (ss <= tt))
        for p in range(n_pairs):
            sl = slice(p * LANES, (p + 1) * LANES)
            units.append(dict(d=d, p=p, before=before, upto=upto, gl=gl[:, sl],
                              ar=jnp.concatenate([_pair_blockdiag(a_t[:, sl]), _pair_blockdiag(r_t[:, sl])], 0),
                              bk=jnp.concatenate([_pair_blockdiag(b_t[:, sl]), _pair_blockdiag(k_t[:, sl])], 0),
                              v=_pair_blockdiag(vb[:, sl])))
    m = 2 * c
    for u in units:
        u["gram"] = _dot_nt(u["ar"], u["bk"])
        u["s0"] = s_ref[u["d"], u["p"]]
    for u in units:
        u["ws"] = _dot_nt(u["ar"], u["s0"].astype(BF16))
        g = u["gram"]
        u["lmat"] = jnp.where(u["before"], g[:m, :m], 0.0)
        aak = jnp.where(u["before"], g[:m, m:], 0.0).astype(BF16)
        u["rbk"] = jnp.concatenate([jnp.where(u["upto"], g[m:, :m], 0.0),
                                    jnp.where(u["upto"], g[m:, m:], 0.0)], axis=1).astype(BF16)
        u["x"] = u["ws"][:m] + _dot(aak, u["v"])
    for lvl, pm in enumerate(pair_masks):
        for u in units:
            link = jnp.where(pm, u["lmat"], 0.0)
            if lvl == 0:
                u["tinv"] = eye + link
            else:
                u["tb"] = u["tinv"].astype(BF16)
                u["tmp"] = _dot(link.astype(BF16), u["tb"]).astype(BF16)
        if lvl > 0:
            for u in units:
                u["tinv"] = u["tinv"] + _dot(u["tb"], u["tmp"])
    for u in units:
        u["uv"] = jnp.concatenate([_dot(u["tinv"].astype(BF16), u["x"].astype(BF16)).astype(BF16), u["v"]], 0)
    ys = [[None] * n_pairs for _ in streams]
    for u in units:
        y = u["ws"][m:] + _dot(u["rbk"], u["uv"])
        ys[u["d"]][u["p"]] = y[:c] + y[c:]
        s_ref[u["d"], u["p"]] = (u["s0"] + _dot_tn(u["uv"], u["bk"])) * u["gl"]
    return [jnp.concatenate(row, axis=-1) for row in ys]


def _rwkv_scan_kernel(rf_ref, vf_ref, kkf_ref, lwf_ref, kdf_ref, bf_ref,
                      rb_ref, vb_ref, kkb_ref, lwb_ref, kdb_ref, bb_ref, s0_ref,
                      yf_ref, yb_ref, sfin_ref, s_ref, *, n_chunks):
    c = pl.program_id(1)
    hd = B_HDIM

    @pl.when(c == 0)
    def _():
        z = jnp.zeros((hd, hd), F32)
        for d in range(2):
            for p in range(B_HEADS // 2):
                s_ref[d, p] = jnp.concatenate(
                    [jnp.concatenate([s0_ref[d, 2 * p], z], axis=1),
                     jnp.concatenate([z, s0_ref[d, 2 * p + 1]], axis=1)], axis=0)

    yf, yb = _rwkv_chunks(
        [(rf_ref[...], vf_ref[...], kkf_ref[...], lwf_ref[...], kdf_ref[...], bf_ref[...], False),
         (rb_ref[...], vb_ref[...], kkb_ref[...], lwb_ref[...], kdb_ref[...], bb_ref[...], True)], s_ref)
    yf_ref[...] = yf
    yb_ref[...] = yb

    @pl.when(c == n_chunks - 1)
    def _():
        for d in range(2):
            for p in range(B_HEADS // 2):
                s = s_ref[d, p]
                sfin_ref[d, 2 * p] = s[:hd, :hd]
                sfin_ref[d, 2 * p + 1] = s[hd:, hd:]


def _rwkv_scan(r, v, kk, lw, kd, b, s0, tok_off, n_seq, seq_len, total):
    cl = RW_CHUNK
    nc = seq_len // cl
    off = tok_off // cl
    fwd = lambda s, c: (off + s * nc + c, 0)
    bwd = lambda s, c: (off + s * nc + (nc - 1 - c), 0)
    fwd2 = lambda d: (lambda s, c: (d, off + s * nc + c, 0))
    bwd2 = lambda d: (lambda s, c: (d, off + s * nc + (nc - 1 - c), 0))
    blk = lambda im: pl.BlockSpec((cl, B_WIDTH), im)
    blk2 = lambda im: pl.BlockSpec((None, cl, B_WIDTH), im)
    st = pl.BlockSpec((None, 2, B_HEADS, B_HDIM, B_HDIM), lambda s, c: (s, 0, 0, 0, 0))
    n_rows = n_seq * seq_len
    del total
    yshape = jax.ShapeDtypeStruct((n_rows, B_WIDTH), F32)
    local_f = lambda s, c: (s * nc + c, 0)
    local_b = lambda s, c: (s * nc + (nc - 1 - c), 0)
    return pl.pallas_call(
        functools.partial(_rwkv_scan_kernel, n_chunks=nc),
        grid=(n_seq, nc),
        in_specs=[blk(fwd), blk(fwd), blk(fwd), blk2(fwd2(0)), blk2(fwd2(0)), blk2(fwd2(0)),
                  blk(bwd), blk(bwd), blk(bwd), blk2(bwd2(1)), blk2(bwd2(1)), blk2(bwd2(1)), st],
        out_specs=[pl.BlockSpec((cl, B_WIDTH), local_f), pl.BlockSpec((cl, B_WIDTH), local_b), st],
        out_shape=[yshape, yshape, jax.ShapeDtypeStruct(s0.shape, F32)],
        scratch_shapes=[pltpu.VMEM((2, B_HEADS // 2, LANES, LANES), F32)],
        compiler_params=_cparams("parallel", "arbitrary"),
        name="rwkv7_scan",
    )(r, v, kk, lw, kd, b, r, v, kk, lw, kd, b, s0)


def _rwkv_post_kernel(yf_ref, yb_ref, bonus_ref, gate_ref, g_ref, b_ref, o_ref):
    y = yf_ref[...] + yb_ref[...]
    mean = _head_sum(y, 3) * (1.0 / B_HDIM)
    yc = y - mean
    var = _head_sum(yc * yc) * (1.0 / B_HDIM)
    y = yc * lax.rsqrt(var + LNX_EPS) * g_ref[...] + b_ref[...] + bonus_ref[...]
    o_ref[...] = (y * gate_ref[...]).astype(o_ref.dtype)


def _rwkv_post(yf, yb, bonus, gate, lnx_g, lnx_b):
    t = yf.shape[0]
    tm = TOKEN_TILE
    tile = pl.BlockSpec((tm, B_WIDTH), lambda i: (i, 0))
    vec = pl.BlockSpec((1, B_WIDTH), lambda i: (0, 0))
    return pl.pallas_call(
        _rwkv_post_kernel,
        grid=(t // tm,),
        in_specs=[tile, tile, tile, tile, vec, vec],
        out_specs=tile,
        out_shape=jax.ShapeDtypeStruct((t, B_WIDTH), BF16),
        compiler_params=_cparams("parallel"),
        name="rwkv_post",
    )(yf, yb, bonus, gate, lnx_g, lnx_b)


def _hgrn_block(qraw, fraw, v, lb, st_ref, reverse):
    n = qraw.shape[0]
    sub = HG_SUB
    fg = lb + (1.0 - lb) * _sigmoid(fraw)
    logf = jnp.log(fg)
    kg = 1.0 - fg
    q = _silu(qraw)
    ri, ci = _iota((n, n), 0), _iota((n, n), 1)
    same = (ri // sub) == (ci // sub)
    upto = same & ((ci >= ri) if reverse else (ci <= ri))
    bcum = _exact_left(upto.astype(BF16), logf, 3)
    ones = jnp.ones((LANES, LANES), BF16)
    si = _iota((sub, 1), 0)
    order = range(n // sub - 1, -1, -1) if reverse else range(n // sub)
    out_rows = [None] * (n // sub)
    for sc in order:
        rows = slice(sc * sub, (sc + 1) * sub)
        last = sc * sub if reverse else (sc + 1) * sub - 1
        outs = []
        for h in range(C_HEADS):
            cols = slice(h * C_KDIM, (h + 1) * C_KDIM)
            qs, ks, vs, bs = q[rows, cols], kg[rows, cols], v[rows, cols], bcum[rows, cols]
            b_end = bcum[last:last + 1, cols]
            slabs = []
            for j in range(sub):
                live = (si <= j) if reverse else (si >= j)
                e = jnp.exp(jnp.where(live, bs - bs[j:j + 1, :], -1e30))
                slabs.append(qs * e * ks[j:j + 1, :])
            att = _dot(jnp.concatenate(slabs, axis=0).astype(BF16), ones)
            o = _dot_nt((qs * jnp.exp(bs)).astype(BF16), st_ref[h].astype(BF16))
            for j in range(sub):
                o = o + att[j * sub:(j + 1) * sub, :] * vs[j:j + 1, :]
            outs.append(o)
            ke = (ks * jnp.exp(b_end - bs)).astype(BF16)
            st_ref[h] = st_ref[h] * jnp.exp(b_end) + _dot_tn(vs.astype(BF16), ke)
        out_rows[sc] = jnp.concatenate(outs, axis=-1)
    return jnp.concatenate(out_rows, axis=0)


def _hgrn_scan_kernel(qf_ref, ff_ref, vf_ref, qb_ref, fb_ref, vb_ref, lb_ref, s0_ref,
                      of_ref, ob_ref, sfin_ref, st_ref, *, n_chunks):
    c = pl.program_id(1)

    @pl.when(c == 0)
    def _():
        for d in range(2):
            for h in range(C_HEADS):
                st_ref[d, h] = s0_ref[d, h].T

    of_ref[...] = _hgrn_block(qf_ref[...], ff_ref[...], vf_ref[...], lb_ref[0:1, :], st_ref.at[0], False)
    ob_ref[...] = _hgrn_block(qb_ref[...], fb_ref[...], vb_ref[...], lb_ref[1:2, :], st_ref.at[1], True)

    @pl.when(c == n_chunks - 1)
    def _():
        for d in range(2):
            for h in range(C_HEADS):
                sfin_ref[d, h] = st_ref[d, h].T


def _hgrn_scan(zc, lb, s0, tok_off, n_seq, seq_len):
    cl = HG_BLOCK
    nc = seq_len // cl
    off = tok_off // cl
    fwd = lambda col: (lambda s, c: (off + s * nc + c, col))
    bwd = lambda col: (lambda s, c: (off + s * nc + (nc - 1 - c), col))
    blk = lambda im: pl.BlockSpec((cl, C_WIDTH), im)
    st = pl.BlockSpec((None, 2, C_HEADS, C_KDIM, C_KDIM), lambda s, c: (s, 0, 0, 0, 0))
    n_rows = n_seq * seq_len
    oshape = jax.ShapeDtypeStruct((n_rows, C_WIDTH), F32)
    return pl.pallas_call(
        functools.partial(_hgrn_scan_kernel, n_chunks=nc),
        grid=(n_seq, nc),
        in_specs=[blk(fwd(0)), blk(fwd(1)), blk(fwd(3)), blk(bwd(0)), blk(bwd(2)), blk(bwd(3)),
                  pl.BlockSpec((2, C_WIDTH), lambda s, c: (0, 0)), st],
        out_specs=[pl.BlockSpec((cl, C_WIDTH), lambda s, c: (s * nc + c, 0)),
                   pl.BlockSpec((cl, C_WIDTH), lambda s, c: (s * nc + (nc - 1 - c), 0)), st],
        out_shape=[oshape, oshape, jax.ShapeDtypeStruct(s0.shape, F32)],
        scratch_shapes=[pltpu.VMEM((2, C_HEADS, C_KDIM, C_KDIM), F32)],
        compiler_params=_cparams("parallel", "arbitrary"),
        name="hgrn2_scan",
    )(zc, zc, zc, zc, zc, zc, lb, s0)


def _hgrn_post_kernel(of_ref, ob_ref, g_ref, gn_ref, o_ref):
    o = of_ref[...] + ob_ref[...]
    gate = _silu(g_ref[...])
    outs = []
    for h in range(C_HEADS):
        cols = slice(h * C_KDIM, (h + 1) * C_KDIM)
        oh = o[:, cols]
        ms = jnp.mean(oh * oh, axis=-1, keepdims=True)
        outs.append(oh * lax.rsqrt(ms + EPS) * gn_ref[...] * gate[:, cols])
    o_ref[...] = jnp.concatenate(outs, axis=-1).astype(o_ref.dtype)


def _hgrn_post(of, ob, zc, gn):
    t = of.shape[0]
    tm = TOKEN_TILE
    tile = pl.BlockSpec((tm, C_WIDTH), lambda i: (i, 0))
    return pl.pallas_call(
        _hgrn_post_kernel,
        grid=(t // tm,),
        in_specs=[tile, tile, pl.BlockSpec((tm, C_WIDTH), lambda i: (i, 4)),
                  pl.BlockSpec((1, C_KDIM), lambda i: (0, 0))],
        out_specs=tile,
        out_shape=jax.ShapeDtypeStruct((t, C_WIDTH), BF16),
        compiler_params=_cparams("parallel"),
        name="hgrn2_post",
    )(of, ob, zc, gn)


def _mix_out_kernel(oa_ref, ob_ref, oc_ref, wa_ref, wb_ref, wc_ref, x_ref, gt_ref,
                    gn_ref, scn_ref, shn_ref, xo_ref, ho_ref):
    y = _dot(oa_ref[...], wa_ref[...]) + _dot(ob_ref[...], wb_ref[...]) + _dot(oc_ref[...], wc_ref[...])
    xn = x_ref[...] + gt_ref[...] * y
    xo_ref[...] = xn
    ho_ref[...] = _norm_mod(xn, gn_ref[...], scn_ref[...], shn_ref[...]).astype(ho_ref.dtype)


def _mix_out(tok, oa, ob, oc, wa, wb, wc, x, modr, layer, next_g):
    tm = TOKEN_TILE
    row = pl.BlockSpec((tm, D_MODEL), lambda i: (i, 0))
    full = lambda a: pl.BlockSpec(a.shape, lambda i: (0, 0))
    return pl.pallas_call(
        _mix_out_kernel,
        grid=(tok.total // tm,),
        in_specs=[pl.BlockSpec((tm, A_WIDTH), lambda i: (i, 0)),
                  pl.BlockSpec((tm, B_WIDTH), lambda i: (i, 0)),
                  pl.BlockSpec((tm, C_WIDTH), lambda i: (i, 0)),
                  full(wa), full(wb), full(wc), row,
                  _mod_spec(tok, tm, layer, 5, 1),
                  pl.BlockSpec((1, D_MODEL), lambda i: (0, 0)),
                  _mod_spec(tok, tm, layer, 7, 1), _mod_spec(tok, tm, layer, 6, 1)],
        out_specs=[row, row],
        out_shape=[jax.ShapeDtypeStruct((tok.total, D_MODEL), F32),
                   jax.ShapeDtypeStruct((tok.total, D_MODEL), BF16)],
        compiler_params=_cparams("parallel"),
        name="mix_out_proj",
    )(oa, ob, oc, wa, wb, wc, x, modr, next_g, modr, modr)


def kernel(x_prompt, x_sample, state_rwkv, state_hgrn, c, c_ctx, norm_g, w_mod, b_mod, ffn_w1, ffn_w3,
           ffn_w2, w_in, w_out, mlp_norm_g, mlp_ws, mlp_bs, rwkv_mu, rwkv_w0, rwkv_w2, rwkv_a0, rwkv_a2,
           rwkv_g2, rwkv_kk, rwkv_ka, rwkv_rk, rwkv_lnx_g, rwkv_lnx_b, hgrn_lb, hgrn_gn, final_g):
    n_ctx, ctx_len, _ = x_prompt.shape
    n_lat, lat_len, _ = x_sample.shape
    depth = w_mod.shape[0]
    tok = _Tokens(n_ctx, ctx_len, n_lat, lat_len)
    assert n_lat + 1 <= 8 and ctx_len % PREP_TILE == 0 and lat_len % TOKEN_TILE == 0

    cond8 = jnp.zeros((8, D_MODEL), F32).at[0].set(c_ctx).at[1:1 + n_lat].set(c)
    mod = _mod_rows(cond8, w_mod, b_mod)
    modr = mod.reshape(depth, 8, N_MOD, D_MODEL).transpose(0, 2, 1, 3)[:, :, :, None, :]

    sm = jax.nn.softmax(hgrn_lb.astype(F32), axis=0)
    lower = jnp.cumsum(sm, axis=0) - sm[0]

    x = jnp.concatenate([x_prompt.reshape(-1, D_MODEL), x_sample.reshape(-1, D_MODEL)], axis=0)
    zero_sc = jnp.zeros((1, D_MODEL), F32)
    zero_spec = pl.BlockSpec((1, D_MODEL), lambda i, j: (0, 0))
    tm = TOKEN_TILE

    h = _normmod(tok, x, norm_g[0, 0][None], modr, 0, 1, 0)
    rw_states, hg_states = [], []
    for l in range(depth):
        bw = lambda a: a.astype(BF16)
        x, h = _ffn(tok, h, x, modr, l, 2, bw(ffn_w1[l, 0]), bw(ffn_w3[l, 0]), bw(ffn_w2[l, 0]),
                    norm_g[l, 1][None], (modr, _mod_spec(tok, tm, l, 4, 2)),
                    (modr, _mod_spec(tok, tm, l, 3, 2)), BF16)

        w_in_l = bw(w_in[l])
        za = _project(h, w_in_l[:, :A_COLS], 512)
        zb = _project(h, w_in_l[:, A_COLS:A_COLS + B_COLS], 896)
        zc = _project(h, w_in_l[:, A_COLS + B_COLS:], 768)

        oa = _gmlp(za, mlp_norm_g[l], mlp_ws[l], mlp_bs[l])

        r, v, kk, lw, kd, b, gate, bonus = _rwkv_prep(
            tok, zb, rwkv_mu[l][None], rwkv_w0[l], rwkv_w2[l], rwkv_a0[l], rwkv_a2[l], rwkv_g2[l],
            rwkv_kk[l][None], rwkv_ka[l][None], rwkv_rk[l].reshape(1, B_WIDTH))
        s0c = jnp.zeros((n_ctx, 2, B_HEADS, B_HDIM, B_HDIM), F32)
        yfc, ybc, s_rw = _rwkv_scan(r, v, kk, lw, kd, b, s0c, 0, n_ctx, ctx_len, tok.total)
        yfl, ybl, _ = _rwkv_scan(r, v, kk, lw, kd, b, state_rwkv[:, l], tok.ctx_tokens, n_lat, lat_len,
                                 tok.total)
        ob = _rwkv_post(jnp.concatenate([yfc, yfl], 0), jnp.concatenate([ybc, ybl], 0), bonus, gate,
                        rwkv_lnx_g[l].reshape(1, B_WIDTH), rwkv_lnx_b[l].reshape(1, B_WIDTH))
        rw_states.append(s_rw)

        s0h = jnp.zeros((n_ctx, 2, C_HEADS, C_KDIM, C_KDIM), F32)
        ofc, obc, s_hg = _hgrn_scan(zc, lower[l], s0h, 0, n_ctx, ctx_len)
        ofl, obl, _ = _hgrn_scan(zc, lower[l], state_hgrn[:, l], tok.ctx_tokens, n_lat, lat_len)
        oc = _hgrn_post(jnp.concatenate([ofc, ofl], 0), jnp.concatenate([obc, obl], 0), zc,
                        hgrn_gn[l][None])
        hg_states.append(s_hg)

        w_out_l = bw(w_out[l])
        x, h = _mix_out(tok, oa, ob, oc, w_out_l[:A_WIDTH], w_out_l[A_WIDTH:A_WIDTH + B_WIDTH],
                        w_out_l[A_WIDTH + B_WIDTH:], x, modr, l, norm_g[l, 2][None])

        if l + 1 < depth:
            x, h = _ffn(tok, h, x, modr, l, 8, bw(ffn_w1[l, 1]), bw(ffn_w3[l, 1]), bw(ffn_w2[l, 1]),
                        norm_g[l + 1, 0][None], (modr, _mod_spec(tok, tm, l + 1, 1, 2)),
                        (modr, _mod_spec(tok, tm, l + 1, 0, 2)), BF16)
        else:
            x, h = _ffn(tok, h, x, modr, l, 8, bw(ffn_w1[l, 1]), bw(ffn_w3[l, 1]), bw(ffn_w2[l, 1]),
                        final_g[None], (zero_sc, zero_spec), (zero_sc, zero_spec), F32)

    y_prompt = h[:tok.ctx_tokens].reshape(x_prompt.shape)
    y_sample = h[tok.ctx_tokens:].reshape(x_sample.shape)
    return (y_prompt, y_sample, jnp.stack(rw_states, axis=1), jnp.stack(hg_states, axis=1))
```

```python
import functools

import jax
import jax.numpy as jnp
from jax import lax
from jax.experimental import pallas as pl
from jax.experimental.pallas import tpu as pltpu

F32 = jnp.float32
BF16 = jnp.bfloat16

D_MODEL = 2048
D_FF = 5632
N_MOD = 9
EPS = 1e-6
LNX_EPS = 64e-5

CHUNK_MLP = 128
A_WIDTH = 512
A_GROUPS = 4
A_GDIM = 128
B_HDIM = 64
B_WIDTH = 768
B_HEADS = 12
LORA = 64
GATE_LORA = 128
B_COLS = 3 * B_WIDTH + 4 * LORA + GATE_LORA
C_KDIM = 128
C_WIDTH = 768
C_HEADS = 6
C_COLS = 5 * C_WIDTH
A_COLS = 2 * A_WIDTH

LANES = 128
VMEM_LIMIT = 56 * 1024 * 1024

TOKEN_TILE = 512
FF_TILE = 512
PREP_TILE = 256
RW_CHUNK = 64
HG_BLOCK = 64
HG_SUB = 16


def _cparams(*sem):
    return pltpu.CompilerParams(dimension_semantics=sem, vmem_limit_bytes=VMEM_LIMIT)


def _dot(a, b):
    return jnp.dot(a, b, preferred_element_type=F32)


def _dot_nt(a, b):
    return lax.dot_general(a, b, (((1,), (1,)), ((), ())), preferred_element_type=F32)


def _dot_tn(a, b):
    return lax.dot_general(a, b, (((0,), (0,)), ((), ())), preferred_element_type=F32)


def _split(x, parts):
    out = []
    for _ in range(parts - 1):
        hi = x.astype(BF16)
        out.append(hi)
        x = x - hi.astype(F32)
    out.append(x.astype(BF16))
    return out


def _exact_left(m, x, parts):
    acc = None
    for p in _split(x, parts):
        t = _dot(m, p)
        acc = t if acc is None else acc + t
    return acc


def _exact_right(x, m, parts):
    acc = None
    for p in _split(x, parts):
        t = _dot(p, m)
        acc = t if acc is None else acc + t
    return acc


def _iota(shape, dim):
    return lax.broadcasted_iota(jnp.int32, shape, dim)


def _sigmoid(x):
    return jax.nn.sigmoid(x)


def _silu(x):
    return x * jax.nn.sigmoid(x)


def _norm_mod(x, g, sc, sh):
    ms = jnp.mean(x * x, axis=-1, keepdims=True)
    return (x * lax.rsqrt(ms + EPS) * g) * (1.0 + sc) + sh


def _mod_kernel(cond_ref, w_ref, b_ref, o_ref):
    s = _silu(cond_ref[...]).astype(BF16)
    o_ref[...] = _dot(s, w_ref[...].astype(BF16)) + b_ref[...]


def _mod_rows(cond8, w_mod, b_mod):
    depth, _, n = w_mod.shape
    tn = 1024
    return pl.pallas_call(
        _mod_kernel,
        grid=(depth, n // tn),
        in_specs=[
            pl.BlockSpec((8, D_MODEL), lambda l, j: (0, 0)),
            pl.BlockSpec((None, D_MODEL, tn), lambda l, j: (l, 0, j)),
            pl.BlockSpec((None, 1, tn), lambda l, j: (l, 0, j)),
        ],
        out_specs=pl.BlockSpec((None, 8, tn), lambda l, j: (l, 0, j)),
        out_shape=jax.ShapeDtypeStruct((depth, 8, n), F32),
        compiler_params=_cparams("parallel", "parallel"),
        name="adaln_rows",
    )(cond8, w_mod, b_mod.reshape(depth, 1, n))


class _Tokens:
    def __init__(self, n_ctx, ctx_len, n_lat, lat_len):
        self.n_ctx, self.ctx_len, self.n_lat, self.lat_len = n_ctx, ctx_len, n_lat, lat_len
        self.ctx_tokens = n_ctx * ctx_len
        self.total = self.ctx_tokens + n_lat * lat_len

    def mod_row(self, tile, i):
        nct = self.ctx_tokens // tile
        per_lat = self.lat_len // tile
        return jnp.where(i < nct, 0, 1 + (i - nct) // per_lat)


def _mod_spec(tok, tile, layer, which, grid_rank):
    def imap(*idx):
        return (layer, which, tok.mod_row(tile, idx[0]), 0, 0)
    del grid_rank
    return pl.BlockSpec((None, None, None, 1, D_MODEL), imap)


def _normmod_kernel(x_ref, g_ref, sc_ref, sh_ref, h_ref):
    h_ref[...] = _norm_mod(x_ref[...], g_ref[...], sc_ref[...], sh_ref[...]).astype(h_ref.dtype)


def _normmod(tok, x, norm_g_row, modr, layer, sc_i, sh_i):
    tm = TOKEN_TILE
    return pl.pallas_call(
        _normmod_kernel,
        grid=(tok.total // tm,),
        in_specs=[
            pl.BlockSpec((tm, D_MODEL), lambda i: (i, 0)),
            pl.BlockSpec((1, D_MODEL), lambda i: (0, 0)),
            _mod_spec(tok, tm, layer, sc_i, 1),
            _mod_spec(tok, tm, layer, sh_i, 1),
        ],
        out_specs=pl.BlockSpec((tm, D_MODEL), lambda i: (i, 0)),
        out_shape=jax.ShapeDtypeStruct((tok.total, D_MODEL), BF16),
        compiler_params=_cparams("parallel"),
        name="first_norm",
    )(x, norm_g_row, modr, modr)


def _ffn_kernel(h_ref, x_ref, gt_ref, w1_ref, w3_ref, w2_ref, gn_ref, scn_ref, shn_ref,
                xo_ref, ho_ref, acc_ref, *, n_ff):
    j = pl.program_id(1)
    h = h_ref[...]
    a = _dot(h, w1_ref[...])
    b = _dot(h, w3_ref[...])
    p = (_silu(a) * b).astype(BF16)
    part = _dot(p, w2_ref[...])

    @pl.when(j == 0)
    def _():
        acc_ref[...] = part

    @pl.when(j > 0)
    def _():
        acc_ref[...] += part

    @pl.when(j == n_ff - 1)
    def _():
        xn = x_ref[...] + 0.5 * gt_ref[...] * acc_ref[...]
        xo_ref[...] = xn
        ho_ref[...] = _norm_mod(xn, gn_ref[...], scn_ref[...], shn_ref[...]).astype(ho_ref.dtype)


def _ffn(tok, h, x, modr, layer, gate_i, w1, w3, w2, next_g, next_sc, next_sh, next_dtype):
    tm, tf = TOKEN_TILE, FF_TILE
    n_ff = D_FF // tf
    row = pl.BlockSpec((tm, D_MODEL), lambda i, j: (i, 0))
    return pl.pallas_call(
        functools.partial(_ffn_kernel, n_ff=n_ff),
        grid=(tok.total // tm, n_ff),
        in_specs=[
            row, row,
            _mod_spec(tok, tm, layer, gate_i, 2),
            pl.BlockSpec((None, D_MODEL, tf), lambda i, j: (j, 0, 0)),
            pl.BlockSpec((None, D_MODEL, tf), lambda i, j: (j, 0, 0)),
            pl.BlockSpec((tf, D_MODEL), lambda i, j: (j, 0)),
            pl.BlockSpec((1, D_MODEL), lambda i, j: (0, 0)),
            next_sc[1], next_sh[1],
        ],
        out_specs=[row, row],
        out_shape=[jax.ShapeDtypeStruct((tok.total, D_MODEL), F32),
                   jax.ShapeDtypeStruct((tok.total, D_MODEL), next_dtype)],
        scratch_shapes=[pltpu.VMEM((tm, D_MODEL), F32)],
        compiler_params=_cparams("parallel", "arbitrary"),
        name="swiglu_half_step",
    )(h, x, modr, w1, w3, w2, next_g, next_sc[0], next_sh[0])


def _matmul_kernel(h_ref, w_ref, o_ref):
    o_ref[...] = _dot(h_ref[...], w_ref[...])


def _col_tiles(w, tn):
    k, n = w.shape
    return w.astype(BF16).reshape(k, n // tn, tn).transpose(1, 0, 2)


def _project(h, w):
    t, k = h.shape
    n_tiles, _, tn = w.shape
    tm = TOKEN_TILE
    return pl.pallas_call(
        _matmul_kernel,
        grid=(n_tiles, t // tm),
        in_specs=[pl.BlockSpec((tm, k), lambda j, i: (i, 0)),
                  pl.BlockSpec((None, k, tn), lambda j, i: (j, 0, 0))],
        out_specs=pl.BlockSpec((tm, tn), lambda j, i: (i, j)),
        out_shape=jax.ShapeDtypeStruct((t, n_tiles * tn), F32),
        compiler_params=_cparams("parallel", "parallel"),
        name="mix_in_proj",
    )(h, w)


def _gmlp_kernel(za_ref, ng_ref, ws_ref, bs_ref, o_ref):
    z = jax.nn.gelu(za_ref[...])
    outs = []
    for g in range(A_GROUPS):
        u = z[:, g * A_GDIM:(g + 1) * A_GDIM]
        v = z[:, A_WIDTH + g * A_GDIM:A_WIDTH + (g + 1) * A_GDIM]
        ms = jnp.mean(v * v, axis=-1, keepdims=True)
        v = v * lax.rsqrt(ms + EPS) * ng_ref[g:g + 1, :]
        mixed = _dot(ws_ref[g], v.astype(BF16)) + bs_ref[g]
        outs.append(u * mixed)
    o_ref[...] = jnp.concatenate(outs, axis=-1).astype(o_ref.dtype)


def _gmlp(za, ng, ws, bs):
    t = za.shape[0]
    return pl.pallas_call(
        _gmlp_kernel,
        grid=(t // CHUNK_MLP,),
        in_specs=[
            pl.BlockSpec((CHUNK_MLP, A_COLS), lambda i: (i, 0)),
            pl.BlockSpec((A_GROUPS, A_GDIM), lambda i: (0, 0)),
            pl.BlockSpec((A_GROUPS, CHUNK_MLP, CHUNK_MLP), lambda i: (0, 0, 0)),
            pl.BlockSpec((A_GROUPS, CHUNK_MLP, 1), lambda i: (0, 0, 0)),
        ],
        out_specs=pl.BlockSpec((CHUNK_MLP, A_WIDTH), lambda i: (i, 0)),
        out_shape=jax.ShapeDtypeStruct((t, A_WIDTH), BF16),
        compiler_params=_cparams("parallel"),
        name="gmlp_chunk_mix",
    )(za, ng, ws.astype(BF16), bs[:, :, None])


def _head_ones():
    r = _iota((LANES, LANES), 0) // B_HDIM
    c = _iota((LANES, LANES), 1) // B_HDIM
    return (r == c).astype(BF16)


def _head_sum(x, parts=2):
    ones = _head_ones()
    outs = []
    for s in range(x.shape[-1] // LANES):
        outs.append(_exact_right(x[:, s * LANES:(s + 1) * LANES], ones, parts))
    return jnp.concatenate(outs, axis=-1)


def _rwkv_prep_kernel(z_ref, zp_ref, zn_ref, mu_ref, w0_ref, w2_ref, a0_ref, a2_ref, g2_ref,
                      kkw_ref, kaw_ref, rk_ref,
                      r_ref, v_ref, kk_ref, lw_ref, kd_ref, b_ref, gate_ref, bonus_ref,
                      *, n_ctx_tiles, ctx_tiles_per_seq, lat_tiles_per_seq):
    i = pl.program_id(0)
    pos = jnp.where(i < n_ctx_tiles, i % ctx_tiles_per_seq, (i - n_ctx_tiles) % lat_tiles_per_seq)
    last = jnp.where(i < n_ctx_tiles, ctx_tiles_per_seq - 1, lat_tiles_per_seq - 1)
    z = z_ref[...]
    t = z.shape[0]
    row = _iota((t, 1), 0)
    halo_prev = jnp.where(pos != 0, zp_ref[7:8, :], 0.0)
    halo_next = jnp.where(pos != last, zn_ref[0:1, :], 0.0)
    prev = jnp.where(row == 0, halo_prev, pltpu.roll(z, 1, 0))
    nxt = jnp.where(row == t - 1, halo_next, pltpu.roll(z, t - 1, 0))
    z = z + (0.5 * (prev + nxt) - z) * mu_ref[...]

    w = B_WIDTH
    r, k, v = z[:, 0:w], z[:, w:2 * w], z[:, 2 * w:3 * w]
    wd = z[:, 3 * w:3 * w + 2 * LORA]
    ad = z[:, 3 * w + 2 * LORA:3 * w + 4 * LORA]
    gd = z[:, 3 * w + 4 * LORA:]
    twd = jnp.tanh(wd).astype(BF16)
    adb = ad.astype(BF16)

    kk = k * kkw_ref[...]
    nrm = jnp.sqrt(_head_sum(kk * kk))
    kk = kk / jnp.maximum(nrm, 1e-12)

    r_ref[...] = r
    v_ref[...] = v
    kk_ref[...] = kk
    gate_ref[...] = _dot(_sigmoid(gd).astype(BF16), g2_ref[...])
    ksum = jnp.zeros_like(k)
    for d in range(2):
        w_raw = w0_ref[d:d + 1, :] + _dot(twd[:, d * LORA:(d + 1) * LORA], w2_ref[d])
        lw_ref[d] = -jnp.exp(-jax.nn.softplus(-w_raw) - 0.5)
        a = _sigmoid(a0_ref[d:d + 1, :] + _dot(adb[:, d * LORA:(d + 1) * LORA], a2_ref[d]))
        kd = k * (1.0 + (a - 1.0) * kaw_ref[...])
        kd_ref[d] = kd
        b_ref[d] = kk * a
        ksum = ksum + kd
    bonus_ref[...] = _head_sum(r * ksum * rk_ref[...]) * v


def _rwkv_prep(tok, zb, mu, w0, w2, a0, a2, g2, kkw, kaw, rk):
    tp = PREP_TILE
    n_tiles = tok.total // tp
    rows8 = tok.total // 8
    per = tp // 8
    full = lambda shape: pl.BlockSpec(shape, lambda i: (0,) * len(shape))
    tile = pl.BlockSpec((tp, B_WIDTH), lambda i: (i, 0))
    tile2 = pl.BlockSpec((2, tp, B_WIDTH), lambda i: (0, i, 0))
    o1 = jax.ShapeDtypeStruct((tok.total, B_WIDTH), F32)
    o2 = jax.ShapeDtypeStruct((2, tok.total, B_WIDTH), F32)
    kern = functools.partial(_rwkv_prep_kernel, n_ctx_tiles=tok.ctx_tokens // tp,
                             ctx_tiles_per_seq=tok.ctx_len // tp, lat_tiles_per_seq=tok.lat_len // tp)
    return pl.pallas_call(
        kern,
        grid=(n_tiles,),
        in_specs=[
            pl.BlockSpec((tp, B_COLS), lambda i: (i, 0)),
            pl.BlockSpec((8, B_COLS), lambda i: (jnp.maximum(i * per - 1, 0), 0)),
            pl.BlockSpec((8, B_COLS), lambda i: (jnp.minimum((i + 1) * per, rows8 - 1), 0)),
            full((1, B_COLS)), full((2, B_WIDTH)), full((2, LORA, B_WIDTH)), full((2, B_WIDTH)),
            full((2, LORA, B_WIDTH)), full((GATE_LORA, B_WIDTH)),
            full((1, B_WIDTH)), full((1, B_WIDTH)), full((1, B_WIDTH)),
        ],
        out_specs=[tile, tile, tile, tile2, tile2, tile2, tile, tile],
        out_shape=[o1, o1, o1, o2, o2, o2, o1, o1],
        compiler_params=_cparams("parallel"),
        name="rwkv_prep",
    )(zb, zb, zb, mu, w0, w2.astype(BF16), a0, a2.astype(BF16), g2.astype(BF16), kkw, kaw, rk)


def _pair_blockdiag(x):
    lane = _iota(x.shape, 1)
    zero = jnp.zeros_like(x)
    return jnp.concatenate([jnp.where(lane < B_HDIM, x, zero), jnp.where(lane >= B_HDIM, x, zero)], axis=0)


def _rwkv_chunks(streams, s_ref):
    c = streams[0][0].shape[0]
    n_pairs = B_HEADS // 2
    ri, ci = _iota((c, c), 0), _iota((c, c), 1)
    rr, cc = _iota((2 * c, 2 * c), 0), _iota((2 * c, 2 * c), 1)
    same = (rr // c) == (cc // c)
    tt, ss = rr % c, cc % c
    eye = (rr == cc).astype(F32)
    pair_masks = []
    n = 1
    while n < c:
        pair_masks.append(same & ((tt // (2 * n)) == (ss // (2 * n))) & ((tt // n) != (ss // n)))
        n *= 2

    units = []
    for d, (r, v, kk, lw, kd, b, reverse) in enumerate(streams):
        upto_c = (ci >= ri) if reverse else (ci <= ri)
        g = _exact_left(upto_c.astype(BF16), lw, 3)
        g_last = g[0:1, :] if reverse else g[c - 1:c, :]
        eg = jnp.exp(g)
        ieg = jnp.exp(-g)
        a_t = (-kk * jnp.exp(g - lw)).astype(BF16)
        r_t = (r * eg).astype(BF16)
        b_t = (b * ieg).astype(BF16)
        k_t = (kd * ieg).astype(BF16)
        vb = v.astype(BF16)
        gl = jnp.exp(g_last)
        before = same & ((ss > tt) if reverse else (ss < tt))
        upto = same & ((ss >= tt) if reverse else (ss <= tt))
        for p in range(n_pairs):
            sl = slice(p * LANES, (p + 1) * LANES)
            units.append(dict(d=d, p=p, before=before, upto=upto, gl=gl[:, sl],
                              ar=jnp.concatenate([_pair_blockdiag(a_t[:, sl]), _pair_blockdiag(r_t[:, sl])], 0),
                              bk=jnp.concatenate([_pair_blockdiag(b_t[:, sl]), _pair_blockdiag(k_t[:, sl])], 0),
                              v=_pair_blockdiag(vb[:, sl])))
    m = 2 * c
    for u in units:
        u["gram"] = _dot_nt(u["ar"], u["bk"])
        u["s0"] = s_ref[u["d"], u["p"]]
    for u in units:
        u["ws"] = _dot_nt(u["ar"], u["s0"].astype(BF16))
        g = u["gram"]
        u["lmat"] = jnp.where(u["before"], g[:m, :m], 0.0)
        aak = jnp.where(u["before"], g[:m, m:], 0.0).astype(BF16)
        u["rbk"] = jnp.concatenate([jnp.where(u["upto"], g[m:, :m], 0.0),
                                    jnp.where(u["upto"], g[m:, m:], 0.0)], axis=1).astype(BF16)
        u["x"] = u["ws"][:m] + _dot(aak, u["v"])
    for lvl, pm in enumerate(pair_masks):
        for u in units:
            link = jnp.where(pm, u["lmat"], 0.0)
            if lvl == 0:
                u["tinv"] = eye + link
            else:
                u["tb"] = u["tinv"].astype(BF16)
                u["tmp"] = _dot(link.astype(BF16), u["tb"]).astype(BF16)
        if lvl > 0:
            for u in units:
                u["tinv"] = u["tinv"] + _dot(u["tb"], u["tmp"])
    for u in units:
        u["uv"] = jnp.concatenate([_dot(u["tinv"].astype(BF16), u["x"].astype(BF16)).astype(BF16), u["v"]], 0)
    ys = [[None] * n_pairs for _ in streams]
    for u in units:
        y = u["ws"][m:] + _dot(u["rbk"], u["uv"])
        ys[u["d"]][u["p"]] = y[:c] + y[c:]
        s_ref[u["d"], u["p"]] = (u["s0"] + _dot_tn(u["uv"], u["bk"])) * u["gl"]
    return [jnp.concatenate(row, axis=-1) for row in ys]


def _rwkv_scan_kernel(tab_ref, rf_ref, vf_ref, kkf_ref, lwf_ref, kdf_ref, bf_ref,
                      rb_ref, vb_ref, kkb_ref, lwb_ref, kdb_ref, bb_ref, s0_ref,
                      yf_ref, yb_ref, sfin_ref, s_ref):
    step = pl.program_id(0)
    hd = B_HDIM

    @pl.when(tab_ref[3, step] == 1)
    def _():
        z = jnp.zeros((hd, hd), F32)
        for d in range(2):
            for p in range(B_HEADS // 2):
                s_ref[d, p] = jnp.concatenate(
                    [jnp.concatenate([s0_ref[d, 2 * p], z], axis=1),
                     jnp.concatenate([z, s0_ref[d, 2 * p + 1]], axis=1)], axis=0)

    yf, yb = _rwkv_chunks(
        [(rf_ref[...], vf_ref[...], kkf_ref[...], lwf_ref[...], kdf_ref[...], bf_ref[...], False),
         (rb_ref[...], vb_ref[...], kkb_ref[...], lwb_ref[...], kdb_ref[...], bb_ref[...], True)], s_ref)
    yf_ref[...] = yf
    yb_ref[...] = yb

    @pl.when(tab_ref[4, step] == 1)
    def _():
        for d in range(2):
            for p in range(B_HEADS // 2):
                s = s_ref[d, p]
                sfin_ref[d, 2 * p] = s[:hd, :hd]
                sfin_ref[d, 2 * p + 1] = s[hd:, hd:]


def _scan_table(tok, chunk):
    cols = []
    for s in range(tok.n_ctx + tok.n_lat):
        if s < tok.n_ctx:
            base, nc = s * tok.ctx_len // chunk, tok.ctx_len // chunk
        else:
            base = (tok.ctx_tokens + (s - tok.n_ctx) * tok.lat_len) // chunk
            nc = tok.lat_len // chunk
        for c in range(nc):
            cols.append((base + c, base + nc - 1 - c, s, int(c == 0), int(c == nc - 1)))
    return jnp.asarray(list(zip(*cols)), dtype=jnp.int32)


def _rwkv_scan(tok, r, v, kk, lw, kd, b, s0):
    cl = RW_CHUNK
    tab = _scan_table(tok, cl)
    fwd = lambda s, t: (t[0, s], 0)
    bwd = lambda s, t: (t[1, s], 0)
    fwd2 = lambda s, t: (0, t[0, s], 0)
    bwd2 = lambda s, t: (1, t[1, s], 0)
    blk = lambda im: pl.BlockSpec((cl, B_WIDTH), im)
    blk2 = lambda im: pl.BlockSpec((None, cl, B_WIDTH), im)
    st = pl.BlockSpec((None, 2, B_HEADS, B_HDIM, B_HDIM), lambda s, t: (t[2, s], 0, 0, 0, 0))
    yshape = jax.ShapeDtypeStruct((tok.total, B_WIDTH), F32)
    return pl.pallas_call(
        _rwkv_scan_kernel,
        grid_spec=pltpu.PrefetchScalarGridSpec(
            num_scalar_prefetch=1,
            grid=(tab.shape[1],),
            in_specs=[blk(fwd), blk(fwd), blk(fwd), blk2(fwd2), blk2(fwd2), blk2(fwd2),
                      blk(bwd), blk(bwd), blk(bwd), blk2(bwd2), blk2(bwd2), blk2(bwd2), st],
            out_specs=[blk(fwd), blk(bwd), st],
            scratch_shapes=[pltpu.VMEM((2, B_HEADS // 2, LANES, LANES), F32)]),
        out_shape=[yshape, yshape, jax.ShapeDtypeStruct(s0.shape, F32)],
        compiler_params=_cparams("arbitrary"),
        name="rwkv7_scan",
    )(tab, r, v, kk, lw, kd, b, r, v, kk, lw, kd, b, s0)


def _rwkv_post_kernel(yf_ref, yb_ref, bonus_ref, gate_ref, g_ref, b_ref, o_ref):
    y = yf_ref[...] + yb_ref[...]
    mean = _head_sum(y, 3) * (1.0 / B_HDIM)
    yc = y - mean
    var = _head_sum(yc * yc) * (1.0 / B_HDIM)
    y = yc * lax.rsqrt(var + LNX_EPS) * g_ref[...] + b_ref[...] + bonus_ref[...]
    o_ref[...] = (y * gate_ref[...]).astype(o_ref.dtype)


def _rwkv_post(yf, yb, bonus, gate, lnx_g, lnx_b):
    t = yf.shape[0]
    tm = TOKEN_TILE
    tile = pl.BlockSpec((tm, B_WIDTH), lambda i: (i, 0))
    vec = pl.BlockSpec((1, B_WIDTH), lambda i: (0, 0))
    return pl.pallas_call(
        _rwkv_post_kernel,
        grid=(t // tm,),
        in_specs=[tile, tile, tile, tile, vec, vec],
        out_specs=tile,
        out_shape=jax.ShapeDtypeStruct((t, B_WIDTH), BF16),
        compiler_params=_cparams("parallel"),
        name="rwkv_post",
    )(yf, yb, bonus, gate, lnx_g, lnx_b)


def _hgrn_block(qraw, fraw, v, lb, st_ref, reverse):
    n = qraw.shape[0]
    sub = HG_SUB
    fg = lb + (1.0 - lb) * _sigmoid(fraw)
    logf = jnp.log(fg)
    kg = 1.0 - fg
    q = _silu(qraw)
    ri, ci = _iota((n, n), 0), _iota((n, n), 1)
    same = (ri // sub) == (ci // sub)
    upto = same & ((ci >= ri) if reverse else (ci <= ri))
    bcum = _exact_left(upto.astype(BF16), logf, 3)
    ones = jnp.ones((LANES, LANES), BF16)
    half = 8
    hi = _iota((half, 1), 0)
    order = range(n // sub - 1, -1, -1) if reverse else range(n // sub)
    out_rows = [None] * (n // sub)
    for sc in order:
        last = sc * sub if reverse else (sc + 1) * sub - 1
        outs = []
        for h in range(C_HEADS):
            cols = slice(h * C_KDIM, (h + 1) * C_KDIM)
            rows = slice(sc * sub, (sc + 1) * sub)
            qs, ks, vs, bs = q[rows, cols], kg[rows, cols], v[rows, cols], bcum[rows, cols]
            b_end = bcum[last:last + 1, cols]
            slabs, plan = [], []
            for j in range(sub):
                for hf in range(sub // half):
                    rs = slice(hf * half, (hf + 1) * half)
                    diff = bs[rs] - bs[j:j + 1, :]
                    if hf == j // half:
                        live = (hi + hf * half <= j) if reverse else (hi + hf * half >= j)
                        diff = jnp.where(live, diff, -1e30)
                    elif (hf > j // half) == reverse:
                        continue
                    slabs.append(qs[rs] * jnp.exp(diff) * ks[j:j + 1, :])
                    plan.append((j, hf))
            att = _dot(jnp.concatenate(slabs, axis=0).astype(BF16), ones)
            o = _dot_nt((qs * jnp.exp(bs)).astype(BF16), st_ref[h].astype(BF16))
            o_half = [o[hf * half:(hf + 1) * half] for hf in range(sub // half)]
            for idx, (j, hf) in enumerate(plan):
                o_half[hf] = o_half[hf] + att[idx * half:(idx + 1) * half, :] * vs[j:j + 1, :]
            outs.append(jnp.concatenate(o_half, axis=0))
            ke = (ks * jnp.exp(b_end - bs)).astype(BF16)
            st_ref[h] = st_ref[h] * jnp.exp(b_end) + _dot_tn(vs.astype(BF16), ke)
        out_rows[sc] = jnp.concatenate(outs, axis=-1)
    return jnp.concatenate(out_rows, axis=0)


def _hgrn_scan_kernel(tab_ref, qf_ref, ff_ref, vf_ref, qb_ref, fb_ref, vb_ref, lb_ref, s0_ref,
                      of_ref, ob_ref, sfin_ref, st_ref):
    step = pl.program_id(0)

    @pl.when(tab_ref[3, step] == 1)
    def _():
        for d in range(2):
            for h in range(C_HEADS):
                st_ref[d, h] = s0_ref[d, h].T

    of_ref[...] = _hgrn_block(qf_ref[...], ff_ref[...], vf_ref[...], lb_ref[0:1, :], st_ref.at[0], False)
    ob_ref[...] = _hgrn_block(qb_ref[...], fb_ref[...], vb_ref[...], lb_ref[1:2, :], st_ref.at[1], True)

    @pl.when(tab_ref[4, step] == 1)
    def _():
        for d in range(2):
            for h in range(C_HEADS):
                sfin_ref[d, h] = st_ref[d, h].T


def _hgrn_scan(tok, zc, lb, s0):
    cl = HG_BLOCK
    tab = _scan_table(tok, cl)
    fwd = lambda col: (lambda s, t: (t[0, s], col))
    bwd = lambda col: (lambda s, t: (t[1, s], col))
    blk = lambda im: pl.BlockSpec((cl, C_WIDTH), im)
    st = pl.BlockSpec((None, 2, C_HEADS, C_KDIM, C_KDIM), lambda s, t: (t[2, s], 0, 0, 0, 0))
    oshape = jax.ShapeDtypeStruct((tok.total, C_WIDTH), F32)
    return pl.pallas_call(
        _hgrn_scan_kernel,
        grid_spec=pltpu.PrefetchScalarGridSpec(
            num_scalar_prefetch=1,
            grid=(tab.shape[1],),
            in_specs=[blk(fwd(0)), blk(fwd(1)), blk(fwd(3)), blk(bwd(0)), blk(bwd(2)), blk(bwd(3)),
                      pl.BlockSpec((2, C_WIDTH), lambda s, t: (0, 0)), st],
            out_specs=[blk(fwd(0)), blk(bwd(0)), st],
            scratch_shapes=[pltpu.VMEM((2, C_HEADS, C_KDIM, C_KDIM), F32)]),
        out_shape=[oshape, oshape, jax.ShapeDtypeStruct(s0.shape, F32)],
        compiler_params=_cparams("arbitrary"),
        name="hgrn2_scan",
    )(tab, zc, zc, zc, zc, zc, zc, lb, s0)


def _hgrn_post_kernel(of_ref, ob_ref, g_ref, gn_ref, o_ref):
    o = of_ref[...] + ob_ref[...]
    gate = _silu(g_ref[...])
    outs = []
    for h in range(C_HEADS):
        cols = slice(h * C_KDIM, (h + 1) * C_KDIM)
        oh = o[:, cols]
        ms = jnp.mean(oh * oh, axis=-1, keepdims=True)
        outs.append(oh * lax.rsqrt(ms + EPS) * gn_ref[...] * gate[:, cols])
    o_ref[...] = jnp.concatenate(outs, axis=-1).astype(o_ref.dtype)


def _hgrn_post(of, ob, zc, gn):
    t = of.shape[0]
    tm = TOKEN_TILE
    tile = pl.BlockSpec((tm, C_WIDTH), lambda i: (i, 0))
    return pl.pallas_call(
        _hgrn_post_kernel,
        grid=(t // tm,),
        in_specs=[tile, tile, pl.BlockSpec((tm, C_WIDTH), lambda i: (i, 4)),
                  pl.BlockSpec((1, C_KDIM), lambda i: (0, 0))],
        out_specs=tile,
        out_shape=jax.ShapeDtypeStruct((t, C_WIDTH), BF16),
        compiler_params=_cparams("parallel"),
        name="hgrn2_post",
    )(of, ob, zc, gn)


def _mix_out_kernel(oa_ref, ob_ref, oc_ref, wa_ref, wb_ref, wc_ref, x_ref, gt_ref,
                    gn_ref, scn_ref, shn_ref, xo_ref, ho_ref):
    y = _dot(oa_ref[...], wa_ref[...]) + _dot(ob_ref[...], wb_ref[...]) + _dot(oc_ref[...], wc_ref[...])
    xn = x_ref[...] + gt_ref[...] * y
    xo_ref[...] = xn
    ho_ref[...] = _norm_mod(xn, gn_ref[...], scn_ref[...], shn_ref[...]).astype(ho_ref.dtype)


def _mix_out(tok, oa, ob, oc, wa, wb, wc, x, modr, layer, next_g):
    tm = TOKEN_TILE
    row = pl.BlockSpec((tm, D_MODEL), lambda i: (i, 0))
    full = lambda a: pl.BlockSpec(a.shape, lambda i: (0, 0))
    return pl.pallas_call(
        _mix_out_kernel,
        grid=(tok.total // tm,),
        in_specs=[pl.BlockSpec((tm, A_WIDTH), lambda i: (i, 0)),
                  pl.BlockSpec((tm, B_WIDTH), lambda i: (i, 0)),
                  pl.BlockSpec((tm, C_WIDTH), lambda i: (i, 0)),
                  full(wa), full(wb), full(wc), row,
                  _mod_spec(tok, tm, layer, 5, 1),
                  pl.BlockSpec((1, D_MODEL), lambda i: (0, 0)),
                  _mod_spec(tok, tm, layer, 7, 1), _mod_spec(tok, tm, layer, 6, 1)],
        out_specs=[row, row],
        out_shape=[jax.ShapeDtypeStruct((tok.total, D_MODEL), F32),
                   jax.ShapeDtypeStruct((tok.total, D_MODEL), BF16)],
        compiler_params=_cparams("parallel"),
        name="mix_out_proj",
    )(oa, ob, oc, wa, wb, wc, x, modr, next_g, modr, modr)


def kernel(x_prompt, x_sample, state_rwkv, state_hgrn, c, c_ctx, norm_g, w_mod, b_mod, ffn_w1, ffn_w3,
           ffn_w2, w_in, w_out, mlp_norm_g, mlp_ws, mlp_bs, rwkv_mu, rwkv_w0, rwkv_w2, rwkv_a0, rwkv_a2,
           rwkv_g2, rwkv_kk, rwkv_ka, rwkv_rk, rwkv_lnx_g, rwkv_lnx_b, hgrn_lb, hgrn_gn, final_g):
    n_ctx, ctx_len, _ = x_prompt.shape
    n_lat, lat_len, _ = x_sample.shape
    depth = w_mod.shape[0]
    tok = _Tokens(n_ctx, ctx_len, n_lat, lat_len)
    assert n_lat + 1 <= 8 and ctx_len % PREP_TILE == 0 and lat_len % TOKEN_TILE == 0

    cond8 = jnp.zeros((8, D_MODEL), F32).at[0].set(c_ctx).at[1:1 + n_lat].set(c)
    mod = _mod_rows(cond8, w_mod, b_mod)
    modr = mod.reshape(depth, 8, N_MOD, D_MODEL).transpose(0, 2, 1, 3)[:, :, :, None, :]

    sm = jax.nn.softmax(hgrn_lb.astype(F32), axis=0)
    lower = jnp.cumsum(sm, axis=0) - sm[0]

    x = jnp.concatenate([x_prompt.reshape(-1, D_MODEL), x_sample.reshape(-1, D_MODEL)], axis=0)
    zero_sc = jnp.zeros((1, D_MODEL), F32)
    zero_spec = pl.BlockSpec((1, D_MODEL), lambda i, j: (0, 0))
    tm = TOKEN_TILE

    h = _normmod(tok, x, norm_g[0, 0][None], modr, 0, 1, 0)
    rw_states, hg_states = [], []
    for l in range(depth):
        bw = lambda a: a.astype(BF16)
        ct = lambda a: _col_tiles(a, FF_TILE)
        x, h = _ffn(tok, h, x, modr, l, 2, ct(ffn_w1[l, 0]), ct(ffn_w3[l, 0]), bw(ffn_w2[l, 0]),
                    norm_g[l, 1][None], (modr, _mod_spec(tok, tm, l, 4, 2)),
                    (modr, _mod_spec(tok, tm, l, 3, 2)), BF16)

        za = _project(h, _col_tiles(w_in[l, :, :A_COLS], 512))
        zb = _project(h, _col_tiles(w_in[l, :, A_COLS:A_COLS + B_COLS], 896))
        zc = _project(h, _col_tiles(w_in[l, :, A_COLS + B_COLS:], 768))

        oa = _gmlp(za, mlp_norm_g[l], mlp_ws[l], mlp_bs[l])

        r, v, kk, lw, kd, b, gate, bonus = _rwkv_prep(
            tok, zb, rwkv_mu[l][None], rwkv_w0[l], rwkv_w2[l], rwkv_a0[l], rwkv_a2[l], rwkv_g2[l],
            rwkv_kk[l][None], rwkv_ka[l][None], rwkv_rk[l].reshape(1, B_WIDTH))
        s0 = jnp.concatenate([jnp.zeros((n_ctx, 2, B_HEADS, B_HDIM, B_HDIM), F32), state_rwkv[:, l]], 0)
        yf, yb, s_rw = _rwkv_scan(tok, r, v, kk, lw, kd, b, s0)
        ob = _rwkv_post(yf, yb, bonus, gate,
                        rwkv_lnx_g[l].reshape(1, B_WIDTH), rwkv_lnx_b[l].reshape(1, B_WIDTH))
        rw_states.append(s_rw[:n_ctx])

        s0 = jnp.concatenate([jnp.zeros((n_ctx, 2, C_HEADS, C_KDIM, C_KDIM), F32), state_hgrn[:, l]], 0)
        of, obw, s_hg = _hgrn_scan(tok, zc, lower[l], s0)
        oc = _hgrn_post(of, obw, zc, hgrn_gn[l][None])
        hg_states.append(s_hg[:n_ctx])

        w_out_l = bw(w_out[l])
        x, h = _mix_out(tok, oa, ob, oc, w_out_l[:A_WIDTH], w_out_l[A_WIDTH:A_WIDTH + B_WIDTH],
                        w_out_l[A_WIDTH + B_WIDTH:], x, modr, l, norm_g[l, 2][None])

        if l + 1 < depth:
            x, h = _ffn(tok, h, x, modr, l, 8, ct(ffn_w1[l, 1]), ct(ffn_w3[l, 1]), bw(ffn_w2[l, 1]),
                        norm_g[l + 1, 0][None], (modr, _mod_spec(tok, tm, l + 1, 1, 2)),
                        (modr, _mod_spec(tok, tm, l + 1, 0, 2)), BF16)
        else:
            x, h = _ffn(tok, h, x, modr, l, 8, ct(ffn_w1[l, 1]), ct(ffn_w3[l, 1]), bw(ffn_w2[l, 1]),
                        final_g[None], (zero_sc, zero_spec), (zero_sc, zero_spec), F32)

    y_prompt = h[:tok.ctx_tokens].reshape(x_prompt.shape)
    y_sample = h[tok.ctx_tokens:].reshape(x_sample.shape)
    return (y_prompt, y_sample, jnp.stack(rw_states, axis=1), jnp.stack(hg_states, axis=1))
```

```python
import functools

import jax
import jax.numpy as jnp
from jax import lax
from jax.experimental import pallas as pl
from jax.experimental.pallas import tpu as pltpu

F32 = jnp.float32
BF16 = jnp.bfloat16

D_MODEL = 2048
D_FF = 5632
N_MOD = 9
EPS = 1e-6
LNX_EPS = 64e-5

CHUNK_MLP = 128
A_WIDTH = 512
A_GROUPS = 4
A_GDIM = 128
B_HDIM = 64
B_WIDTH = 768
B_HEADS = 12
LORA = 64
GATE_LORA = 128
B_COLS = 3 * B_WIDTH + 4 * LORA + GATE_LORA
C_KDIM = 128
C_WIDTH = 768
C_HEADS = 6
C_COLS = 5 * C_WIDTH
A_COLS = 2 * A_WIDTH

LANES = 128
VMEM_LIMIT = 56 * 1024 * 1024

TOKEN_TILE = 512
FF_TILE = 512
PREP_TILE = 256
RW_CHUNK = 64
HG_BLOCK = 64
HG_SUB = 16


def _cparams(*sem):
    return pltpu.CompilerParams(dimension_semantics=sem, vmem_limit_bytes=VMEM_LIMIT)


def _dot(a, b):
    return jnp.dot(a, b, preferred_element_type=F32)


def _dot_nt(a, b):
    return lax.dot_general(a, b, (((1,), (1,)), ((), ())), preferred_element_type=F32)


def _dot_tn(a, b):
    return lax.dot_general(a, b, (((0,), (0,)), ((), ())), preferred_element_type=F32)


def _split(x, parts):
    out = []
    for _ in range(parts - 1):
        hi = x.astype(BF16)
        out.append(hi)
        x = x - hi.astype(F32)
    out.append(x.astype(BF16))
    return out


def _exact_left(m, x, parts):
    acc = None
    for p in _split(x, parts):
        t = _dot(m, p)
        acc = t if acc is None else acc + t
    return acc


def _exact_right(x, m, parts):
    acc = None
    for p in _split(x, parts):
        t = _dot(p, m)
        acc = t if acc is None else acc + t
    return acc


def _iota(shape, dim):
    return lax.broadcasted_iota(jnp.int32, shape, dim)


def _sigmoid(x):
    return jax.nn.sigmoid(x)


def _silu(x):
    return x * jax.nn.sigmoid(x)


def _norm_mod(x, g, sc, sh):
    ms = jnp.mean(x * x, axis=-1, keepdims=True)
    return (x * lax.rsqrt(ms + EPS) * g) * (1.0 + sc) + sh


def _mod_kernel(cond_ref, w_ref, b_ref, o_ref):
    s = _silu(cond_ref[...]).astype(BF16)
    o_ref[...] = _dot(s, w_ref[...].astype(BF16)) + b_ref[...]


def _mod_rows(cond8, w_mod, b_mod):
    depth, _, n = w_mod.shape
    tn = 1024
    return pl.pallas_call(
        _mod_kernel,
        grid=(depth, n // tn),
        in_specs=[
            pl.BlockSpec((8, D_MODEL), lambda l, j: (0, 0)),
            pl.BlockSpec((None, D_MODEL, tn), lambda l, j: (l, 0, j)),
            pl.BlockSpec((None, 1, tn), lambda l, j: (l, 0, j)),
        ],
        out_specs=pl.BlockSpec((None, 8, tn), lambda l, j: (l, 0, j)),
        out_shape=jax.ShapeDtypeStruct((depth, 8, n), F32),
        compiler_params=_cparams("parallel", "parallel"),
        name="adaln_rows",
    )(cond8, w_mod, b_mod.reshape(depth, 1, n))


class _Tokens:
    def __init__(self, n_ctx, ctx_len, n_lat, lat_len):
        self.n_ctx, self.ctx_len, self.n_lat, self.lat_len = n_ctx, ctx_len, n_lat, lat_len
        self.ctx_tokens = n_ctx * ctx_len
        self.total = self.ctx_tokens + n_lat * lat_len

    def mod_row(self, tile, i):
        nct = self.ctx_tokens // tile
        per_lat = self.lat_len // tile
        return jnp.where(i < nct, 0, 1 + (i - nct) // per_lat)


def _mod_spec(tok, tile, layer, which, grid_rank):
    def imap(*idx):
        return (layer, which, tok.mod_row(tile, idx[0]), 0, 0)
    del grid_rank
    return pl.BlockSpec((None, None, None, 1, D_MODEL), imap)


def _normmod_kernel(x_ref, g_ref, sc_ref, sh_ref, h_ref):
    h_ref[...] = _norm_mod(x_ref[...], g_ref[...], sc_ref[...], sh_ref[...]).astype(h_ref.dtype)


def _normmod(tok, x, norm_g_row, modr, layer, sc_i, sh_i):
    tm = TOKEN_TILE
    return pl.pallas_call(
        _normmod_kernel,
        grid=(tok.total // tm,),
        in_specs=[
            pl.BlockSpec((tm, D_MODEL), lambda i: (i, 0)),
            pl.BlockSpec((1, D_MODEL), lambda i: (0, 0)),
            _mod_spec(tok, tm, layer, sc_i, 1),
            _mod_spec(tok, tm, layer, sh_i, 1),
        ],
        out_specs=pl.BlockSpec((tm, D_MODEL), lambda i: (i, 0)),
        out_shape=jax.ShapeDtypeStruct((tok.total, D_MODEL), BF16),
        compiler_params=_cparams("parallel"),
        name="first_norm",
    )(x, norm_g_row, modr, modr)


def _ffn_kernel(h_ref, x_ref, gt_ref, w1_ref, w3_ref, w2_ref, gn_ref, scn_ref, shn_ref,
                xo_ref, ho_ref, acc_ref, *, n_ff):
    j = pl.program_id(1)
    h = h_ref[...]
    a = _dot(h, w1_ref[...])
    b = _dot(h, w3_ref[...])
    p = (_silu(a) * b).astype(BF16)
    part = _dot(p, w2_ref[...])

    @pl.when(j == 0)
    def _():
        acc_ref[...] = part

    @pl.when(j > 0)
    def _():
        acc_ref[...] += part

    @pl.when(j == n_ff - 1)
    def _():
        xn = x_ref[...] + 0.5 * gt_ref[...] * acc_ref[...]
        xo_ref[...] = xn
        ho_ref[...] = _norm_mod(xn, gn_ref[...], scn_ref[...], shn_ref[...]).astype(ho_ref.dtype)


def _ffn(tok, h, x, modr, layer, which, gate_i, w1, w3, w2, next_g, next_sc, next_sh, next_dtype):
    tm, tf = TOKEN_TILE, FF_TILE
    n_ff = D_FF // tf
    row = pl.BlockSpec((tm, D_MODEL), lambda i, j: (i, 0))
    return pl.pallas_call(
        functools.partial(_ffn_kernel, n_ff=n_ff),
        grid=(tok.total // tm, n_ff),
        in_specs=[
            row, row,
            _mod_spec(tok, tm, layer, gate_i, 2),
            pl.BlockSpec((None, None, D_MODEL, tf), lambda i, j: (layer, which, 0, j)),
            pl.BlockSpec((None, None, D_MODEL, tf), lambda i, j: (layer, which, 0, j)),
            pl.BlockSpec((None, None, tf, D_MODEL), lambda i, j: (layer, which, j, 0)),
            pl.BlockSpec((1, D_MODEL), lambda i, j: (0, 0)),
            next_sc[1], next_sh[1],
        ],
        out_specs=[row, row],
        out_shape=[jax.ShapeDtypeStruct((tok.total, D_MODEL), F32),
                   jax.ShapeDtypeStruct((tok.total, D_MODEL), next_dtype)],
        scratch_shapes=[pltpu.VMEM((tm, D_MODEL), F32)],
        compiler_params=_cparams("parallel", "arbitrary"),
        name="swiglu_half_step",
    )(h, x, modr, w1, w3, w2, next_g, next_sc[0], next_sh[0])


def _matmul_kernel(h_ref, w_ref, o_ref):
    o_ref[...] = _dot(h_ref[...], w_ref[...])


def _project(h, w, tn):
    t, k = h.shape
    n = w.shape[1]
    tm = TOKEN_TILE
    return pl.pallas_call(
        _matmul_kernel,
        grid=(n // tn, t // tm),
        in_specs=[pl.BlockSpec((tm, k), lambda j, i: (i, 0)),
                  pl.BlockSpec((k, tn), lambda j, i: (0, j))],
        out_specs=pl.BlockSpec((tm, tn), lambda j, i: (i, j)),
        out_shape=jax.ShapeDtypeStruct((t, n), F32),
        compiler_params=_cparams("parallel", "parallel"),
        name="mix_in_proj",
    )(h, w)


def _gmlp_kernel(za_ref, ng_ref, ws_ref, bs_ref, o_ref):
    z = jax.nn.gelu(za_ref[...])
    outs = []
    for g in range(A_GROUPS):
        u = z[:, g * A_GDIM:(g + 1) * A_GDIM]
        v = z[:, A_WIDTH + g * A_GDIM:A_WIDTH + (g + 1) * A_GDIM]
        ms = jnp.mean(v * v, axis=-1, keepdims=True)
        v = v * lax.rsqrt(ms + EPS) * ng_ref[g:g + 1, :]
        mixed = _dot(ws_ref[g], v.astype(BF16)) + bs_ref[g]
        outs.append(u * mixed)
    o_ref[...] = jnp.concatenate(outs, axis=-1).astype(o_ref.dtype)


def _gmlp(za, ng, ws, bs):
    t = za.shape[0]
    return pl.pallas_call(
        _gmlp_kernel,
        grid=(t // CHUNK_MLP,),
        in_specs=[
            pl.BlockSpec((CHUNK_MLP, A_COLS), lambda i: (i, 0)),
            pl.BlockSpec((A_GROUPS, A_GDIM), lambda i: (0, 0)),
            pl.BlockSpec((A_GROUPS, CHUNK_MLP, CHUNK_MLP), lambda i: (0, 0, 0)),
            pl.BlockSpec((A_GROUPS, CHUNK_MLP, 1), lambda i: (0, 0, 0)),
        ],
        out_specs=pl.BlockSpec((CHUNK_MLP, A_WIDTH), lambda i: (i, 0)),
        out_shape=jax.ShapeDtypeStruct((t, A_WIDTH), BF16),
        compiler_params=_cparams("parallel"),
        name="gmlp_chunk_mix",
    )(za, ng, ws.astype(BF16), bs[:, :, None])


def _head_ones():
    r = _iota((LANES, LANES), 0) // B_HDIM
    c = _iota((LANES, LANES), 1) // B_HDIM
    return (r == c).astype(BF16)


def _head_sum(x, parts=2):
    ones = _head_ones()
    outs = []
    for s in range(x.shape[-1] // LANES):
        outs.append(_exact_right(x[:, s * LANES:(s + 1) * LANES], ones, parts))
    return jnp.concatenate(outs, axis=-1)


def _rwkv_prep_kernel(z_ref, zp_ref, zn_ref, mu_ref, w0_ref, w2_ref, a0_ref, a2_ref, g2_ref,
                      kkw_ref, kaw_ref, rk_ref,
                      at_ref, rt_ref, bt_ref, kt_ref, v_ref, gl_ref, gate_ref, bonus_ref,
                      *, n_ctx_tiles, ctx_tiles_per_seq, lat_tiles_per_seq):
    i = pl.program_id(0)
    pos = jnp.where(i < n_ctx_tiles, i % ctx_tiles_per_seq, (i - n_ctx_tiles) % lat_tiles_per_seq)
    last = jnp.where(i < n_ctx_tiles, ctx_tiles_per_seq - 1, lat_tiles_per_seq - 1)
    z = z_ref[...]
    t = z.shape[0]
    row = _iota((t, 1), 0)
    halo_prev = jnp.where(pos != 0, zp_ref[7:8, :], 0.0)
    halo_next = jnp.where(pos != last, zn_ref[0:1, :], 0.0)
    prev = jnp.where(row == 0, halo_prev, pltpu.roll(z, 1, 0))
    nxt = jnp.where(row == t - 1, halo_next, pltpu.roll(z, t - 1, 0))
    z = z + (0.5 * (prev + nxt) - z) * mu_ref[...]

    w = B_WIDTH
    r, k, v = z[:, 0:w], z[:, w:2 * w], z[:, 2 * w:3 * w]
    wd = z[:, 3 * w:3 * w + 2 * LORA]
    ad = z[:, 3 * w + 2 * LORA:3 * w + 4 * LORA]
    gd = z[:, 3 * w + 4 * LORA:]
    twd = jnp.tanh(wd).astype(BF16)
    adb = ad.astype(BF16)

    kk = k * kkw_ref[...]
    nrm = jnp.sqrt(_head_sum(kk * kk))
    kk = kk / jnp.maximum(nrm, 1e-12)

    v_ref[...] = v.astype(BF16)
    gate_ref[...] = _dot(_sigmoid(gd).astype(BF16), g2_ref[...])
    cl = RW_CHUNK
    ri, ci = _iota((t, t), 0), _iota((t, t), 1)
    same_chunk = (ri // cl) == (ci // cl)
    ksum = jnp.zeros_like(k)
    for d in range(2):
        w_raw = w0_ref[d:d + 1, :] + _dot(twd[:, d * LORA:(d + 1) * LORA], w2_ref[d])
        lw = -jnp.exp(-jax.nn.softplus(-w_raw) - 0.5)
        a = _sigmoid(a0_ref[d:d + 1, :] + _dot(adb[:, d * LORA:(d + 1) * LORA], a2_ref[d]))
        kd = k * (1.0 + (a - 1.0) * kaw_ref[...])
        ksum = ksum + kd
        upto = same_chunk & ((ci >= ri) if d == 1 else (ci <= ri))
        g = _exact_left(upto.astype(BF16), lw, 3)
        ieg = jnp.exp(-g)
        at_ref[d] = (-kk * jnp.exp(g - lw)).astype(BF16)
        rt_ref[d] = (r * jnp.exp(g)).astype(BF16)
        bt_ref[d] = (kk * a * ieg).astype(BF16)
        kt_ref[d] = (kd * ieg).astype(BF16)
        for c in range(t // cl):
            end = c * cl if d == 1 else (c + 1) * cl - 1
            gl_ref[d, c] = jnp.exp(g[end:end + 1, :])
    bonus_ref[...] = _head_sum(r * ksum * rk_ref[...]) * v


def _rwkv_prep(tok, zb, mu, w0, w2, a0, a2, g2, kkw, kaw, rk):
    tp = PREP_TILE
    n_tiles = tok.total // tp
    rows8 = tok.total // 8
    per = tp // 8
    full = lambda shape: pl.BlockSpec(shape, lambda i: (0,) * len(shape))
    cpt = tp // RW_CHUNK
    tile = pl.BlockSpec((tp, B_WIDTH), lambda i: (i, 0))
    tile2 = pl.BlockSpec((2, tp, B_WIDTH), lambda i: (0, i, 0))
    gl_spec = pl.BlockSpec((2, cpt, 1, B_WIDTH), lambda i: (0, i, 0, 0))
    o1 = jax.ShapeDtypeStruct((tok.total, B_WIDTH), F32)
    o1b = jax.ShapeDtypeStruct((tok.total, B_WIDTH), BF16)
    o2b = jax.ShapeDtypeStruct((2, tok.total, B_WIDTH), BF16)
    ogl = jax.ShapeDtypeStruct((2, tok.total // RW_CHUNK, 1, B_WIDTH), F32)
    kern = functools.partial(_rwkv_prep_kernel, n_ctx_tiles=tok.ctx_tokens // tp,
                             ctx_tiles_per_seq=tok.ctx_len // tp, lat_tiles_per_seq=tok.lat_len // tp)
    return pl.pallas_call(
        kern,
        grid=(n_tiles,),
        in_specs=[
            pl.BlockSpec((tp, B_COLS), lambda i: (i, 0)),
            pl.BlockSpec((8, B_COLS), lambda i: (jnp.maximum(i * per - 1, 0), 0)),
            pl.BlockSpec((8, B_COLS), lambda i: (jnp.minimum((i + 1) * per, rows8 - 1), 0)),
            full((1, B_COLS)), full((2, B_WIDTH)), full((2, LORA, B_WIDTH)), full((2, B_WIDTH)),
            full((2, LORA, B_WIDTH)), full((GATE_LORA, B_WIDTH)),
            full((1, B_WIDTH)), full((1, B_WIDTH)), full((1, B_WIDTH)),
        ],
        out_specs=[tile2, tile2, tile2, tile2, tile, gl_spec, tile, tile],
        out_shape=[o2b, o2b, o2b, o2b, o1b, ogl, o1, o1],
        compiler_params=_cparams("parallel"),
        name="rwkv_prep",
    )(zb, zb, zb, mu, w0, w2.astype(BF16), a0, a2.astype(BF16), g2.astype(BF16), kkw, kaw, rk)


def _pair_blockdiag(x):
    lane = _iota(x.shape, 1)
    zero = jnp.zeros_like(x)
    return jnp.concatenate([jnp.where(lane < B_HDIM, x, zero), jnp.where(lane >= B_HDIM, x, zero)], axis=0)


def _rwkv_chunks(streams, s_ref):
    c = streams[0][0].shape[0]
    n_pairs = B_HEADS // 2
    rr, cc = _iota((2 * c, 2 * c), 0), _iota((2 * c, 2 * c), 1)
    same = (rr // c) == (cc // c)
    tt, ss = rr % c, cc % c
    eye = (rr == cc).astype(F32)
    pair_masks = []
    n = 1
    while n < c:
        pair_masks.append(same & ((tt // (2 * n)) == (ss // (2 * n))) & ((tt // n) != (ss // n)))
        n *= 2

    units = []
    for d, (a_t, r_t, b_t, k_t, vb, gl, reverse) in enumerate(streams):
        before = same & ((ss > tt) if reverse else (ss < tt))
        upto = same & ((ss >= tt) if reverse else (ss <= tt))
        for p in range(n_pairs):
            sl = slice(p * LANES, (p + 1) * LANES)
            units.append(dict(d=d, p=p, before=before, upto=upto, gl=gl[:, sl],
                              ar=jnp.concatenate([_pair_blockdiag(a_t[:, sl]), _pair_blockdiag(r_t[:, sl])], 0),
                              bk=jnp.concatenate([_pair_blockdiag(b_t[:, sl]), _pair_blockdiag(k_t[:, sl])], 0),
                              v=_pair_blockdiag(vb[:, sl])))
    m = 2 * c
    for u in units:
        u["gram"] = _dot_nt(u["ar"], u["bk"])
        u["s0"] = s_ref[u["d"], u["p"]]
    for u in units:
        u["ws"] = _dot_nt(u["ar"], u["s0"].astype(BF16))
        g = u["gram"]
        u["lmat"] = jnp.where(u["before"], g[:m, :m], 0.0)
        aak = jnp.where(u["before"], g[:m, m:], 0.0).astype(BF16)
        u["rbk"] = jnp.concatenate([jnp.where(u["upto"], g[m:, :m], 0.0),
                                    jnp.where(u["upto"], g[m:, m:], 0.0)], axis=1).astype(BF16)
        u["x"] = u["ws"][:m] + _dot(aak, u["v"])
    for lvl, pm in enumerate(pair_masks):
        for u in units:
            link = jnp.where(pm, u["lmat"], 0.0)
            if lvl == 0:
                u["tinv"] = eye + link
            else:
                u["tb"] = u["tinv"].astype(BF16)
                u["tmp"] = _dot(link.astype(BF16), u["tb"]).astype(BF16)
        if lvl > 0:
            for u in units:
                u["tinv"] = u["tinv"] + _dot(u["tb"], u["tmp"])
    for u in units:
        u["uv"] = jnp.concatenate([_dot(u["tinv"].astype(BF16), u["x"].astype(BF16)).astype(BF16), u["v"]], 0)
    ys = [[None] * n_pairs for _ in streams]
    for u in units:
        y = u["ws"][m:] + _dot(u["rbk"], u["uv"])
        ys[u["d"]][u["p"]] = y[:c] + y[c:]
        s_ref[u["d"], u["p"]] = (u["s0"] + _dot_tn(u["uv"], u["bk"])) * u["gl"]
    return [jnp.concatenate(row, axis=-1) for row in ys]


def _rwkv_scan_kernel(tab_ref, atf_ref, rtf_ref, btf_ref, ktf_ref, vf_ref, glf_ref,
                      atb_ref, rtb_ref, btb_ref, ktb_ref, vb_ref, glb_ref, s0_ref,
                      yf_ref, yb_ref, sfin_ref, s_ref):
    step = pl.program_id(0)
    hd = B_HDIM

    @pl.when(tab_ref[3, step] == 1)
    def _():
        z = jnp.zeros((hd, hd), F32)
        for d in range(2):
            for p in range(B_HEADS // 2):
                s_ref[d, p] = jnp.concatenate(
                    [jnp.concatenate([s0_ref[d, 2 * p], z], axis=1),
                     jnp.concatenate([z, s0_ref[d, 2 * p + 1]], axis=1)], axis=0)

    yf, yb = _rwkv_chunks(
        [(atf_ref[...], rtf_ref[...], btf_ref[...], ktf_ref[...], vf_ref[...], glf_ref[...], False),
         (atb_ref[...], rtb_ref[...], btb_ref[...], ktb_ref[...], vb_ref[...], glb_ref[...], True)], s_ref)
    yf_ref[...] = yf
    yb_ref[...] = yb

    @pl.when(tab_ref[4, step] == 1)
    def _():
        for d in range(2):
            for p in range(B_HEADS // 2):
                s = s_ref[d, p]
                sfin_ref[d, 2 * p] = s[:hd, :hd]
                sfin_ref[d, 2 * p + 1] = s[hd:, hd:]


def _scan_table(tok, chunk):
    cols = []
    for s in range(tok.n_ctx + tok.n_lat):
        if s < tok.n_ctx:
            base, nc = s * tok.ctx_len // chunk, tok.ctx_len // chunk
        else:
            base = (tok.ctx_tokens + (s - tok.n_ctx) * tok.lat_len) // chunk
            nc = tok.lat_len // chunk
        for c in range(nc):
            cols.append((base + c, base + nc - 1 - c, s, int(c == 0), int(c == nc - 1)))
    return jnp.asarray(list(zip(*cols)), dtype=jnp.int32)


def _rwkv_scan(tok, at, rt, bt, kt, v, gl, s0):
    cl = RW_CHUNK
    tab = _scan_table(tok, cl)
    fwd = lambda s, t: (t[0, s], 0)
    bwd = lambda s, t: (t[1, s], 0)
    fwd2 = lambda s, t: (0, t[0, s], 0)
    bwd2 = lambda s, t: (1, t[1, s], 0)
    blk = lambda im: pl.BlockSpec((cl, B_WIDTH), im)
    blk2 = lambda im: pl.BlockSpec((None, cl, B_WIDTH), im)
    glf = pl.BlockSpec((None, None, 1, B_WIDTH), lambda s, t: (0, t[0, s], 0, 0))
    glb = pl.BlockSpec((None, None, 1, B_WIDTH), lambda s, t: (1, t[1, s], 0, 0))
    st = pl.BlockSpec((None, 2, B_HEADS, B_HDIM, B_HDIM), lambda s, t: (t[2, s], 0, 0, 0, 0))
    yshape = jax.ShapeDtypeStruct((tok.total, B_WIDTH), F32)
    return pl.pallas_call(
        _rwkv_scan_kernel,
        grid_spec=pltpu.PrefetchScalarGridSpec(
            num_scalar_prefetch=1,
            grid=(tab.shape[1],),
            in_specs=[blk2(fwd2), blk2(fwd2), blk2(fwd2), blk2(fwd2), blk(fwd), glf,
                      blk2(bwd2), blk2(bwd2), blk2(bwd2), blk2(bwd2), blk(bwd), glb, st],
            out_specs=[blk(fwd), blk(bwd), st],
            scratch_shapes=[pltpu.VMEM((2, B_HEADS // 2, LANES, LANES), F32)]),
        out_shape=[yshape, yshape, jax.ShapeDtypeStruct(s0.shape, F32)],
        compiler_params=_cparams("arbitrary"),
        name="rwkv7_scan",
    )(tab, at, rt, bt, kt, v, gl, at, rt, bt, kt, v, gl, s0)


def _rwkv_post_kernel(yf_ref, yb_ref, bonus_ref, gate_ref, g_ref, b_ref, o_ref):
    y = yf_ref[...] + yb_ref[...]
    mean = _head_sum(y, 3) * (1.0 / B_HDIM)
    yc = y - mean
    var = _head_sum(yc * yc) * (1.0 / B_HDIM)
    y = yc * lax.rsqrt(var + LNX_EPS) * g_ref[...] + b_ref[...] + bonus_ref[...]
    o_ref[...] = (y * gate_ref[...]).astype(o_ref.dtype)


def _rwkv_post(yf, yb, bonus, gate, lnx_g, lnx_b):
    t = yf.shape[0]
    tm = TOKEN_TILE
    tile = pl.BlockSpec((tm, B_WIDTH), lambda i: (i, 0))
    vec = pl.BlockSpec((1, B_WIDTH), lambda i: (0, 0))
    return pl.pallas_call(
        _rwkv_post_kernel,
        grid=(t // tm,),
        in_specs=[tile, tile, tile, tile, vec, vec],
        out_specs=tile,
        out_shape=jax.ShapeDtypeStruct((t, B_WIDTH), BF16),
        compiler_params=_cparams("parallel"),
        name="rwkv_post",
    )(yf, yb, bonus, gate, lnx_g, lnx_b)


def _hgrn_block(qraw, fraw, v, lb, st_ref, reverse):
    n = qraw.shape[0]
    sub = HG_SUB
    fg = lb + (1.0 - lb) * _sigmoid(fraw)
    logf = jnp.log(fg)
    kg = 1.0 - fg
    q = _silu(qraw)
    ri, ci = _iota((n, n), 0), _iota((n, n), 1)
    same = (ri // sub) == (ci // sub)
    upto = same & ((ci >= ri) if reverse else (ci <= ri))
    bcum = _exact_left(upto.astype(BF16), logf, 3)
    ones = jnp.ones((LANES, LANES), BF16)
    half = 8
    hi = _iota((half, 1), 0)
    order = range(n // sub - 1, -1, -1) if reverse else range(n // sub)
    out_rows = [None] * (n // sub)
    for sc in order:
        last = sc * sub if reverse else (sc + 1) * sub - 1
        outs = []
        for h in range(C_HEADS):
            cols = slice(h * C_KDIM, (h + 1) * C_KDIM)
            rows = slice(sc * sub, (sc + 1) * sub)
            qs, ks, vs, bs = q[rows, cols], kg[rows, cols], v[rows, cols], bcum[rows, cols]
            b_end = bcum[last:last + 1, cols]
            slabs, plan = [], []
            for j in range(sub):
                for hf in range(sub // half):
                    rs = slice(hf * half, (hf + 1) * half)
                    diff = bs[rs] - bs[j:j + 1, :]
                    if hf == j // half:
                        live = (hi + hf * half <= j) if reverse else (hi + hf * half >= j)
                        diff = jnp.where(live, diff, -1e30)
                    elif (hf > j // half) == reverse:
                        continue
                    slabs.append(qs[rs] * jnp.exp(diff) * ks[j:j + 1, :])
                    plan.append((j, hf))
            att = _dot(jnp.concatenate(slabs, axis=0).astype(BF16), ones)
            o = _dot_nt((qs * jnp.exp(bs)).astype(BF16), st_ref[h].astype(BF16))
            o_half = [o[hf * half:(hf + 1) * half] for hf in range(sub // half)]
            for idx, (j, hf) in enumerate(plan):
                o_half[hf] = o_half[hf] + att[idx * half:(idx + 1) * half, :] * vs[j:j + 1, :]
            outs.append(jnp.concatenate(o_half, axis=0))
            ke = (ks * jnp.exp(b_end - bs)).astype(BF16)
            st_ref[h] = st_ref[h] * jnp.exp(b_end) + _dot_tn(vs.astype(BF16), ke)
        out_rows[sc] = jnp.concatenate(outs, axis=-1)
    return jnp.concatenate(out_rows, axis=0)


def _hgrn_scan_kernel(tab_ref, qf_ref, ff_ref, vf_ref, qb_ref, fb_ref, vb_ref, lb_ref, s0_ref,
                      of_ref, ob_ref, sfin_ref, st_ref):
    step = pl.program_id(0)

    @pl.when(tab_ref[3, step] == 1)
    def _():
        for d in range(2):
            for h in range(C_HEADS):
                st_ref[d, h] = s0_ref[d, h].T

    of_ref[...] = _hgrn_block(qf_ref[...], ff_ref[...], vf_ref[...], lb_ref[0:1, :], st_ref.at[0], False)
    ob_ref[...] = _hgrn_block(qb_ref[...], fb_ref[...], vb_ref[...], lb_ref[1:2, :], st_ref.at[1], True)

    @pl.when(tab_ref[4, step] == 1)
    def _():
        for d in range(2):
            for h in range(C_HEADS):
                sfin_ref[d, h] = st_ref[d, h].T


def _hgrn_scan(tok, zc, lb, s0):
    cl = HG_BLOCK
    tab = _scan_table(tok, cl)
    fwd = lambda col: (lambda s, t: (t[0, s], col))
    bwd = lambda col: (lambda s, t: (t[1, s], col))
    blk = lambda im: pl.BlockSpec((cl, C_WIDTH), im)
    st = pl.BlockSpec((None, 2, C_HEADS, C_KDIM, C_KDIM), lambda s, t: (t[2, s], 0, 0, 0, 0))
    oshape = jax.ShapeDtypeStruct((tok.total, C_WIDTH), F32)
    return pl.pallas_call(
        _hgrn_scan_kernel,
        grid_spec=pltpu.PrefetchScalarGridSpec(
            num_scalar_prefetch=1,
            grid=(tab.shape[1],),
            in_specs=[blk(fwd(0)), blk(fwd(1)), blk(fwd(3)), blk(bwd(0)), blk(bwd(2)), blk(bwd(3)),
                      pl.BlockSpec((2, C_WIDTH), lambda s, t: (0, 0)), st],
            out_specs=[blk(fwd(0)), blk(bwd(0)), st],
            scratch_shapes=[pltpu.VMEM((2, C_HEADS, C_KDIM, C_KDIM), F32)]),
        out_shape=[oshape, oshape, jax.ShapeDtypeStruct(s0.shape, F32)],
        compiler_params=_cparams("arbitrary"),
        name="hgrn2_scan",
    )(tab, zc, zc, zc, zc, zc, zc, lb, s0)


def _hgrn_post_kernel(of_ref, ob_ref, g_ref, gn_ref, o_ref):
    o = of_ref[...] + ob_ref[...]
    gate = _silu(g_ref[...])
    outs = []
    for h in range(C_HEADS):
        cols = slice(h * C_KDIM, (h + 1) * C_KDIM)
        oh = o[:, cols]
        ms = jnp.mean(oh * oh, axis=-1, keepdims=True)
        outs.append(oh * lax.rsqrt(ms + EPS) * gn_ref[...] * gate[:, cols])
    o_ref[...] = jnp.concatenate(outs, axis=-1).astype(o_ref.dtype)


def _hgrn_post(of, ob, zc, gn):
    t = of.shape[0]
    tm = TOKEN_TILE
    tile = pl.BlockSpec((tm, C_WIDTH), lambda i: (i, 0))
    return pl.pallas_call(
        _hgrn_post_kernel,
        grid=(t // tm,),
        in_specs=[tile, tile, pl.BlockSpec((tm, C_WIDTH), lambda i: (i, 4)),
                  pl.BlockSpec((1, C_KDIM), lambda i: (0, 0))],
        out_specs=tile,
        out_shape=jax.ShapeDtypeStruct((t, C_WIDTH), BF16),
        compiler_params=_cparams("parallel"),
        name="hgrn2_post",
    )(of, ob, zc, gn)


def _mix_out_kernel(oa_ref, ob_ref, oc_ref, wa_ref, wb_ref, wc_ref, x_ref, gt_ref,
                    gn_ref, scn_ref, shn_ref, xo_ref, ho_ref):
    y = _dot(oa_ref[...], wa_ref[...]) + _dot(ob_ref[...], wb_ref[...]) + _dot(oc_ref[...], wc_ref[...])
    xn = x_ref[...] + gt_ref[...] * y
    xo_ref[...] = xn
    ho_ref[...] = _norm_mod(xn, gn_ref[...], scn_ref[...], shn_ref[...]).astype(ho_ref.dtype)


def _mix_out(tok, oa, ob, oc, wa, wb, wc, x, modr, layer, next_g):
    tm = TOKEN_TILE
    row = pl.BlockSpec((tm, D_MODEL), lambda i: (i, 0))
    full = lambda a: pl.BlockSpec(a.shape, lambda i: (0, 0))
    return pl.pallas_call(
        _mix_out_kernel,
        grid=(tok.total // tm,),
        in_specs=[pl.BlockSpec((tm, A_WIDTH), lambda i: (i, 0)),
                  pl.BlockSpec((tm, B_WIDTH), lambda i: (i, 0)),
                  pl.BlockSpec((tm, C_WIDTH), lambda i: (i, 0)),
                  full(wa), full(wb), full(wc), row,
                  _mod_spec(tok, tm, layer, 5, 1),
                  pl.BlockSpec((1, D_MODEL), lambda i: (0, 0)),
                  _mod_spec(tok, tm, layer, 7, 1), _mod_spec(tok, tm, layer, 6, 1)],
        out_specs=[row, row],
        out_shape=[jax.ShapeDtypeStruct((tok.total, D_MODEL), F32),
                   jax.ShapeDtypeStruct((tok.total, D_MODEL), BF16)],
        compiler_params=_cparams("parallel"),
        name="mix_out_proj",
    )(oa, ob, oc, wa, wb, wc, x, modr, next_g, modr, modr)


def kernel(x_prompt, x_sample, state_rwkv, state_hgrn, c, c_ctx, norm_g, w_mod, b_mod, ffn_w1, ffn_w3,
           ffn_w2, w_in, w_out, mlp_norm_g, mlp_ws, mlp_bs, rwkv_mu, rwkv_w0, rwkv_w2, rwkv_a0, rwkv_a2,
           rwkv_g2, rwkv_kk, rwkv_ka, rwkv_rk, rwkv_lnx_g, rwkv_lnx_b, hgrn_lb, hgrn_gn, final_g):
    n_ctx, ctx_len, _ = x_prompt.shape
    n_lat, lat_len, _ = x_sample.shape
    depth = w_mod.shape[0]
    tok = _Tokens(n_ctx, ctx_len, n_lat, lat_len)
    assert n_lat + 1 <= 8 and ctx_len % PREP_TILE == 0 and lat_len % TOKEN_TILE == 0

    cond8 = jnp.zeros((8, D_MODEL), F32).at[0].set(c_ctx).at[1:1 + n_lat].set(c)
    mod = _mod_rows(cond8, w_mod, b_mod)
    modr = mod.reshape(depth, 8, N_MOD, D_MODEL).transpose(0, 2, 1, 3)[:, :, :, None, :]

    sm = jax.nn.softmax(hgrn_lb.astype(F32), axis=0)
    lower = jnp.cumsum(sm, axis=0) - sm[0]

    x = jnp.concatenate([x_prompt.reshape(-1, D_MODEL), x_sample.reshape(-1, D_MODEL)], axis=0)
    zero_sc = jnp.zeros((1, D_MODEL), F32)
    zero_spec = pl.BlockSpec((1, D_MODEL), lambda i, j: (0, 0))
    tm = TOKEN_TILE

    h = _normmod(tok, x, norm_g[0, 0][None], modr, 0, 1, 0)
    bw = lambda a: a.astype(BF16)
    w1b, w3b, w2b = bw(ffn_w1), bw(ffn_w3), bw(ffn_w2)
    rw_states, hg_states = [], []
    for l in range(depth):
        x, h = _ffn(tok, h, x, modr, l, 0, 2, w1b, w3b, w2b,
                    norm_g[l, 1][None], (modr, _mod_spec(tok, tm, l, 4, 2)),
                    (modr, _mod_spec(tok, tm, l, 3, 2)), BF16)

        za = _project(h, bw(w_in[l, :, :A_COLS]), 512)
        zb = _project(h, bw(w_in[l, :, A_COLS:A_COLS + B_COLS]), 896)
        zc = _project(h, bw(w_in[l, :, A_COLS + B_COLS:]), 768)

        oa = _gmlp(za, mlp_norm_g[l], mlp_ws[l], mlp_bs[l])

        at, rt, bt, kt, v, gl, gate, bonus = _rwkv_prep(
            tok, zb, rwkv_mu[l][None], rwkv_w0[l], rwkv_w2[l], rwkv_a0[l], rwkv_a2[l], rwkv_g2[l],
            rwkv_kk[l][None], rwkv_ka[l][None], rwkv_rk[l].reshape(1, B_WIDTH))
        s0 = jnp.concatenate([jnp.zeros((n_ctx, 2, B_HEADS, B_HDIM, B_HDIM), F32), state_rwkv[:, l]], 0)
        yf, yb, s_rw = _rwkv_scan(tok, at, rt, bt, kt, v, gl, s0)
        ob = _rwkv_post(yf, yb, bonus, gate,
                        rwkv_lnx_g[l].reshape(1, B_WIDTH), rwkv_lnx_b[l].reshape(1, B_WIDTH))
        rw_states.append(s_rw[:n_ctx])

        s0 = jnp.concatenate([jnp.zeros((n_ctx, 2, C_HEADS, C_KDIM, C_KDIM), F32), state_hgrn[:, l]], 0)
        of, obw, s_hg = _hgrn_scan(tok, zc, lower[l], s0)
        oc = _hgrn_post(of, obw, zc, hgrn_gn[l][None])
        hg_states.append(s_hg[:n_ctx])

        w_out_l = bw(w_out[l])
        x, h = _mix_out(tok, oa, ob, oc, w_out_l[:A_WIDTH], w_out_l[A_WIDTH:A_WIDTH + B_WIDTH],
                        w_out_l[A_WIDTH + B_WIDTH:], x, modr, l, norm_g[l, 2][None])

        if l + 1 < depth:
            x, h = _ffn(tok, h, x, modr, l, 1, 8, w1b, w3b, w2b,
                        norm_g[l + 1, 0][None], (modr, _mod_spec(tok, tm, l + 1, 1, 2)),
                        (modr, _mod_spec(tok, tm, l + 1, 0, 2)), BF16)
        else:
            x, h = _ffn(tok, h, x, modr, l, 1, 8, w1b, w3b, w2b,
                        final_g[None], (zero_sc, zero_spec), (zero_sc, zero_spec), F32)

    y_prompt = h[:tok.ctx_tokens].reshape(x_prompt.shape)
    y_sample = h[tok.ctx_tokens:].reshape(x_sample.shape)
    return (y_prompt, y_sample, jnp.stack(rw_states, axis=1), jnp.stack(hg_states, axis=1))
```

```python
import functools

import jax
import jax.numpy as jnp
from jax import lax
from jax.experimental import pallas as pl
from jax.experimental.pallas import tpu as pltpu

F32 = jnp.float32
BF16 = jnp.bfloat16

D_MODEL = 2048
D_FF = 5632
N_MOD = 9
EPS = 1e-6
LNX_EPS = 64e-5

CHUNK_MLP = 128
A_WIDTH = 512
A_GROUPS = 4
A_GDIM = 128
B_HDIM = 64
B_WIDTH = 768
B_HEADS = 12
LORA = 64
GATE_LORA = 128
B_COLS = 3 * B_WIDTH + 4 * LORA + GATE_LORA
C_KDIM = 128
C_WIDTH = 768
C_HEADS = 6
C_COLS = 5 * C_WIDTH
A_COLS = 2 * A_WIDTH

LANES = 128
VMEM_LIMIT = 56 * 1024 * 1024

TOKEN_TILE = 512
FF_TILE = 512
PREP_TILE = 256
RW_CHUNK = 64
HG_BLOCK = 64
HG_SUB = 16


def _cparams(*sem):
    return pltpu.CompilerParams(dimension_semantics=sem, vmem_limit_bytes=VMEM_LIMIT)


def _dot(a, b):
    return jnp.dot(a, b, preferred_element_type=F32)


def _dot_nt(a, b):
    return lax.dot_general(a, b, (((1,), (1,)), ((), ())), preferred_element_type=F32)


def _dot_tn(a, b):
    return lax.dot_general(a, b, (((0,), (0,)), ((), ())), preferred_element_type=F32)


def _split(x, parts):
    out = []
    for _ in range(parts - 1):
        hi = x.astype(BF16)
        out.append(hi)
        x = x - hi.astype(F32)
    out.append(x.astype(BF16))
    return out


def _exact_left(m, x, parts):
    acc = None
    for p in _split(x, parts):
        t = _dot(m, p)
        acc = t if acc is None else acc + t
    return acc


def _exact_right(x, m, parts):
    acc = None
    for p in _split(x, parts):
        t = _dot(p, m)
        acc = t if acc is None else acc + t
    return acc


def _iota(shape, dim):
    return lax.broadcasted_iota(jnp.int32, shape, dim)


def _sigmoid(x):
    return jax.nn.sigmoid(x)


def _silu(x):
    return x * jax.nn.sigmoid(x)


def _norm_mod(x, g, sc, sh):
    ms = jnp.mean(x * x, axis=-1, keepdims=True)
    return (x * lax.rsqrt(ms + EPS) * g) * (1.0 + sc) + sh


def _mod_kernel(cond_ref, w_ref, b_ref, o_ref):
    s = _silu(cond_ref[...]).astype(BF16)
    o_ref[...] = _dot(s, w_ref[...].astype(BF16)) + b_ref[...]


def _mod_rows(cond8, w_mod, b_mod):
    depth, _, n = w_mod.shape
    tn = 1024
    return pl.pallas_call(
        _mod_kernel,
        grid=(depth, n // tn),
        in_specs=[
            pl.BlockSpec((8, D_MODEL), lambda l, j: (0, 0)),
            pl.BlockSpec((None, D_MODEL, tn), lambda l, j: (l, 0, j)),
            pl.BlockSpec((None, 1, tn), lambda l, j: (l, 0, j)),
        ],
        out_specs=pl.BlockSpec((None, 8, tn), lambda l, j: (l, 0, j)),
        out_shape=jax.ShapeDtypeStruct((depth, 8, n), F32),
        compiler_params=_cparams("parallel", "parallel"),
        name="adaln_rows",
    )(cond8, w_mod, b_mod.reshape(depth, 1, n))


class _Tokens:
    def __init__(self, n_ctx, ctx_len, n_lat, lat_len):
        self.n_ctx, self.ctx_len, self.n_lat, self.lat_len = n_ctx, ctx_len, n_lat, lat_len
        self.ctx_tokens = n_ctx * ctx_len
        self.total = self.ctx_tokens + n_lat * lat_len

    def mod_row(self, tile, i):
        nct = self.ctx_tokens // tile
        per_lat = self.lat_len // tile
        return jnp.where(i < nct, 0, 1 + (i - nct) // per_lat)


def _mod_spec(tok, tile, layer, which, grid_rank):
    def imap(*idx):
        return (layer, which, tok.mod_row(tile, idx[0]), 0, 0)
    del grid_rank
    return pl.BlockSpec((None, None, None, 1, D_MODEL), imap)


def _normmod_kernel(x_ref, g_ref, sc_ref, sh_ref, h_ref):
    h_ref[...] = _norm_mod(x_ref[...], g_ref[...], sc_ref[...], sh_ref[...]).astype(h_ref.dtype)


def _normmod(tok, x, norm_g_row, modr, layer, sc_i, sh_i):
    tm = TOKEN_TILE
    return pl.pallas_call(
        _normmod_kernel,
        grid=(tok.total // tm,),
        in_specs=[
            pl.BlockSpec((tm, D_MODEL), lambda i: (i, 0)),
            pl.BlockSpec((1, D_MODEL), lambda i: (0, 0)),
            _mod_spec(tok, tm, layer, sc_i, 1),
            _mod_spec(tok, tm, layer, sh_i, 1),
        ],
        out_specs=pl.BlockSpec((tm, D_MODEL), lambda i: (i, 0)),
        out_shape=jax.ShapeDtypeStruct((tok.total, D_MODEL), BF16),
        compiler_params=_cparams("parallel"),
        name="first_norm",
    )(x, norm_g_row, modr, modr)


def _ffn_kernel(h_ref, x_ref, gt_ref, w1_ref, w3_ref, w2_ref, gn_ref, scn_ref, shn_ref,
                xo_ref, ho_ref, acc_ref, *, n_ff):
    j = pl.program_id(1)

    @pl.when(j == 0)
    def _():
        acc_ref[...] = jnp.zeros_like(acc_ref)

    h = h_ref[...]
    a = _dot(h, w1_ref[...])
    b = _dot(h, w3_ref[...])
    p = (_silu(a) * b).astype(BF16)
    acc_ref[...] += _dot(p, w2_ref[...])

    @pl.when(j == n_ff - 1)
    def _():
        xn = x_ref[...] + 0.5 * gt_ref[...] * acc_ref[...]
        xo_ref[...] = xn
        ho_ref[...] = _norm_mod(xn, gn_ref[...], scn_ref[...], shn_ref[...]).astype(ho_ref.dtype)


def _ffn(tok, h, x, modr, layer, which, gate_i, w1, w3, w2, next_g, next_sc, next_sh, next_dtype):
    tm, tf = TOKEN_TILE, FF_TILE
    n_ff = D_FF // tf
    row = pl.BlockSpec((tm, D_MODEL), lambda i, j: (i, 0))
    return pl.pallas_call(
        functools.partial(_ffn_kernel, n_ff=n_ff),
        grid=(tok.total // tm, n_ff),
        in_specs=[
            row, row,
            _mod_spec(tok, tm, layer, gate_i, 2),
            pl.BlockSpec((None, None, D_MODEL, tf), lambda i, j: (layer, which, 0, j)),
            pl.BlockSpec((None, None, D_MODEL, tf), lambda i, j: (layer, which, 0, j)),
            pl.BlockSpec((None, None, tf, D_MODEL), lambda i, j: (layer, which, j, 0)),
            pl.BlockSpec((1, D_MODEL), lambda i, j: (0, 0)),
            next_sc[1], next_sh[1],
        ],
        out_specs=[row, row],
        out_shape=[jax.ShapeDtypeStruct((tok.total, D_MODEL), F32),
                   jax.ShapeDtypeStruct((tok.total, D_MODEL), next_dtype)],
        scratch_shapes=[pltpu.VMEM((tm, D_MODEL), F32)],
        compiler_params=_cparams("parallel", "arbitrary"),
        name="swiglu_half_step",
    )(h, x, modr, w1, w3, w2, next_g, next_sc[0], next_sh[0])


def _matmul_kernel(h_ref, w_ref, o_ref):
    o_ref[...] = _dot(h_ref[...], w_ref[...])


def _project(h, w, tn):
    t, k = h.shape
    n = w.shape[1]
    tm = TOKEN_TILE
    return pl.pallas_call(
        _matmul_kernel,
        grid=(n // tn, t // tm),
        in_specs=[pl.BlockSpec((tm, k), lambda j, i: (i, 0)),
                  pl.BlockSpec((k, tn), lambda j, i: (0, j))],
        out_specs=pl.BlockSpec((tm, tn), lambda j, i: (i, j)),
        out_shape=jax.ShapeDtypeStruct((t, n), F32),
        compiler_params=_cparams("parallel", "parallel"),
        name="mix_in_proj",
    )(h, w)


def _gmlp_kernel(za_ref, ng_ref, ws_ref, bs_ref, o_ref):
    z = jax.nn.gelu(za_ref[...])
    outs = []
    for g in range(A_GROUPS):
        u = z[:, g * A_GDIM:(g + 1) * A_GDIM]
        v = z[:, A_WIDTH + g * A_GDIM:A_WIDTH + (g + 1) * A_GDIM]
        ms = jnp.mean(v * v, axis=-1, keepdims=True)
        v = v * lax.rsqrt(ms + EPS) * ng_ref[g:g + 1, :]
        mixed = _dot(ws_ref[g], v.astype(BF16)) + bs_ref[g]
        outs.append(u * mixed)
    o_ref[...] = jnp.concatenate(outs, axis=-1).astype(o_ref.dtype)


def _gmlp(za, ng, ws, bs):
    t = za.shape[0]
    return pl.pallas_call(
        _gmlp_kernel,
        grid=(t // CHUNK_MLP,),
        in_specs=[
            pl.BlockSpec((CHUNK_MLP, A_COLS), lambda i: (i, 0)),
            pl.BlockSpec((A_GROUPS, A_GDIM), lambda i: (0, 0)),
            pl.BlockSpec((A_GROUPS, CHUNK_MLP, CHUNK_MLP), lambda i: (0, 0, 0)),
            pl.BlockSpec((A_GROUPS, CHUNK_MLP, 1), lambda i: (0, 0, 0)),
        ],
        out_specs=pl.BlockSpec((CHUNK_MLP, A_WIDTH), lambda i: (i, 0)),
        out_shape=jax.ShapeDtypeStruct((t, A_WIDTH), BF16),
        compiler_params=_cparams("parallel"),
        name="gmlp_chunk_mix",
    )(za, ng, ws.astype(BF16), bs[:, :, None])


def _head_ones():
    r = _iota((LANES, LANES), 0) // B_HDIM
    c = _iota((LANES, LANES), 1) // B_HDIM
    return (r == c).astype(BF16)


def _head_sum(x, parts=2):
    ones = _head_ones()
    outs = []
    for s in range(x.shape[-1] // LANES):
        outs.append(_exact_right(x[:, s * LANES:(s + 1) * LANES], ones, parts))
    return jnp.concatenate(outs, axis=-1)


def _rwkv_prep_kernel(z_ref, zp_ref, zn_ref, mu_ref, w0_ref, w2_ref, a0_ref, a2_ref, g2_ref,
                      kkw_ref, kaw_ref, rk_ref,
                      at_ref, rt_ref, bt_ref, kt_ref, v_ref, gl_ref, gate_ref, bonus_ref,
                      *, n_ctx_tiles, ctx_tiles_per_seq, lat_tiles_per_seq):
    i = pl.program_id(0)
    pos = jnp.where(i < n_ctx_tiles, i % ctx_tiles_per_seq, (i - n_ctx_tiles) % lat_tiles_per_seq)
    last = jnp.where(i < n_ctx_tiles, ctx_tiles_per_seq - 1, lat_tiles_per_seq - 1)
    z = z_ref[...]
    t = z.shape[0]
    row = _iota((t, 1), 0)
    halo_prev = jnp.where(pos != 0, zp_ref[7:8, :], 0.0)
    halo_next = jnp.where(pos != last, zn_ref[0:1, :], 0.0)
    prev = jnp.where(row == 0, halo_prev, pltpu.roll(z, 1, 0))
    nxt = jnp.where(row == t - 1, halo_next, pltpu.roll(z, t - 1, 0))
    z = z + (0.5 * (prev + nxt) - z) * mu_ref[...]

    w = B_WIDTH
    r, k, v = z[:, 0:w], z[:, w:2 * w], z[:, 2 * w:3 * w]
    wd = z[:, 3 * w:3 * w + 2 * LORA]
    ad = z[:, 3 * w + 2 * LORA:3 * w + 4 * LORA]
    gd = z[:, 3 * w + 4 * LORA:]
    twd = jnp.tanh(wd).astype(BF16)
    adb = ad.astype(BF16)

    kk = k * kkw_ref[...]
    nrm = jnp.sqrt(_head_sum(kk * kk))
    kk = kk / jnp.maximum(nrm, 1e-12)

    v_ref[...] = v.astype(BF16)
    gate_ref[...] = _dot(_sigmoid(gd).astype(BF16), g2_ref[...])
    cl = RW_CHUNK
    ri, ci = _iota((t, t), 0), _iota((t, t), 1)
    same_chunk = (ri // cl) == (ci // cl)
    ksum = jnp.zeros_like(k)
    for d in range(2):
        w_raw = w0_ref[d:d + 1, :] + _dot(twd[:, d * LORA:(d + 1) * LORA], w2_ref[d])
        lw = -jnp.exp(-jax.nn.softplus(-w_raw) - 0.5)
        a = _sigmoid(a0_ref[d:d + 1, :] + _dot(adb[:, d * LORA:(d + 1) * LORA], a2_ref[d]))
        kd = k * (1.0 + (a - 1.0) * kaw_ref[...])
        ksum = ksum + kd
        upto = same_chunk & ((ci >= ri) if d == 1 else (ci <= ri))
        g = _exact_left(upto.astype(BF16), lw, 3)
        ieg = jnp.exp(-g)
        at_ref[d] = (-kk * jnp.exp(g - lw)).astype(BF16)
        rt_ref[d] = (r * jnp.exp(g)).astype(BF16)
        bt_ref[d] = (kk * a * ieg).astype(BF16)
        kt_ref[d] = (kd * ieg).astype(BF16)
        for c in range(t // cl):
            end = c * cl if d == 1 else (c + 1) * cl - 1
            gl_ref[d, c] = jnp.exp(g[end:end + 1, :])
    bonus_ref[...] = _head_sum(r * ksum * rk_ref[...]) * v


def _rwkv_prep(tok, zb, mu, w0, w2, a0, a2, g2, kkw, kaw, rk):
    tp = PREP_TILE
    n_tiles = tok.total // tp
    rows8 = tok.total // 8
    per = tp // 8
    full = lambda shape: pl.BlockSpec(shape, lambda i: (0,) * len(shape))
    cpt = tp // RW_CHUNK
    tile = pl.BlockSpec((tp, B_WIDTH), lambda i: (i, 0))
    tile2 = pl.BlockSpec((2, tp, B_WIDTH), lambda i: (0, i, 0))
    gl_spec = pl.BlockSpec((2, cpt, 1, B_WIDTH), lambda i: (0, i, 0, 0))
    o1 = jax.ShapeDtypeStruct((tok.total, B_WIDTH), F32)
    o1b = jax.ShapeDtypeStruct((tok.total, B_WIDTH), BF16)
    o2b = jax.ShapeDtypeStruct((2, tok.total, B_WIDTH), BF16)
    ogl = jax.ShapeDtypeStruct((2, tok.total // RW_CHUNK, 1, B_WIDTH), F32)
    kern = functools.partial(_rwkv_prep_kernel, n_ctx_tiles=tok.ctx_tokens // tp,
                             ctx_tiles_per_seq=tok.ctx_len // tp, lat_tiles_per_seq=tok.lat_len // tp)
    return pl.pallas_call(
        kern,
        grid=(n_tiles,),
        in_specs=[
            pl.BlockSpec((tp, B_COLS), lambda i: (i, 0)),
            pl.BlockSpec((8, B_COLS), lambda i: (jnp.maximum(i * per - 1, 0), 0)),
            pl.BlockSpec((8, B_COLS), lambda i: (jnp.minimum((i + 1) * per, rows8 - 1), 0)),
            full((1, B_COLS)), full((2, B_WIDTH)), full((2, LORA, B_WIDTH)), full((2, B_WIDTH)),
            full((2, LORA, B_WIDTH)), full((GATE_LORA, B_WIDTH)),
            full((1, B_WIDTH)), full((1, B_WIDTH)), full((1, B_WIDTH)),
        ],
        out_specs=[tile2, tile2, tile2, tile2, tile, gl_spec, tile, tile],
        out_shape=[o2b, o2b, o2b, o2b, o1b, ogl, o1, o1],
        compiler_params=_cparams("parallel"),
        name="rwkv_prep",
    )(zb, zb, zb, mu, w0, w2.astype(BF16), a0, a2.astype(BF16), g2.astype(BF16), kkw, kaw, rk)


def _pair_blockdiag(x):
    lane = _iota(x.shape, 1)
    zero = jnp.zeros_like(x)
    return jnp.concatenate([jnp.where(lane < B_HDIM, x, zero), jnp.where(lane >= B_HDIM, x, zero)], axis=0)


def _rwkv_chunks(streams, s_ref, y_refs):
    c = streams[0][0].shape[0]
    n_pairs = B_HEADS // 2
    rr, cc = _iota((2 * c, 2 * c), 0), _iota((2 * c, 2 * c), 1)
    same = (rr // c) == (cc // c)
    tt, ss = rr % c, cc % c
    eye = (rr == cc).astype(F32)
    pair_masks = []
    n = 1
    while n < c:
        pair_masks.append(same & ((tt // (2 * n)) == (ss // (2 * n))) & ((tt // n) != (ss // n)))
        n *= 2

    units = []
    for d, (a_t, r_t, b_t, k_t, vb, gl, reverse) in enumerate(streams):
        before = same & ((ss > tt) if reverse else (ss < tt))
        upto = same & ((ss >= tt) if reverse else (ss <= tt))
        for p in range(n_pairs):
            sl = slice(p * LANES, (p + 1) * LANES)
            units.append(dict(d=d, p=p, before=before, upto=upto, gl=gl[:, sl],
                              ar=jnp.concatenate([_pair_blockdiag(a_t[:, sl]), _pair_blockdiag(r_t[:, sl])], 0),
                              bk=jnp.concatenate([_pair_blockdiag(b_t[:, sl]), _pair_blockdiag(k_t[:, sl])], 0),
                              v=_pair_blockdiag(vb[:, sl])))
    m = 2 * c
    for u in units:
        u["gram"] = _dot_nt(u["ar"], u["bk"])
        u["s0"] = s_ref[u["d"], u["p"]]
    yield
    for u in units:
        u["ws"] = _dot_nt(u["ar"], u["s0"].astype(BF16))
        g = u["gram"]
        u["lmat"] = jnp.where(u["before"], g[:m, :m], 0.0)
        aak = jnp.where(u["before"], g[:m, m:], 0.0).astype(BF16)
        u["rbk"] = jnp.concatenate([jnp.where(u["upto"], g[m:, :m], 0.0),
                                    jnp.where(u["upto"], g[m:, m:], 0.0)], axis=1).astype(BF16)
        u["x"] = u["ws"][:m] + _dot(aak, u["v"])
    yield
    for lvl, pm in enumerate(pair_masks):
        for u in units:
            link = jnp.where(pm, u["lmat"], 0.0)
            if lvl == 0:
                u["tinv"] = eye + link
            else:
                u["tb"] = u["tinv"].astype(BF16)
                u["tmp"] = _dot(link.astype(BF16), u["tb"]).astype(BF16)
        if lvl > 0:
            yield
            for u in units:
                u["tinv"] = u["tinv"] + _dot(u["tb"], u["tmp"])
            yield
    for u in units:
        u["uv"] = jnp.concatenate([_dot(u["tinv"].astype(BF16), u["x"].astype(BF16)).astype(BF16), u["v"]], 0)
    yield
    ys = [[None] * n_pairs for _ in streams]
    for u in units:
        y = u["ws"][m:] + _dot(u["rbk"], u["uv"])
        ys[u["d"]][u["p"]] = y[:c] + y[c:]
        s_ref[u["d"], u["p"]] = (u["s0"] + _dot_tn(u["uv"], u["bk"])) * u["gl"]
    for d, row in enumerate(ys):
        y_refs[d][...] = jnp.concatenate(row, axis=-1)


def _scan_table(tok, chunk):
    cols = []
    for s in range(tok.n_ctx + tok.n_lat):
        if s < tok.n_ctx:
            base, nc = s * tok.ctx_len // chunk, tok.ctx_len // chunk
        else:
            base = (tok.ctx_tokens + (s - tok.n_ctx) * tok.lat_len) // chunk
            nc = tok.lat_len // chunk
        for c in range(nc):
            cols.append((base + c, base + nc - 1 - c, s, int(c == 0), int(c == nc - 1)))
    return jnp.asarray(list(zip(*cols)), dtype=jnp.int32)


def _rwkv_post_kernel(yf_ref, yb_ref, bonus_ref, gate_ref, g_ref, b_ref, o_ref):
    y = yf_ref[...] + yb_ref[...]
    mean = _head_sum(y, 3) * (1.0 / B_HDIM)
    yc = y - mean
    var = _head_sum(yc * yc) * (1.0 / B_HDIM)
    y = yc * lax.rsqrt(var + LNX_EPS) * g_ref[...] + b_ref[...] + bonus_ref[...]
    o_ref[...] = (y * gate_ref[...]).astype(o_ref.dtype)


def _rwkv_post(yf, yb, bonus, gate, lnx_g, lnx_b):
    t = yf.shape[0]
    tm = TOKEN_TILE
    tile = pl.BlockSpec((tm, B_WIDTH), lambda i: (i, 0))
    vec = pl.BlockSpec((1, B_WIDTH), lambda i: (0, 0))
    return pl.pallas_call(
        _rwkv_post_kernel,
        grid=(t // tm,),
        in_specs=[tile, tile, tile, tile, vec, vec],
        out_specs=tile,
        out_shape=jax.ShapeDtypeStruct((t, B_WIDTH), BF16),
        compiler_params=_cparams("parallel"),
        name="rwkv_post",
    )(yf, yb, bonus, gate, lnx_g, lnx_b)


def _hgrn_block(qraw, fraw, v, lb, st_ref, o_ref, reverse):
    n = qraw.shape[0]
    sub = HG_SUB
    fg = lb + (1.0 - lb) * _sigmoid(fraw)
    logf = jnp.log(fg)
    kg = 1.0 - fg
    q = _silu(qraw)
    ri, ci = _iota((n, n), 0), _iota((n, n), 1)
    same = (ri // sub) == (ci // sub)
    upto = same & ((ci >= ri) if reverse else (ci <= ri))
    bcum = _exact_left(upto.astype(BF16), logf, 3)
    half = 8
    hi = _iota((half, 1), 0)
    order = range(n // sub - 1, -1, -1) if reverse else range(n // sub)
    out_rows = [None] * (n // sub)
    for sc in order:
        last = sc * sub if reverse else (sc + 1) * sub - 1
        outs = []
        for h in range(C_HEADS):
            cols = slice(h * C_KDIM, (h + 1) * C_KDIM)
            rows = slice(sc * sub, (sc + 1) * sub)
            qs, ks, vs, bs = q[rows, cols], kg[rows, cols], v[rows, cols], bcum[rows, cols]
            b_end = bcum[last:last + 1, cols]
            slabs, plan = [], []
            for j in range(sub):
                for hf in range(sub // half):
                    rs = slice(hf * half, (hf + 1) * half)
                    diff = bs[rs] - bs[j:j + 1, :]
                    if hf == j // half:
                        live = (hi + hf * half <= j) if reverse else (hi + hf * half >= j)
                        diff = jnp.where(live, diff, -1e30)
                    elif (hf > j // half) == reverse:
                        continue
                    slabs.append(jnp.sum(qs[rs] * jnp.exp(diff) * ks[j:j + 1, :], axis=-1, keepdims=True))
                    plan.append((j, hf))
            o = _dot_nt((qs * jnp.exp(bs)).astype(BF16), st_ref[h].astype(BF16))
            o_half = [o[hf * half:(hf + 1) * half] for hf in range(sub // half)]
            for att, (j, hf) in zip(slabs, plan):
                o_half[hf] = o_half[hf] + att * vs[j:j + 1, :]
            outs.append(jnp.concatenate(o_half, axis=0))
            ke = (ks * jnp.exp(b_end - bs)).astype(BF16)
            st_ref[h] = st_ref[h] * jnp.exp(b_end) + _dot_tn(vs.astype(BF16), ke)
            if h % (C_HEADS // 2) == C_HEADS // 2 - 1:
                yield
        out_rows[sc] = jnp.concatenate(outs, axis=-1)
    o_ref[...] = jnp.concatenate(out_rows, axis=0)


def _mix_scan_kernel(tab_ref, atf_ref, rtf_ref, btf_ref, ktf_ref, vf_ref, glf_ref,
                     atb_ref, rtb_ref, btb_ref, ktb_ref, vb_ref, glb_ref, rw0_ref,
                     qf_ref, ff_ref, if_ref, qb_ref, fb_ref, ib_ref, lb_ref, hg0_ref,
                     yf_ref, yb_ref, rwfin_ref, of_ref, ob_ref, hgfin_ref, s_ref, st_ref):
    step = pl.program_id(0)
    hd = B_HDIM

    @pl.when(tab_ref[3, step] == 1)
    def _():
        z = jnp.zeros((hd, hd), F32)
        for d in range(2):
            for p in range(B_HEADS // 2):
                s_ref[d, p] = jnp.concatenate(
                    [jnp.concatenate([rw0_ref[d, 2 * p], z], axis=1),
                     jnp.concatenate([z, rw0_ref[d, 2 * p + 1]], axis=1)], axis=0)
            for h in range(C_HEADS):
                st_ref[d, h] = hg0_ref[d, h].T

    pending = [
        _rwkv_chunks(
            [(atf_ref[...], rtf_ref[...], btf_ref[...], ktf_ref[...], vf_ref[...], glf_ref[...], False),
             (atb_ref[...], rtb_ref[...], btb_ref[...], ktb_ref[...], vb_ref[...], glb_ref[...], True)],
            s_ref, (yf_ref, yb_ref)),
        _hgrn_block(qf_ref[...], ff_ref[...], if_ref[...], lb_ref[0:1, :], st_ref.at[0], of_ref, False),
        _hgrn_block(qb_ref[...], fb_ref[...], ib_ref[...], lb_ref[1:2, :], st_ref.at[1], ob_ref, True),
    ]
    while pending:
        for gen in list(pending):
            if next(gen, pending) is pending:
                pending.remove(gen)

    @pl.when(tab_ref[4, step] == 1)
    def _():
        for d in range(2):
            for p in range(B_HEADS // 2):
                s = s_ref[d, p]
                rwfin_ref[d, 2 * p] = s[:hd, :hd]
                rwfin_ref[d, 2 * p + 1] = s[hd:, hd:]
            for h in range(C_HEADS):
                hgfin_ref[d, h] = st_ref[d, h].T


def _mix_scan(tok, at, rt, bt, kt, v, gl, rw0, zc, lb, hg0):
    cl = RW_CHUNK
    assert HG_BLOCK == cl
    tab = _scan_table(tok, cl)
    fwd = lambda col: (lambda s, t: (t[0, s], col))
    bwd = lambda col: (lambda s, t: (t[1, s], col))
    fwd2 = lambda s, t: (0, t[0, s], 0)
    bwd2 = lambda s, t: (1, t[1, s], 0)
    blk = lambda im: pl.BlockSpec((cl, B_WIDTH), im)
    blk2 = lambda im: pl.BlockSpec((None, cl, B_WIDTH), im)
    glf = pl.BlockSpec((None, None, 1, B_WIDTH), lambda s, t: (0, t[0, s], 0, 0))
    glb = pl.BlockSpec((None, None, 1, B_WIDTH), lambda s, t: (1, t[1, s], 0, 0))
    rw_st = pl.BlockSpec((None, 2, B_HEADS, B_HDIM, B_HDIM), lambda s, t: (t[2, s], 0, 0, 0, 0))
    hg_st = pl.BlockSpec((None, 2, C_HEADS, C_KDIM, C_KDIM), lambda s, t: (t[2, s], 0, 0, 0, 0))
    cblk = lambda im: pl.BlockSpec((cl, C_WIDTH), im)
    yshape = jax.ShapeDtypeStruct((tok.total, B_WIDTH), F32)
    oshape = jax.ShapeDtypeStruct((tok.total, C_WIDTH), F32)
    return pl.pallas_call(
        _mix_scan_kernel,
        grid_spec=pltpu.PrefetchScalarGridSpec(
            num_scalar_prefetch=1,
            grid=(tab.shape[1],),
            in_specs=[blk2(fwd2), blk2(fwd2), blk2(fwd2), blk2(fwd2), blk(fwd(0)), glf,
                      blk2(bwd2), blk2(bwd2), blk2(bwd2), blk2(bwd2), blk(bwd(0)), glb, rw_st,
                      cblk(fwd(0)), cblk(fwd(1)), cblk(fwd(3)), cblk(bwd(0)), cblk(bwd(2)), cblk(bwd(3)),
                      pl.BlockSpec((2, C_WIDTH), lambda s, t: (0, 0)), hg_st],
            out_specs=[blk(fwd(0)), blk(bwd(0)), rw_st, cblk(fwd(0)), cblk(bwd(0)), hg_st],
            scratch_shapes=[pltpu.VMEM((2, B_HEADS // 2, LANES, LANES), F32),
                            pltpu.VMEM((2, C_HEADS, C_KDIM, C_KDIM), F32)]),
        out_shape=[yshape, yshape, jax.ShapeDtypeStruct(rw0.shape, F32),
                   oshape, oshape, jax.ShapeDtypeStruct(hg0.shape, F32)],
        compiler_params=_cparams("arbitrary"),
        name="rwkv_hgrn_scan",
    )(tab, at, rt, bt, kt, v, gl, at, rt, bt, kt, v, gl, rw0, zc, zc, zc, zc, zc, zc, lb, hg0)


def _hgrn_post_kernel(of_ref, ob_ref, g_ref, gn_ref, o_ref):
    o = of_ref[...] + ob_ref[...]
    gate = _silu(g_ref[...])
    outs = []
    for h in range(C_HEADS):
        cols = slice(h * C_KDIM, (h + 1) * C_KDIM)
        oh = o[:, cols]
        ms = jnp.mean(oh * oh, axis=-1, keepdims=True)
        outs.append(oh * lax.rsqrt(ms + EPS) * gn_ref[...] * gate[:, cols])
    o_ref[...] = jnp.concatenate(outs, axis=-1).astype(o_ref.dtype)


def _hgrn_post(of, ob, zc, gn):
    t = of.shape[0]
    tm = TOKEN_TILE
    tile = pl.BlockSpec((tm, C_WIDTH), lambda i: (i, 0))
    return pl.pallas_call(
        _hgrn_post_kernel,
        grid=(t // tm,),
        in_specs=[tile, tile, pl.BlockSpec((tm, C_WIDTH), lambda i: (i, 4)),
                  pl.BlockSpec((1, C_KDIM), lambda i: (0, 0))],
        out_specs=tile,
        out_shape=jax.ShapeDtypeStruct((t, C_WIDTH), BF16),
        compiler_params=_cparams("parallel"),
        name="hgrn2_post",
    )(of, ob, zc, gn)


def _mix_out_kernel(oa_ref, ob_ref, oc_ref, wa_ref, wb_ref, wc_ref, x_ref, gt_ref,
                    gn_ref, scn_ref, shn_ref, xo_ref, ho_ref):
    y = _dot(oa_ref[...], wa_ref[...]) + _dot(ob_ref[...], wb_ref[...]) + _dot(oc_ref[...], wc_ref[...])
    xn = x_ref[...] + gt_ref[...] * y
    xo_ref[...] = xn
    ho_ref[...] = _norm_mod(xn, gn_ref[...], scn_ref[...], shn_ref[...]).astype(ho_ref.dtype)


def _mix_out(tok, oa, ob, oc, wa, wb, wc, x, modr, layer, next_g):
    tm = TOKEN_TILE
    row = pl.BlockSpec((tm, D_MODEL), lambda i: (i, 0))
    full = lambda a: pl.BlockSpec(a.shape, lambda i: (0, 0))
    return pl.pallas_call(
        _mix_out_kernel,
        grid=(tok.total // tm,),
        in_specs=[pl.BlockSpec((tm, A_WIDTH), lambda i: (i, 0)),
                  pl.BlockSpec((tm, B_WIDTH), lambda i: (i, 0)),
                  pl.BlockSpec((tm, C_WIDTH), lambda i: (i, 0)),
                  full(wa), full(wb), full(wc), row,
                  _mod_spec(tok, tm, layer, 5, 1),
                  pl.BlockSpec((1, D_MODEL), lambda i: (0, 0)),
                  _mod_spec(tok, tm, layer, 7, 1), _mod_spec(tok, tm, layer, 6, 1)],
        out_specs=[row, row],
        out_shape=[jax.ShapeDtypeStruct((tok.total, D_MODEL), F32),
                   jax.ShapeDtypeStruct((tok.total, D_MODEL), BF16)],
        compiler_params=_cparams("parallel"),
        name="mix_out_proj",
    )(oa, ob, oc, wa, wb, wc, x, modr, next_g, modr, modr)


def kernel(x_prompt, x_sample, state_rwkv, state_hgrn, c, c_ctx, norm_g, w_mod, b_mod, ffn_w1, ffn_w3,
           ffn_w2, w_in, w_out, mlp_norm_g, mlp_ws, mlp_bs, rwkv_mu, rwkv_w0, rwkv_w2, rwkv_a0, rwkv_a2,
           rwkv_g2, rwkv_kk, rwkv_ka, rwkv_rk, rwkv_lnx_g, rwkv_lnx_b, hgrn_lb, hgrn_gn, final_g):
    n_ctx, ctx_len, _ = x_prompt.shape
    n_lat, lat_len, _ = x_sample.shape
    depth = w_mod.shape[0]
    tok = _Tokens(n_ctx, ctx_len, n_lat, lat_len)
    assert n_lat + 1 <= 8 and ctx_len % PREP_TILE == 0 and lat_len % TOKEN_TILE == 0

    cond8 = jnp.zeros((8, D_MODEL), F32).at[0].set(c_ctx).at[1:1 + n_lat].set(c)
    mod = _mod_rows(cond8, w_mod, b_mod)
    modr = mod.reshape(depth, 8, N_MOD, D_MODEL).transpose(0, 2, 1, 3)[:, :, :, None, :]

    sm = jax.nn.softmax(hgrn_lb.astype(F32), axis=0)
    lower = jnp.cumsum(sm, axis=0) - sm[0]

    x = jnp.concatenate([x_prompt.reshape(-1, D_MODEL), x_sample.reshape(-1, D_MODEL)], axis=0)
    zero_sc = jnp.zeros((1, D_MODEL), F32)
    zero_spec = pl.BlockSpec((1, D_MODEL), lambda i, j: (0, 0))
    tm = TOKEN_TILE

    h = _normmod(tok, x, norm_g[0, 0][None], modr, 0, 1, 0)
    bw = lambda a: a.astype(BF16)
    w1b, w3b, w2b = bw(ffn_w1), bw(ffn_w3), bw(ffn_w2)
    rw_states, hg_states = [], []
    for l in range(depth):
        x, h = _ffn(tok, h, x, modr, l, 0, 2, w1b, w3b, w2b,
                    norm_g[l, 1][None], (modr, _mod_spec(tok, tm, l, 4, 2)),
                    (modr, _mod_spec(tok, tm, l, 3, 2)), BF16)

        za = _project(h, bw(w_in[l, :, :A_COLS]), 512)
        zb = _project(h, bw(w_in[l, :, A_COLS:A_COLS + B_COLS]), 896)
        zc = _project(h, bw(w_in[l, :, A_COLS + B_COLS:]), 768)

        oa = _gmlp(za, mlp_norm_g[l], mlp_ws[l], mlp_bs[l])

        at, rt, bt, kt, v, gl, gate, bonus = _rwkv_prep(
            tok, zb, rwkv_mu[l][None], rwkv_w0[l], rwkv_w2[l], rwkv_a0[l], rwkv_a2[l], rwkv_g2[l],
            rwkv_kk[l][None], rwkv_ka[l][None], rwkv_rk[l].reshape(1, B_WIDTH))
        rw0 = jnp.concatenate([jnp.zeros((n_ctx, 2, B_HEADS, B_HDIM, B_HDIM), F32), state_rwkv[:, l]], 0)
        hg0 = jnp.concatenate([jnp.zeros((n_ctx, 2, C_HEADS, C_KDIM, C_KDIM), F32), state_hgrn[:, l]], 0)
        yf, yb, s_rw, of, obw, s_hg = _mix_scan(tok, at, rt, bt, kt, v, gl, rw0, zc, lower[l], hg0)
        ob = _rwkv_post(yf, yb, bonus, gate,
                        rwkv_lnx_g[l].reshape(1, B_WIDTH), rwkv_lnx_b[l].reshape(1, B_WIDTH))
        oc = _hgrn_post(of, obw, zc, hgrn_gn[l][None])
        rw_states.append(s_rw[:n_ctx])
        hg_states.append(s_hg[:n_ctx])

        w_out_l = bw(w_out[l])
        x, h = _mix_out(tok, oa, ob, oc, w_out_l[:A_WIDTH], w_out_l[A_WIDTH:A_WIDTH + B_WIDTH],
                        w_out_l[A_WIDTH + B_WIDTH:], x, modr, l, norm_g[l, 2][None])

        if l + 1 < depth:
            x, h = _ffn(tok, h, x, modr, l, 1, 8, w1b, w3b, w2b,
                        norm_g[l + 1, 0][None], (modr, _mod_spec(tok, tm, l + 1, 1, 2)),
                        (modr, _mod_spec(tok, tm, l + 1, 0, 2)), BF16)
        else:
            x, h = _ffn(tok, h, x, modr, l, 1, 8, w1b, w3b, w2b,
                        final_g[None], (zero_sc, zero_spec), (zero_sc, zero_spec), F32)

    y_prompt = h[:tok.ctx_tokens].reshape(x_prompt.shape)
    y_sample = h[tok.ctx_tokens:].reshape(x_sample.shape)
    return (y_prompt, y_sample, jnp.stack(rw_states, axis=1), jnp.stack(hg_states, axis=1))
```

```python
import functools

import jax
import jax.numpy as jnp
from jax import lax
from jax.experimental import pallas as pl
from jax.experimental.pallas import tpu as pltpu

F32 = jnp.float32
BF16 = jnp.bfloat16

D_MODEL = 2048
D_FF = 5632
N_MOD = 9
EPS = 1e-6
LNX_EPS = 64e-5

CHUNK_MLP = 128
A_WIDTH = 512
A_GROUPS = 4
A_GDIM = 128
B_HDIM = 64
B_WIDTH = 768
B_HEADS = 12
LORA = 64
GATE_LORA = 128
B_COLS = 3 * B_WIDTH + 4 * LORA + GATE_LORA
C_KDIM = 128
C_WIDTH = 768
C_HEADS = 6
C_COLS = 5 * C_WIDTH
A_COLS = 2 * A_WIDTH

LANES = 128
VMEM_LIMIT = 56 * 1024 * 1024

TOKEN_TILE = 512
FF_TILE = 512
PREP_TILE = 256
RW_CHUNK = 64
HG_BLOCK = 64
HG_SUB = 16


def _cparams(*sem):
    return pltpu.CompilerParams(dimension_semantics=sem, vmem_limit_bytes=VMEM_LIMIT)


def _dot(a, b):
    return jnp.dot(a, b, preferred_element_type=F32)


def _dot_nt(a, b):
    return lax.dot_general(a, b, (((1,), (1,)), ((), ())), preferred_element_type=F32)


def _dot_tn(a, b):
    return lax.dot_general(a, b, (((0,), (0,)), ((), ())), preferred_element_type=F32)


def _split(x, parts):
    out = []
    for _ in range(parts - 1):
        hi = x.astype(BF16)
        out.append(hi)
        x = x - hi.astype(F32)
    out.append(x.astype(BF16))
    return out


def _exact_left(m, x, parts):
    acc = None
    for p in _split(x, parts):
        t = _dot(m, p)
        acc = t if acc is None else acc + t
    return acc


def _exact_right(x, m, parts):
    acc = None
    for p in _split(x, parts):
        t = _dot(p, m)
        acc = t if acc is None else acc + t
    return acc


def _iota(shape, dim):
    return lax.broadcasted_iota(jnp.int32, shape, dim)


def _sigmoid(x):
    return jax.nn.sigmoid(x)


def _silu(x):
    return x * jax.nn.sigmoid(x)


def _norm_mod(x, g, sc, sh):
    ms = jnp.mean(x * x, axis=-1, keepdims=True)
    return (x * lax.rsqrt(ms + EPS) * g) * (1.0 + sc) + sh


def _mod_kernel(cond_ref, w_ref, b_ref, o_ref):
    s = _silu(cond_ref[...]).astype(BF16)
    o_ref[...] = _dot(s, w_ref[...].astype(BF16)) + b_ref[...]


def _mod_rows(cond8, w_mod, b_mod):
    depth, _, n = w_mod.shape
    tn = 1024
    return pl.pallas_call(
        _mod_kernel,
        grid=(depth, n // tn),
        in_specs=[
            pl.BlockSpec((8, D_MODEL), lambda l, j: (0, 0)),
            pl.BlockSpec((None, D_MODEL, tn), lambda l, j: (l, 0, j)),
            pl.BlockSpec((None, 1, tn), lambda l, j: (l, 0, j)),
        ],
        out_specs=pl.BlockSpec((None, 8, tn), lambda l, j: (l, 0, j)),
        out_shape=jax.ShapeDtypeStruct((depth, 8, n), F32),
        compiler_params=_cparams("parallel", "parallel"),
        name="adaln_rows",
    )(cond8, w_mod, b_mod.reshape(depth, 1, n))


class _Tokens:
    def __init__(self, n_ctx, ctx_len, n_lat, lat_len):
        self.n_ctx, self.ctx_len, self.n_lat, self.lat_len = n_ctx, ctx_len, n_lat, lat_len
        self.ctx_tokens = n_ctx * ctx_len
        self.total = self.ctx_tokens + n_lat * lat_len

    def mod_row(self, tile, i):
        nct = self.ctx_tokens // tile
        per_lat = self.lat_len // tile
        return jnp.where(i < nct, 0, 1 + (i - nct) // per_lat)


def _mod_spec(tok, tile, layer, which, grid_rank):
    def imap(*idx):
        return (layer, which, tok.mod_row(tile, idx[0]), 0, 0)
    del grid_rank
    return pl.BlockSpec((None, None, None, 1, D_MODEL), imap)


def _normmod_kernel(x_ref, g_ref, sc_ref, sh_ref, h_ref):
    h_ref[...] = _norm_mod(x_ref[...], g_ref[...], sc_ref[...], sh_ref[...]).astype(h_ref.dtype)


def _normmod(tok, x, norm_g_row, modr, layer, sc_i, sh_i):
    tm = TOKEN_TILE
    return pl.pallas_call(
        _normmod_kernel,
        grid=(tok.total // tm,),
        in_specs=[
            pl.BlockSpec((tm, D_MODEL), lambda i: (i, 0)),
            pl.BlockSpec((1, D_MODEL), lambda i: (0, 0)),
            _mod_spec(tok, tm, layer, sc_i, 1),
            _mod_spec(tok, tm, layer, sh_i, 1),
        ],
        out_specs=pl.BlockSpec((tm, D_MODEL), lambda i: (i, 0)),
        out_shape=jax.ShapeDtypeStruct((tok.total, D_MODEL), BF16),
        compiler_params=_cparams("parallel"),
        name="first_norm",
    )(x, norm_g_row, modr, modr)


def _ffn_kernel(h_ref, x_ref, gt_ref, w1_ref, w3_ref, w2_ref, gn_ref, scn_ref, shn_ref,
                out_a_ref, out_b_ref, acc_ref, *, n_ff, n_ctx_tiles):
    i = pl.program_id(0)
    j = pl.program_id(1)

    @pl.when(j == 0)
    def _():
        acc_ref[...] = jnp.zeros_like(acc_ref)

    h = h_ref[...]
    a = _dot(h, w1_ref[...])
    b = _dot(h, w3_ref[...])
    p = (_silu(a) * b).astype(BF16)
    acc_ref[...] += _dot(p, w2_ref[...])

    @pl.when(j == n_ff - 1)
    def _():
        xn = x_ref[...] + 0.5 * gt_ref[...] * acc_ref[...]
        hn = _norm_mod(xn, gn_ref[...], scn_ref[...], shn_ref[...])
        if n_ctx_tiles is None:
            out_a_ref[...] = xn
            out_b_ref[...] = hn.astype(out_b_ref.dtype)
        else:
            @pl.when(i < n_ctx_tiles)
            def _():
                out_a_ref[...] = hn

            @pl.when(i >= n_ctx_tiles)
            def _():
                out_b_ref[...] = hn


def _ffn(tok, h, x, modr, layer, which, gate_i, w1, w3, w2, next_g, next_sc, next_sh, final):
    tm, tf = TOKEN_TILE, FF_TILE
    n_ff = D_FF // tf
    nct = tok.ctx_tokens // tm
    row = pl.BlockSpec((tm, D_MODEL), lambda i, j: (i, 0))
    if final:
        out_specs = [pl.BlockSpec((tm, D_MODEL), lambda i, j: (jnp.minimum(i, nct - 1), 0)),
                     pl.BlockSpec((tm, D_MODEL), lambda i, j: (jnp.maximum(i - nct, 0), 0))]
        out_shape = [jax.ShapeDtypeStruct((tok.ctx_tokens, D_MODEL), F32),
                     jax.ShapeDtypeStruct((tok.total - tok.ctx_tokens, D_MODEL), F32)]
    else:
        out_specs = [row, row]
        out_shape = [jax.ShapeDtypeStruct((tok.total, D_MODEL), F32),
                     jax.ShapeDtypeStruct((tok.total, D_MODEL), BF16)]
    return pl.pallas_call(
        functools.partial(_ffn_kernel, n_ff=n_ff, n_ctx_tiles=nct if final else None),
        grid=(tok.total // tm, n_ff),
        in_specs=[
            row, row,
            _mod_spec(tok, tm, layer, gate_i, 2),
            pl.BlockSpec((None, None, D_MODEL, tf), lambda i, j: (layer, which, 0, j)),
            pl.BlockSpec((None, None, D_MODEL, tf), lambda i, j: (layer, which, 0, j)),
            pl.BlockSpec((None, None, tf, D_MODEL), lambda i, j: (layer, which, j, 0)),
            pl.BlockSpec((1, D_MODEL), lambda i, j: (0, 0)),
            next_sc[1], next_sh[1],
        ],
        out_specs=out_specs,
        out_shape=out_shape,
        scratch_shapes=[pltpu.VMEM((tm, D_MODEL), F32)],
        compiler_params=_cparams("arbitrary", "arbitrary"),
        name="swiglu_half_step",
    )(h, x, modr, w1, w3, w2, next_g, next_sc[0], next_sh[0])


def _project_kernel(h_ref, w_ref, o_ref, wb_ref):
    @pl.when(pl.program_id(1) == 0)
    def _():
        wb_ref[...] = w_ref[...].astype(BF16)

    o_ref[...] = _dot(h_ref[...], wb_ref[...])


def _project(h, w_in, layer, col0, n, tn):
    t, k = h.shape
    tm = TOKEN_TILE
    return pl.pallas_call(
        _project_kernel,
        grid=(n // tn, t // tm),
        in_specs=[pl.BlockSpec((tm, k), lambda j, i: (i, 0)),
                  pl.BlockSpec((pl.Squeezed(), pl.Element(k), pl.Element(tn)),
                               lambda j, i: (layer, 0, pl.multiple_of(col0 + j * tn, LANES)))],
        out_specs=pl.BlockSpec((tm, tn), lambda j, i: (i, j)),
        out_shape=jax.ShapeDtypeStruct((t, n), F32),
        scratch_shapes=[pltpu.VMEM((k, tn), BF16)],
        compiler_params=_cparams("parallel", "arbitrary"),
        name="mix_in_proj",
    )(h, w_in)


def _gmlp_kernel(za_ref, ng_ref, ws_ref, bs_ref, o_ref):
    z = jax.nn.gelu(za_ref[...])
    outs = []
    for g in range(A_GROUPS):
        u = z[:, g * A_GDIM:(g + 1) * A_GDIM]
        v = z[:, A_WIDTH + g * A_GDIM:A_WIDTH + (g + 1) * A_GDIM]
        ms = jnp.mean(v * v, axis=-1, keepdims=True)
        v = v * lax.rsqrt(ms + EPS) * ng_ref[g:g + 1, :]
        mixed = _dot(ws_ref[g], v.astype(BF16)) + bs_ref[g]
        outs.append(u * mixed)
    o_ref[...] = jnp.concatenate(outs, axis=-1).astype(o_ref.dtype)


def _gmlp(za, ng, ws, bs):
    t = za.shape[0]
    return pl.pallas_call(
        _gmlp_kernel,
        grid=(t // CHUNK_MLP,),
        in_specs=[
            pl.BlockSpec((CHUNK_MLP, A_COLS), lambda i: (i, 0)),
            pl.BlockSpec((A_GROUPS, A_GDIM), lambda i: (0, 0)),
            pl.BlockSpec((A_GROUPS, CHUNK_MLP, CHUNK_MLP), lambda i: (0, 0, 0)),
            pl.BlockSpec((A_GROUPS, CHUNK_MLP, 1), lambda i: (0, 0, 0)),
        ],
        out_specs=pl.BlockSpec((CHUNK_MLP, A_WIDTH), lambda i: (i, 0)),
        out_shape=jax.ShapeDtypeStruct((t, A_WIDTH), BF16),
        compiler_params=_cparams("parallel"),
        name="gmlp_chunk_mix",
    )(za, ng, ws.astype(BF16), bs[:, :, None])


def _head_ones():
    r = _iota((LANES, LANES), 0) // B_HDIM
    c = _iota((LANES, LANES), 1) // B_HDIM
    return (r == c).astype(BF16)


def _head_sum(x, parts=2):
    ones = _head_ones()
    outs = []
    for s in range(x.shape[-1] // LANES):
        outs.append(_exact_right(x[:, s * LANES:(s + 1) * LANES], ones, parts))
    return jnp.concatenate(outs, axis=-1)


def _rwkv_prep_kernel(z_ref, zp_ref, zn_ref, mu_ref, w0_ref, w2_ref, a0_ref, a2_ref, g2_ref,
                      kkw_ref, kaw_ref, rk_ref,
                      at_ref, rt_ref, bt_ref, kt_ref, v_ref, gl_ref, gate_ref, bonus_ref,
                      *, n_ctx_tiles, ctx_tiles_per_seq, lat_tiles_per_seq):
    i = pl.program_id(0)
    pos = jnp.where(i < n_ctx_tiles, i % ctx_tiles_per_seq, (i - n_ctx_tiles) % lat_tiles_per_seq)
    last = jnp.where(i < n_ctx_tiles, ctx_tiles_per_seq - 1, lat_tiles_per_seq - 1)
    z = z_ref[...]
    t = z.shape[0]
    row = _iota((t, 1), 0)
    halo_prev = jnp.where(pos != 0, zp_ref[7:8, :], 0.0)
    halo_next = jnp.where(pos != last, zn_ref[0:1, :], 0.0)
    prev = jnp.where(row == 0, halo_prev, pltpu.roll(z, 1, 0))
    nxt = jnp.where(row == t - 1, halo_next, pltpu.roll(z, t - 1, 0))
    z = z + (0.5 * (prev + nxt) - z) * mu_ref[...]

    w = B_WIDTH
    r, k, v = z[:, 0:w], z[:, w:2 * w], z[:, 2 * w:3 * w]
    wd = z[:, 3 * w:3 * w + 2 * LORA]
    ad = z[:, 3 * w + 2 * LORA:3 * w + 4 * LORA]
    gd = z[:, 3 * w + 4 * LORA:]
    twd = jnp.tanh(wd).astype(BF16)
    adb = ad.astype(BF16)

    kk = k * kkw_ref[...]
    nrm = jnp.sqrt(_head_sum(kk * kk))
    kk = kk / jnp.maximum(nrm, 1e-12)

    v_ref[...] = v.astype(BF16)
    gate_ref[...] = _dot(_sigmoid(gd).astype(BF16), g2_ref[...])
    cl = RW_CHUNK
    ri, ci = _iota((t, t), 0), _iota((t, t), 1)
    same_chunk = (ri // cl) == (ci // cl)
    ksum = jnp.zeros_like(k)
    for d in range(2):
        w_raw = w0_ref[d:d + 1, :] + _dot(twd[:, d * LORA:(d + 1) * LORA], w2_ref[d])
        lw = -jnp.exp(-jax.nn.softplus(-w_raw) - 0.5)
        a = _sigmoid(a0_ref[d:d + 1, :] + _dot(adb[:, d * LORA:(d + 1) * LORA], a2_ref[d]))
        kd = k * (1.0 + (a - 1.0) * kaw_ref[...])
        ksum = ksum + kd
        upto = same_chunk & ((ci >= ri) if d == 1 else (ci <= ri))
        g = _exact_left(upto.astype(BF16), lw, 3)
        ieg = jnp.exp(-g)
        at_ref[d] = (-kk * jnp.exp(g - lw)).astype(BF16)
        rt_ref[d] = (r * jnp.exp(g)).astype(BF16)
        bt_ref[d] = (kk * a * ieg).astype(BF16)
        kt_ref[d] = (kd * ieg).astype(BF16)
        for c in range(t // cl):
            end = c * cl if d == 1 else (c + 1) * cl - 1
            gl_ref[d, c] = jnp.exp(g[end:end + 1, :])
    bonus_ref[...] = _head_sum(r * ksum * rk_ref[...]) * v


def _rwkv_prep(tok, zb, mu, w0, w2, a0, a2, g2, kkw, kaw, rk):
    tp = PREP_TILE
    n_tiles = tok.total // tp
    rows8 = tok.total // 8
    per = tp // 8
    full = lambda shape: pl.BlockSpec(shape, lambda i: (0,) * len(shape))
    cpt = tp // RW_CHUNK
    tile = pl.BlockSpec((tp, B_WIDTH), lambda i: (i, 0))
    tile2 = pl.BlockSpec((2, tp, B_WIDTH), lambda i: (0, i, 0))
    gl_spec = pl.BlockSpec((2, cpt, 1, B_WIDTH), lambda i: (0, i, 0, 0))
    o1 = jax.ShapeDtypeStruct((tok.total, B_WIDTH), F32)
    o1b = jax.ShapeDtypeStruct((tok.total, B_WIDTH), BF16)
    o2b = jax.ShapeDtypeStruct((2, tok.total, B_WIDTH), BF16)
    ogl = jax.ShapeDtypeStruct((2, tok.total // RW_CHUNK, 1, B_WIDTH), F32)
    kern = functools.partial(_rwkv_prep_kernel, n_ctx_tiles=tok.ctx_tokens // tp,
                             ctx_tiles_per_seq=tok.ctx_len // tp, lat_tiles_per_seq=tok.lat_len // tp)
    return pl.pallas_call(
        kern,
        grid=(n_tiles,),
        in_specs=[
            pl.BlockSpec((tp, B_COLS), lambda i: (i, 0)),
            pl.BlockSpec((8, B_COLS), lambda i: (jnp.maximum(i * per - 1, 0), 0)),
            pl.BlockSpec((8, B_COLS), lambda i: (jnp.minimum((i + 1) * per, rows8 - 1), 0)),
            full((1, B_COLS)), full((2, B_WIDTH)), full((2, LORA, B_WIDTH)), full((2, B_WIDTH)),
            full((2, LORA, B_WIDTH)), full((GATE_LORA, B_WIDTH)),
            full((1, B_WIDTH)), full((1, B_WIDTH)), full((1, B_WIDTH)),
        ],
        out_specs=[tile2, tile2, tile2, tile2, tile, gl_spec, tile, tile],
        out_shape=[o2b, o2b, o2b, o2b, o1b, ogl, o1, o1],
        compiler_params=_cparams("parallel"),
        name="rwkv_prep",
    )(zb, zb, zb, mu, w0, w2.astype(BF16), a0, a2.astype(BF16), g2.astype(BF16), kkw, kaw, rk)


def _pair_blockdiag(x):
    lane = _iota(x.shape, 1)
    zero = jnp.zeros_like(x)
    return jnp.concatenate([jnp.where(lane < B_HDIM, x, zero), jnp.where(lane >= B_HDIM, x, zero)], axis=0)


def _rwkv_chunks(streams, s_ref, y_refs):
    c = streams[0][0].shape[0]
    n_pairs = B_HEADS // 2
    rr, cc = _iota((2 * c, 2 * c), 0), _iota((2 * c, 2 * c), 1)
    same = (rr // c) == (cc // c)
    tt, ss = rr % c, cc % c
    eye = (rr == cc).astype(F32)
    pair_masks = []
    n = 1
    while n < c:
        pair_masks.append(same & ((tt // (2 * n)) == (ss // (2 * n))) & ((tt // n) != (ss // n)))
        n *= 2

    units = []
    for d, (a_t, r_t, b_t, k_t, vb, gl, reverse) in enumerate(streams):
        before = same & ((ss > tt) if reverse else (ss < tt))
        upto = same & ((ss >= tt) if reverse else (ss <= tt))
        for p in range(n_pairs):
            sl = slice(p * LANES, (p + 1) * LANES)
            units.append(dict(d=d, p=p, before=before, upto=upto, gl=gl[:, sl],
                              ar=jnp.concatenate([_pair_blockdiag(a_t[:, sl]), _pair_blockdiag(r_t[:, sl])], 0),
                              bk=jnp.concatenate([_pair_blockdiag(b_t[:, sl]), _pair_blockdiag(k_t[:, sl])], 0),
                              v=_pair_blockdiag(vb[:, sl])))
    m = 2 * c
    for u in units:
        u["gram"] = _dot_nt(u["ar"], u["bk"])
        u["s0"] = s_ref[u["d"], u["p"]]
    yield
    for u in units:
        u["ws"] = _dot_nt(u["ar"], u["s0"].astype(BF16))
        g = u["gram"]
        u["lmat"] = jnp.where(u["before"], g[:m, :m], 0.0)
        aak = jnp.where(u["before"], g[:m, m:], 0.0).astype(BF16)
        u["rbk"] = jnp.concatenate([jnp.where(u["upto"], g[m:, :m], 0.0),
                                    jnp.where(u["upto"], g[m:, m:], 0.0)], axis=1).astype(BF16)
        u["x"] = u["ws"][:m] + _dot(aak, u["v"])
    yield
    for lvl, pm in enumerate(pair_masks):
        for u in units:
            link = jnp.where(pm, u["lmat"], 0.0)
            if lvl == 0:
                u["tinv"] = eye + link
            else:
                u["tb"] = u["tinv"].astype(BF16)
                u["tmp"] = _dot(link.astype(BF16), u["tb"]).astype(BF16)
        if lvl > 0:
            yield
            for u in units:
                u["tinv"] = u["tinv"] + _dot(u["tb"], u["tmp"])
            yield
    for u in units:
        u["uv"] = jnp.concatenate([_dot(u["tinv"].astype(BF16), u["x"].astype(BF16)).astype(BF16), u["v"]], 0)
    yield
    ys = [[None] * n_pairs for _ in streams]
    for u in units:
        y = u["ws"][m:] + _dot(u["rbk"], u["uv"])
        ys[u["d"]][u["p"]] = y[:c] + y[c:]
        s_ref[u["d"], u["p"]] = (u["s0"] + _dot_tn(u["uv"], u["bk"])) * u["gl"]
    for d, row in enumerate(ys):
        y_refs[d][...] = jnp.concatenate(row, axis=-1)


def _scan_table(tok, chunk):
    cols = []
    for s in range(tok.n_ctx + tok.n_lat):
        if s < tok.n_ctx:
            base, nc = s * tok.ctx_len // chunk, tok.ctx_len // chunk
        else:
            base = (tok.ctx_tokens + (s - tok.n_ctx) * tok.lat_len) // chunk
            nc = tok.lat_len // chunk
        for c in range(nc):
            cols.append((base + c, base + nc - 1 - c, int(s < tok.n_ctx), int(c == 0), int(c == nc - 1),
                         max(s - tok.n_ctx, 0), min(s, tok.n_ctx - 1)))
    return jnp.asarray(list(zip(*cols)), dtype=jnp.int32)


def _rwkv_post_kernel(yf_ref, yb_ref, bonus_ref, gate_ref, g_ref, b_ref, o_ref):
    y = yf_ref[...] + yb_ref[...]
    mean = _head_sum(y, 3) * (1.0 / B_HDIM)
    yc = y - mean
    var = _head_sum(yc * yc) * (1.0 / B_HDIM)
    y = yc * lax.rsqrt(var + LNX_EPS) * g_ref[...] + b_ref[...] + bonus_ref[...]
    o_ref[...] = (y * gate_ref[...]).astype(o_ref.dtype)


def _rwkv_post(yf, yb, bonus, gate, lnx_g, lnx_b):
    t = yf.shape[0]
    tm = TOKEN_TILE
    tile = pl.BlockSpec((tm, B_WIDTH), lambda i: (i, 0))
    vec = pl.BlockSpec((1, B_WIDTH), lambda i: (0, 0))
    return pl.pallas_call(
        _rwkv_post_kernel,
        grid=(t // tm,),
        in_specs=[tile, tile, tile, tile, vec, vec],
        out_specs=tile,
        out_shape=jax.ShapeDtypeStruct((t, B_WIDTH), BF16),
        compiler_params=_cparams("parallel"),
        name="rwkv_post",
    )(yf, yb, bonus, gate, lnx_g, lnx_b)


def _hgrn_block(qraw, fraw, v, lb, st_ref, o_ref, reverse):
    n = qraw.shape[0]
    sub = HG_SUB
    fg = lb + (1.0 - lb) * _sigmoid(fraw)
    logf = jnp.log(fg)
    kg = 1.0 - fg
    q = _silu(qraw)
    ri, ci = _iota((n, n), 0), _iota((n, n), 1)
    same = (ri // sub) == (ci // sub)
    upto = same & ((ci >= ri) if reverse else (ci <= ri))
    bcum = _exact_left(upto.astype(BF16), logf, 3)
    half = 8
    hi = _iota((half, 1), 0)
    order = range(n // sub - 1, -1, -1) if reverse else range(n // sub)
    out_rows = [None] * (n // sub)
    for sc in order:
        last = sc * sub if reverse else (sc + 1) * sub - 1
        outs = []
        for h in range(C_HEADS):
            cols = slice(h * C_KDIM, (h + 1) * C_KDIM)
            rows = slice(sc * sub, (sc + 1) * sub)
            qs, ks, vs, bs = q[rows, cols], kg[rows, cols], v[rows, cols], bcum[rows, cols]
            b_end = bcum[last:last + 1, cols]
            intra = [jnp.zeros((half, C_KDIM), F32) for _ in range(sub // half)]
            for j in range(sub):
                for hf in range(sub // half):
                    rs = slice(hf * half, (hf + 1) * half)
                    diff = bs[rs] - bs[j:j + 1, :]
                    if hf == j // half:
                        live = (hi + hf * half <= j) if reverse else (hi + hf * half >= j)
                        diff = jnp.where(live, diff, -1e30)
                    elif (hf > j // half) == reverse:
                        continue
                    att = jnp.sum(qs[rs] * jnp.exp(diff) * ks[j:j + 1, :], axis=-1, keepdims=True)
                    intra[hf] = intra[hf] + att * vs[j:j + 1, :]
            o = _dot_nt((qs * jnp.exp(bs)).astype(BF16), st_ref[h].astype(BF16))
            outs.append(o + jnp.concatenate(intra, axis=0))
            ke = (ks * jnp.exp(b_end - bs)).astype(BF16)
            st_ref[h] = st_ref[h] * jnp.exp(b_end) + _dot_tn(vs.astype(BF16), ke)
            if h % (C_HEADS // 2) == C_HEADS // 2 - 1:
                yield
        out_rows[sc] = jnp.concatenate(outs, axis=-1)
    o_ref[...] = jnp.concatenate(out_rows, axis=0)


def _mix_scan_kernel(tab_ref, atf_ref, rtf_ref, btf_ref, ktf_ref, vf_ref, glf_ref,
                     atb_ref, rtb_ref, btb_ref, ktb_ref, vb_ref, glb_ref, rw0_ref,
                     qf_ref, ff_ref, if_ref, qb_ref, fb_ref, ib_ref, lb_ref, hg0_ref,
                     yf_ref, yb_ref, rwfin_ref, of_ref, ob_ref, hgfin_ref, s_ref, st_ref):
    step = pl.program_id(0)
    hd = B_HDIM

    is_ctx = tab_ref[2, step] == 1
    first = tab_ref[3, step] == 1
    last = tab_ref[4, step] == 1

    @pl.when(first & is_ctx)
    def _():
        s_ref[...] = jnp.zeros_like(s_ref)
        st_ref[...] = jnp.zeros_like(st_ref)

    @pl.when(first & jnp.logical_not(is_ctx))
    def _():
        z = jnp.zeros((hd, hd), F32)
        for d in range(2):
            for p in range(B_HEADS // 2):
                s_ref[d, p] = jnp.concatenate(
                    [jnp.concatenate([rw0_ref[d, 2 * p], z], axis=1),
                     jnp.concatenate([z, rw0_ref[d, 2 * p + 1]], axis=1)], axis=0)
            for h in range(C_HEADS):
                st_ref[d, h] = hg0_ref[d, h].T

    pending = [
        _rwkv_chunks(
            [(atf_ref[...], rtf_ref[...], btf_ref[...], ktf_ref[...], vf_ref[...], glf_ref[...], False),
             (atb_ref[...], rtb_ref[...], btb_ref[...], ktb_ref[...], vb_ref[...], glb_ref[...], True)],
            s_ref, (yf_ref, yb_ref)),
        _hgrn_block(qf_ref[...], ff_ref[...], if_ref[...], lb_ref[0:1, :], st_ref.at[0], of_ref, False),
        _hgrn_block(qb_ref[...], fb_ref[...], ib_ref[...], lb_ref[1:2, :], st_ref.at[1], ob_ref, True),
    ]
    while pending:
        for gen in list(pending):
            if next(gen, pending) is pending:
                pending.remove(gen)

    @pl.when(last & is_ctx)
    def _():
        for d in range(2):
            for p in range(B_HEADS // 2):
                s = s_ref[d, p]
                rwfin_ref[d, 2 * p] = s[:hd, :hd]
                rwfin_ref[d, 2 * p + 1] = s[hd:, hd:]
            for h in range(C_HEADS):
                hgfin_ref[d, h] = st_ref[d, h].T


def _mix_scan(tok, layer, at, rt, bt, kt, v, gl, rw0, zc, lb, hg0):
    cl = RW_CHUNK
    assert HG_BLOCK == cl
    tab = _scan_table(tok, cl)
    fwd = lambda col: (lambda s, t: (t[0, s], col))
    bwd = lambda col: (lambda s, t: (t[1, s], col))
    fwd2 = lambda s, t: (0, t[0, s], 0)
    bwd2 = lambda s, t: (1, t[1, s], 0)
    blk = lambda im: pl.BlockSpec((cl, B_WIDTH), im)
    blk2 = lambda im: pl.BlockSpec((None, cl, B_WIDTH), im)
    glf = pl.BlockSpec((None, None, 1, B_WIDTH), lambda s, t: (0, t[0, s], 0, 0))
    glb = pl.BlockSpec((None, None, 1, B_WIDTH), lambda s, t: (1, t[1, s], 0, 0))
    rw_in = pl.BlockSpec((None, None, 2, B_HEADS, B_HDIM, B_HDIM), lambda s, t: (t[5, s], layer, 0, 0, 0, 0))
    hg_in = pl.BlockSpec((None, None, 2, C_HEADS, C_KDIM, C_KDIM), lambda s, t: (t[5, s], layer, 0, 0, 0, 0))
    rw_out = pl.BlockSpec((None, 2, B_HEADS, B_HDIM, B_HDIM), lambda s, t: (t[6, s], 0, 0, 0, 0))
    hg_out = pl.BlockSpec((None, 2, C_HEADS, C_KDIM, C_KDIM), lambda s, t: (t[6, s], 0, 0, 0, 0))
    cblk = lambda im: pl.BlockSpec((cl, C_WIDTH), im)
    yshape = jax.ShapeDtypeStruct((tok.total, B_WIDTH), F32)
    oshape = jax.ShapeDtypeStruct((tok.total, C_WIDTH), F32)
    return pl.pallas_call(
        _mix_scan_kernel,
        grid_spec=pltpu.PrefetchScalarGridSpec(
            num_scalar_prefetch=1,
            grid=(tab.shape[1],),
            in_specs=[blk2(fwd2), blk2(fwd2), blk2(fwd2), blk2(fwd2), blk(fwd(0)), glf,
                      blk2(bwd2), blk2(bwd2), blk2(bwd2), blk2(bwd2), blk(bwd(0)), glb, rw_in,
                      cblk(fwd(0)), cblk(fwd(1)), cblk(fwd(3)), cblk(bwd(0)), cblk(bwd(2)), cblk(bwd(3)),
                      pl.BlockSpec((2, C_WIDTH), lambda s, t: (0, 0)), hg_in],
            out_specs=[blk(fwd(0)), blk(bwd(0)), rw_out, cblk(fwd(0)), cblk(bwd(0)), hg_out],
            scratch_shapes=[pltpu.VMEM((2, B_HEADS // 2, LANES, LANES), F32),
                            pltpu.VMEM((2, C_HEADS, C_KDIM, C_KDIM), F32)]),
        out_shape=[yshape, yshape, jax.ShapeDtypeStruct((tok.n_ctx,) + rw0.shape[2:], F32),
                   oshape, oshape, jax.ShapeDtypeStruct((tok.n_ctx,) + hg0.shape[2:], F32)],
        compiler_params=_cparams("arbitrary"),
        name="rwkv_hgrn_scan",
    )(tab, at, rt, bt, kt, v, gl, at, rt, bt, kt, v, gl, rw0, zc, zc, zc, zc, zc, zc, lb, hg0)


def _hgrn_post_kernel(of_ref, ob_ref, g_ref, gn_ref, o_ref):
    o = of_ref[...] + ob_ref[...]
    gate = _silu(g_ref[...])
    outs = []
    for h in range(C_HEADS):
        cols = slice(h * C_KDIM, (h + 1) * C_KDIM)
        oh = o[:, cols]
        ms = jnp.mean(oh * oh, axis=-1, keepdims=True)
        outs.append(oh * lax.rsqrt(ms + EPS) * gn_ref[...] * gate[:, cols])
    o_ref[...] = jnp.concatenate(outs, axis=-1).astype(o_ref.dtype)


def _hgrn_post(of, ob, zc, gn):
    t = of.shape[0]
    tm = TOKEN_TILE
    tile = pl.BlockSpec((tm, C_WIDTH), lambda i: (i, 0))
    return pl.pallas_call(
        _hgrn_post_kernel,
        grid=(t // tm,),
        in_specs=[tile, tile, pl.BlockSpec((tm, C_WIDTH), lambda i: (i, 4)),
                  pl.BlockSpec((1, C_KDIM), lambda i: (0, 0))],
        out_specs=tile,
        out_shape=jax.ShapeDtypeStruct((t, C_WIDTH), BF16),
        compiler_params=_cparams("parallel"),
        name="hgrn2_post",
    )(of, ob, zc, gn)


def _mix_out_kernel(oa_ref, ob_ref, oc_ref, wa_ref, wb_ref, wc_ref, x_ref, gt_ref,
                    gn_ref, scn_ref, shn_ref, xo_ref, ho_ref):
    y = _dot(oa_ref[...], wa_ref[...]) + _dot(ob_ref[...], wb_ref[...]) + _dot(oc_ref[...], wc_ref[...])
    xn = x_ref[...] + gt_ref[...] * y
    xo_ref[...] = xn
    ho_ref[...] = _norm_mod(xn, gn_ref[...], scn_ref[...], shn_ref[...]).astype(ho_ref.dtype)


def _mix_out(tok, oa, ob, oc, wa, wb, wc, x, modr, layer, next_g):
    tm = TOKEN_TILE
    row = pl.BlockSpec((tm, D_MODEL), lambda i: (i, 0))
    full = lambda a: pl.BlockSpec(a.shape, lambda i: (0, 0))
    return pl.pallas_call(
        _mix_out_kernel,
        grid=(tok.total // tm,),
        in_specs=[pl.BlockSpec((tm, A_WIDTH), lambda i: (i, 0)),
                  pl.BlockSpec((tm, B_WIDTH), lambda i: (i, 0)),
                  pl.BlockSpec((tm, C_WIDTH), lambda i: (i, 0)),
                  full(wa), full(wb), full(wc), row,
                  _mod_spec(tok, tm, layer, 5, 1),
                  pl.BlockSpec((1, D_MODEL), lambda i: (0, 0)),
                  _mod_spec(tok, tm, layer, 7, 1), _mod_spec(tok, tm, layer, 6, 1)],
        out_specs=[row, row],
        out_shape=[jax.ShapeDtypeStruct((tok.total, D_MODEL), F32),
                   jax.ShapeDtypeStruct((tok.total, D_MODEL), BF16)],
        compiler_params=_cparams("parallel"),
        name="mix_out_proj",
    )(oa, ob, oc, wa, wb, wc, x, modr, next_g, modr, modr)


def kernel(x_prompt, x_sample, state_rwkv, state_hgrn, c, c_ctx, norm_g, w_mod, b_mod, ffn_w1, ffn_w3,
           ffn_w2, w_in, w_out, mlp_norm_g, mlp_ws, mlp_bs, rwkv_mu, rwkv_w0, rwkv_w2, rwkv_a0, rwkv_a2,
           rwkv_g2, rwkv_kk, rwkv_ka, rwkv_rk, rwkv_lnx_g, rwkv_lnx_b, hgrn_lb, hgrn_gn, final_g):
    n_ctx, ctx_len, _ = x_prompt.shape
    n_lat, lat_len, _ = x_sample.shape
    depth = w_mod.shape[0]
    tok = _Tokens(n_ctx, ctx_len, n_lat, lat_len)
    assert n_lat + 1 <= 8 and ctx_len % PREP_TILE == 0 and lat_len % TOKEN_TILE == 0

    cond8 = jnp.zeros((8, D_MODEL), F32).at[0].set(c_ctx).at[1:1 + n_lat].set(c)
    mod = _mod_rows(cond8, w_mod, b_mod)
    modr = mod.reshape(depth, 8, N_MOD, D_MODEL).transpose(0, 2, 1, 3)[:, :, :, None, :]

    sm = jax.nn.softmax(hgrn_lb.astype(F32), axis=0)
    lower = jnp.cumsum(sm, axis=0) - sm[0]

    x = jnp.concatenate([x_prompt.reshape(-1, D_MODEL), x_sample.reshape(-1, D_MODEL)], axis=0)
    zero_sc = jnp.zeros((1, D_MODEL), F32)
    zero_spec = pl.BlockSpec((1, D_MODEL), lambda i, j: (0, 0))
    tm = TOKEN_TILE

    h = _normmod(tok, x, norm_g[0, 0][None], modr, 0, 1, 0)
    bw = lambda a: a.astype(BF16)
    w1b, w3b, w2b = bw(ffn_w1), bw(ffn_w3), bw(ffn_w2)
    rw_states, hg_states = [], []
    for l in range(depth):
        x, h = _ffn(tok, h, x, modr, l, 0, 2, w1b, w3b, w2b,
                    norm_g[l, 1][None], (modr, _mod_spec(tok, tm, l, 4, 2)),
                    (modr, _mod_spec(tok, tm, l, 3, 2)), False)

        za = _project(h, w_in, l, 0, A_COLS, 512)
        zb = _project(h, w_in, l, A_COLS, B_COLS, 896)
        zc = _project(h, w_in, l, A_COLS + B_COLS, C_COLS, 768)

        oa = _gmlp(za, mlp_norm_g[l], mlp_ws[l], mlp_bs[l])

        at, rt, bt, kt, v, gl, gate, bonus = _rwkv_prep(
            tok, zb, rwkv_mu[l][None], rwkv_w0[l], rwkv_w2[l], rwkv_a0[l], rwkv_a2[l], rwkv_g2[l],
            rwkv_kk[l][None], rwkv_ka[l][None], rwkv_rk[l].reshape(1, B_WIDTH))
        yf, yb, s_rw, of, obw, s_hg = _mix_scan(tok, l, at, rt, bt, kt, v, gl, state_rwkv, zc, lower[l],
                                                state_hgrn)
        ob = _rwkv_post(yf, yb, bonus, gate,
                        rwkv_lnx_g[l].reshape(1, B_WIDTH), rwkv_lnx_b[l].reshape(1, B_WIDTH))
        oc = _hgrn_post(of, obw, zc, hgrn_gn[l][None])
        rw_states.append(s_rw)
        hg_states.append(s_hg)

        w_out_l = bw(w_out[l])
        x, h = _mix_out(tok, oa, ob, oc, w_out_l[:A_WIDTH], w_out_l[A_WIDTH:A_WIDTH + B_WIDTH],
                        w_out_l[A_WIDTH + B_WIDTH:], x, modr, l, norm_g[l, 2][None])

        if l + 1 < depth:
            x, h = _ffn(tok, h, x, modr, l, 1, 8, w1b, w3b, w2b,
                        norm_g[l + 1, 0][None], (modr, _mod_spec(tok, tm, l + 1, 1, 2)),
                        (modr, _mod_spec(tok, tm, l + 1, 0, 2)), False)
        else:
            y_prompt, y_sample = _ffn(tok, h, x, modr, l, 1, 8, w1b, w3b, w2b,
                                      final_g[None], (zero_sc, zero_spec), (zero_sc, zero_spec), True)

    return (y_prompt.reshape(x_prompt.shape), y_sample.reshape(x_sample.shape),
            jnp.stack(rw_states, axis=1), jnp.stack(hg_states, axis=1))
```

```python
import functools
import math

import jax
import jax.numpy as jnp
from jax import lax
from jax.experimental import pallas as pl
from jax.experimental.pallas import tpu as pltpu

F32 = jnp.float32
BF16 = jnp.bfloat16

D_MODEL = 2048
D_FF = 5632
N_MOD = 9
EPS = 1e-6
LNX_EPS = 64e-5

CHUNK_MLP = 128
A_WIDTH = 512
A_GROUPS = 4
A_GDIM = 128
B_HDIM = 64
B_WIDTH = 768
B_HEADS = 12
LORA = 64
GATE_LORA = 128
B_COLS = 3 * B_WIDTH + 4 * LORA + GATE_LORA
C_KDIM = 128
C_WIDTH = 768
C_HEADS = 6
C_COLS = 5 * C_WIDTH
A_COLS = 2 * A_WIDTH

LANES = 128
VMEM_LIMIT = 56 * 1024 * 1024

TOKEN_TILE = 512
FF_TILE = 512
PREP_TILE = 256
RW_CHUNK = 64
HG_BLOCK = 64
HG_SUB = 16


def _cparams(*sem):
    return pltpu.CompilerParams(dimension_semantics=sem, vmem_limit_bytes=VMEM_LIMIT)


def _dot(a, b):
    return jnp.dot(a, b, preferred_element_type=F32)


def _dot_nt(a, b):
    return lax.dot_general(a, b, (((1,), (1,)), ((), ())), preferred_element_type=F32)


def _dot_tn(a, b):
    return lax.dot_general(a, b, (((0,), (0,)), ((), ())), preferred_element_type=F32)


def _split(x, parts):
    out = []
    for _ in range(parts - 1):
        hi = x.astype(BF16)
        out.append(hi)
        x = x - hi.astype(F32)
    out.append(x.astype(BF16))
    return out


def _exact_left(m, x, parts):
    acc = None
    for p in _split(x, parts):
        t = _dot(m, p)
        acc = t if acc is None else acc + t
    return acc


def _exact_right(x, m, parts):
    acc = None
    for p in _split(x, parts):
        t = _dot(p, m)
        acc = t if acc is None else acc + t
    return acc


def _iota(shape, dim):
    return lax.broadcasted_iota(jnp.int32, shape, dim)


def _sigmoid(x):
    return jax.nn.sigmoid(x)


def _silu(x):
    return x * jax.nn.sigmoid(x)


def _norm_mod(x, g, sc, sh):
    ms = jnp.mean(x * x, axis=-1, keepdims=True)
    return (x * lax.rsqrt(ms + EPS) * g) * (1.0 + sc) + sh


def _mod_kernel(cond_ref, w_ref, b_ref, o_ref):
    s = _silu(cond_ref[...]).astype(BF16)
    o_ref[...] = _dot(s, w_ref[...].astype(BF16)) + b_ref[...]


def _mod_rows(cond8, w_mod, b_mod):
    depth, _, n = w_mod.shape
    tn = 1024
    return pl.pallas_call(
        _mod_kernel,
        grid=(depth, n // tn),
        in_specs=[
            pl.BlockSpec((8, D_MODEL), lambda l, j: (0, 0)),
            pl.BlockSpec((None, D_MODEL, tn), lambda l, j: (l, 0, j)),
            pl.BlockSpec((None, 1, tn), lambda l, j: (l, 0, j)),
        ],
        out_specs=pl.BlockSpec((None, 8, tn), lambda l, j: (l, 0, j)),
        out_shape=jax.ShapeDtypeStruct((depth, 8, n), F32),
        compiler_params=_cparams("parallel", "parallel"),
        name="adaln_rows",
    )(cond8, w_mod, b_mod.reshape(depth, 1, n))


class _Tokens:
    def __init__(self, n_ctx, ctx_len, n_lat, lat_len):
        self.n_ctx, self.ctx_len, self.n_lat, self.lat_len = n_ctx, ctx_len, n_lat, lat_len
        self.ctx_tokens = n_ctx * ctx_len
        self.total = self.ctx_tokens + n_lat * lat_len

    def mod_row(self, tile, i):
        nct = self.ctx_tokens // tile
        per_lat = self.lat_len // tile
        return jnp.where(i < nct, 0, 1 + (i - nct) // per_lat)


def _mod_spec(tok, tile, layer, which, grid_rank):
    def imap(*idx):
        return (layer, which, tok.mod_row(tile, idx[0]), 0, 0)
    del grid_rank
    return pl.BlockSpec((None, None, None, 1, D_MODEL), imap)


def _normmod_kernel(x_ref, g_ref, sc_ref, sh_ref, h_ref):
    h_ref[...] = _norm_mod(x_ref[...], g_ref[...], sc_ref[...], sh_ref[...]).astype(h_ref.dtype)


def _normmod(tok, x, norm_g_row, modr, layer, sc_i, sh_i):
    tm = TOKEN_TILE
    return pl.pallas_call(
        _normmod_kernel,
        grid=(tok.total // tm,),
        in_specs=[
            pl.BlockSpec((tm, D_MODEL), lambda i: (i, 0)),
            pl.BlockSpec((1, D_MODEL), lambda i: (0, 0)),
            _mod_spec(tok, tm, layer, sc_i, 1),
            _mod_spec(tok, tm, layer, sh_i, 1),
        ],
        out_specs=pl.BlockSpec((tm, D_MODEL), lambda i: (i, 0)),
        out_shape=jax.ShapeDtypeStruct((tok.total, D_MODEL), BF16),
        compiler_params=_cparams("parallel"),
        name="first_norm",
    )(x, norm_g_row, modr, modr)


def _ffn_kernel(h_ref, x_ref, gt_ref, w1_ref, w3_ref, w2_ref, gn_ref, scn_ref, shn_ref,
                out_a_ref, out_b_ref, acc_ref, *, n_ff, n_ctx_tiles):
    i = pl.program_id(0)
    j = pl.program_id(1)

    @pl.when(j == 0)
    def _():
        acc_ref[...] = jnp.zeros_like(acc_ref)

    h = h_ref[...]
    a = _dot(h, w1_ref[...])
    b = _dot(h, w3_ref[...])
    p = (_silu(a) * b).astype(BF16)
    acc_ref[...] += _dot(p, w2_ref[...])

    @pl.when(j == n_ff - 1)
    def _():
        xn = x_ref[...] + 0.5 * gt_ref[...] * acc_ref[...]
        hn = _norm_mod(xn, gn_ref[...], scn_ref[...], shn_ref[...])
        if n_ctx_tiles is None:
            out_a_ref[...] = xn
            out_b_ref[...] = hn.astype(out_b_ref.dtype)
        else:
            @pl.when(i < n_ctx_tiles)
            def _():
                out_a_ref[...] = hn

            @pl.when(i >= n_ctx_tiles)
            def _():
                out_b_ref[...] = hn


def _ffn(tok, h, x, modr, layer, which, gate_i, w1, w3, w2, next_g, next_sc, next_sh, final):
    tm, tf = TOKEN_TILE, FF_TILE
    n_ff = D_FF // tf
    nct = tok.ctx_tokens // tm
    row = pl.BlockSpec((tm, D_MODEL), lambda i, j: (i, 0))
    if final:
        out_specs = [pl.BlockSpec((tm, D_MODEL), lambda i, j: (jnp.minimum(i, nct - 1), 0)),
                     pl.BlockSpec((tm, D_MODEL), lambda i, j: (jnp.maximum(i - nct, 0), 0))]
        out_shape = [jax.ShapeDtypeStruct((tok.ctx_tokens, D_MODEL), F32),
                     jax.ShapeDtypeStruct((tok.total - tok.ctx_tokens, D_MODEL), F32)]
    else:
        out_specs = [row, row]
        out_shape = [jax.ShapeDtypeStruct((tok.total, D_MODEL), F32),
                     jax.ShapeDtypeStruct((tok.total, D_MODEL), BF16)]
    return pl.pallas_call(
        functools.partial(_ffn_kernel, n_ff=n_ff, n_ctx_tiles=nct if final else None),
        grid=(tok.total // tm, n_ff),
        in_specs=[
            row, row,
            _mod_spec(tok, tm, layer, gate_i, 2),
            pl.BlockSpec((None, None, D_MODEL, tf), lambda i, j: (layer, which, 0, j)),
            pl.BlockSpec((None, None, D_MODEL, tf), lambda i, j: (layer, which, 0, j)),
            pl.BlockSpec((None, None, tf, D_MODEL), lambda i, j: (layer, which, j, 0)),
            pl.BlockSpec((1, D_MODEL), lambda i, j: (0, 0)),
            next_sc[1], next_sh[1],
        ],
        out_specs=out_specs,
        out_shape=out_shape,
        scratch_shapes=[pltpu.VMEM((tm, D_MODEL), F32)],
        compiler_params=_cparams("arbitrary", "arbitrary"),
        name="swiglu_half_step",
    )(h, x, modr, w1, w3, w2, next_g, next_sc[0], next_sh[0])


def _project_kernel(h_ref, w_ref, o_ref, wb_ref):
    @pl.when(pl.program_id(1) == 0)
    def _():
        wb_ref[...] = w_ref[...].astype(BF16)

    o_ref[...] = _dot(h_ref[...], wb_ref[...])


def _project(h, w_in, layer, col0, n, tn):
    t, k = h.shape
    tm = TOKEN_TILE
    return pl.pallas_call(
        _project_kernel,
        grid=(n // tn, t // tm),
        in_specs=[pl.BlockSpec((tm, k), lambda j, i: (i, 0)),
                  pl.BlockSpec((pl.Squeezed(), pl.Element(k), pl.Element(tn)),
                               lambda j, i: (layer, 0, pl.multiple_of(col0 + j * tn, LANES)))],
        out_specs=pl.BlockSpec((tm, tn), lambda j, i: (i, j)),
        out_shape=jax.ShapeDtypeStruct((t, n), F32),
        scratch_shapes=[pltpu.VMEM((k, tn), BF16)],
        compiler_params=_cparams("parallel", "arbitrary"),
        name="mix_in_proj",
    )(h, w_in)


def _gmlp_kernel(h_ref, w_ref, ng_ref, ws_ref, bs_ref, o_ref, wb_ref):
    @pl.when(pl.program_id(0) == 0)
    def _():
        wb_ref[...] = w_ref[...].astype(BF16)

    z = jax.nn.gelu(_dot(h_ref[...], wb_ref[...]))
    for c in range(z.shape[0] // CHUNK_MLP):
        rows = slice(c * CHUNK_MLP, (c + 1) * CHUNK_MLP)
        outs = []
        for g in range(A_GROUPS):
            u = z[rows, g * A_GDIM:(g + 1) * A_GDIM]
            v = z[rows, A_WIDTH + g * A_GDIM:A_WIDTH + (g + 1) * A_GDIM]
            ms = jnp.mean(v * v, axis=-1, keepdims=True)
            v = v * lax.rsqrt(ms + EPS) * ng_ref[g:g + 1, :]
            mixed = _dot(ws_ref[g], v.astype(BF16)) + bs_ref[g]
            outs.append(u * mixed)
        o_ref[rows, :] = jnp.concatenate(outs, axis=-1).astype(o_ref.dtype)


def _gmlp(h, w_in, layer, ng, ws, bs):
    t, k = h.shape
    tm = TOKEN_TILE
    return pl.pallas_call(
        _gmlp_kernel,
        grid=(t // tm,),
        in_specs=[
            pl.BlockSpec((tm, k), lambda i: (i, 0)),
            pl.BlockSpec((pl.Squeezed(), pl.Element(k), pl.Element(A_COLS)), lambda i: (layer, 0, 0)),
            pl.BlockSpec((A_GROUPS, A_GDIM), lambda i: (0, 0)),
            pl.BlockSpec((A_GROUPS, CHUNK_MLP, CHUNK_MLP), lambda i: (0, 0, 0)),
            pl.BlockSpec((A_GROUPS, CHUNK_MLP, 1), lambda i: (0, 0, 0)),
        ],
        out_specs=pl.BlockSpec((tm, A_WIDTH), lambda i: (i, 0)),
        out_shape=jax.ShapeDtypeStruct((t, A_WIDTH), BF16),
        scratch_shapes=[pltpu.VMEM((k, A_COLS), BF16)],
        compiler_params=_cparams("arbitrary"),
        name="gmlp_chunk_mix",
    )(h, w_in, ng, ws.astype(BF16), bs[:, :, None])


def _head_ones():
    r = _iota((LANES, LANES), 0) // B_HDIM
    c = _iota((LANES, LANES), 1) // B_HDIM
    return (r == c).astype(BF16)


def _head_sum(x, parts=2):
    ones = _head_ones()
    outs = []
    for s in range(x.shape[-1] // LANES):
        outs.append(_exact_right(x[:, s * LANES:(s + 1) * LANES], ones, parts))
    return jnp.concatenate(outs, axis=-1)


def _rwkv_prep_kernel(z_ref, zp_ref, zn_ref, mu_ref, w0_ref, w2_ref, a0_ref, a2_ref, g2_ref,
                      kkw_ref, kaw_ref, rk_ref,
                      at_ref, rt_ref, bt_ref, kt_ref, v_ref, gl_ref, gate_ref, bonus_ref,
                      *, n_ctx_tiles, ctx_tiles_per_seq, lat_tiles_per_seq):
    i = pl.program_id(0)
    pos = jnp.where(i < n_ctx_tiles, i % ctx_tiles_per_seq, (i - n_ctx_tiles) % lat_tiles_per_seq)
    last = jnp.where(i < n_ctx_tiles, ctx_tiles_per_seq - 1, lat_tiles_per_seq - 1)
    z = z_ref[...]
    t = z.shape[0]
    row = _iota((t, 1), 0)
    halo_prev = jnp.where(pos != 0, zp_ref[7:8, :], 0.0)
    halo_next = jnp.where(pos != last, zn_ref[0:1, :], 0.0)
    prev = jnp.where(row == 0, halo_prev, pltpu.roll(z, 1, 0))
    nxt = jnp.where(row == t - 1, halo_next, pltpu.roll(z, t - 1, 0))
    z = z + (0.5 * (prev + nxt) - z) * mu_ref[...]

    w = B_WIDTH
    r, k, v = z[:, 0:w], z[:, w:2 * w], z[:, 2 * w:3 * w]
    wd = z[:, 3 * w:3 * w + 2 * LORA]
    ad = z[:, 3 * w + 2 * LORA:3 * w + 4 * LORA]
    gd = z[:, 3 * w + 4 * LORA:]
    twd = jnp.tanh(wd).astype(BF16)
    adb = ad.astype(BF16)

    kk = k * kkw_ref[...]
    nrm = jnp.sqrt(_head_sum(kk * kk))
    kk = kk / jnp.maximum(nrm, 1e-12)

    v_ref[...] = v.astype(BF16)
    gate_ref[...] = _dot(_sigmoid(gd).astype(BF16), g2_ref[...])
    cl = RW_CHUNK
    ri, ci = _iota((t, t), 0), _iota((t, t), 1)
    same_chunk = (ri // cl) == (ci // cl)
    ksum = jnp.zeros_like(k)
    for d in range(2):
        w_raw = w0_ref[d:d + 1, :] + _dot(twd[:, d * LORA:(d + 1) * LORA], w2_ref[d])
        lw = -math.exp(-0.5) * _sigmoid(w_raw)
        a = _sigmoid(a0_ref[d:d + 1, :] + _dot(adb[:, d * LORA:(d + 1) * LORA], a2_ref[d]))
        kd = k * (1.0 + (a - 1.0) * kaw_ref[...])
        ksum = ksum + kd
        upto = same_chunk & ((ci >= ri) if d == 1 else (ci <= ri))
        g = _exact_left(upto.astype(BF16), lw, 2)
        ieg = jnp.exp(-g)
        at_ref[d] = (-kk * jnp.exp(g - lw)).astype(BF16)
        rt_ref[d] = (r * jnp.exp(g)).astype(BF16)
        bt_ref[d] = (kk * a * ieg).astype(BF16)
        kt_ref[d] = (kd * ieg).astype(BF16)
        for c in range(t // cl):
            end = c * cl if d == 1 else (c + 1) * cl - 1
            gl_ref[d, c] = jnp.exp(g[end:end + 1, :])
    bonus_ref[...] = _head_sum(r * ksum * rk_ref[...]) * v


def _rwkv_prep(tok, zb, mu, w0, w2, a0, a2, g2, kkw, kaw, rk):
    tp = PREP_TILE
    n_tiles = tok.total // tp
    rows8 = tok.total // 8
    per = tp // 8
    full = lambda shape: pl.BlockSpec(shape, lambda i: (0,) * len(shape))
    cpt = tp // RW_CHUNK
    tile = pl.BlockSpec((tp, B_WIDTH), lambda i: (i, 0))
    tile2 = pl.BlockSpec((2, tp, B_WIDTH), lambda i: (0, i, 0))
    gl_spec = pl.BlockSpec((2, cpt, 1, B_WIDTH), lambda i: (0, i, 0, 0))
    o1 = jax.ShapeDtypeStruct((tok.total, B_WIDTH), F32)
    o1b = jax.ShapeDtypeStruct((tok.total, B_WIDTH), BF16)
    o2b = jax.ShapeDtypeStruct((2, tok.total, B_WIDTH), BF16)
    ogl = jax.ShapeDtypeStruct((2, tok.total // RW_CHUNK, 1, B_WIDTH), F32)
    kern = functools.partial(_rwkv_prep_kernel, n_ctx_tiles=tok.ctx_tokens // tp,
                             ctx_tiles_per_seq=tok.ctx_len // tp, lat_tiles_per_seq=tok.lat_len // tp)
    return pl.pallas_call(
        kern,
        grid=(n_tiles,),
        in_specs=[
            pl.BlockSpec((tp, B_COLS), lambda i: (i, 0)),
            pl.BlockSpec((8, B_COLS), lambda i: (jnp.maximum(i * per - 1, 0), 0)),
            pl.BlockSpec((8, B_COLS), lambda i: (jnp.minimum((i + 1) * per, rows8 - 1), 0)),
            full((1, B_COLS)), full((2, B_WIDTH)), full((2, LORA, B_WIDTH)), full((2, B_WIDTH)),
            full((2, LORA, B_WIDTH)), full((GATE_LORA, B_WIDTH)),
            full((1, B_WIDTH)), full((1, B_WIDTH)), full((1, B_WIDTH)),
        ],
        out_specs=[tile2, tile2, tile2, tile2, tile, gl_spec, tile, tile],
        out_shape=[o2b, o2b, o2b, o2b, o1b, ogl, o1, o1],
        compiler_params=_cparams("parallel"),
        name="rwkv_prep",
    )(zb, zb, zb, mu, w0, w2.astype(BF16), a0, a2.astype(BF16), g2.astype(BF16), kkw, kaw, rk)


def _pair_blockdiag(x):
    lane = _iota(x.shape, 1)
    zero = jnp.zeros_like(x)
    return jnp.concatenate([jnp.where(lane < B_HDIM, x, zero), jnp.where(lane >= B_HDIM, x, zero)], axis=0)


def _rwkv_chunks(streams, s_ref, y_refs):
    c = streams[0][0].shape[0]
    n_pairs = B_HEADS // 2
    rr, cc = _iota((2 * c, 2 * c), 0), _iota((2 * c, 2 * c), 1)
    same = (rr // c) == (cc // c)
    tt, ss = rr % c, cc % c
    eye = (rr == cc).astype(F32)
    pair_masks = []
    n = 1
    while n < c:
        pair_masks.append(same & ((tt // (2 * n)) == (ss // (2 * n))) & ((tt // n) != (ss // n)))
        n *= 2

    units = []
    for d, (a_t, r_t, b_t, k_t, vb, gl, reverse) in enumerate(streams):
        before = same & ((ss > tt) if reverse else (ss < tt))
        upto = same & ((ss >= tt) if reverse else (ss <= tt))
        for p in range(n_pairs):
            sl = slice(p * LANES, (p + 1) * LANES)
            units.append(dict(d=d, p=p, before=before, upto=upto, gl=gl[:, sl],
                              ar=jnp.concatenate([_pair_blockdiag(a_t[:, sl]), _pair_blockdiag(r_t[:, sl])], 0),
                              bk=jnp.concatenate([_pair_blockdiag(b_t[:, sl]), _pair_blockdiag(k_t[:, sl])], 0),
                              v=_pair_blockdiag(vb[:, sl])))
    m = 2 * c
    for u in units:
        u["gram"] = _dot_nt(u["ar"], u["bk"])
        u["s0"] = s_ref[u["d"], u["p"]]
    yield
    for u in units:
        u["ws"] = _dot_nt(u["ar"], u["s0"].astype(BF16))
        g = u["gram"]
        u["lmat"] = jnp.where(u["before"], g[:m, :m], 0.0)
        aak = jnp.where(u["before"], g[:m, m:], 0.0).astype(BF16)
        u["rbk"] = jnp.concatenate([jnp.where(u["upto"], g[m:, :m], 0.0),
                                    jnp.where(u["upto"], g[m:, m:], 0.0)], axis=1).astype(BF16)
        u["x"] = u["ws"][:m] + _dot(aak, u["v"])
    yield
    for lvl, pm in enumerate(pair_masks):
        for u in units:
            link = jnp.where(pm, u["lmat"], 0.0)
            if lvl == 0:
                u["tinv"] = eye + link
            else:
                u["tb"] = u["tinv"].astype(BF16)
                u["tmp"] = _dot(link.astype(BF16), u["tb"]).astype(BF16)
        if lvl > 0:
            yield
            for u in units:
                u["tinv"] = u["tinv"] + _dot(u["tb"], u["tmp"])
            yield
    for u in units:
        u["uv"] = jnp.concatenate([_dot(u["tinv"].astype(BF16), u["x"].astype(BF16)).astype(BF16), u["v"]], 0)
    yield
    ys = [[None] * n_pairs for _ in streams]
    for u in units:
        y = u["ws"][m:] + _dot(u["rbk"], u["uv"])
        ys[u["d"]][u["p"]] = y[:c] + y[c:]
        s_ref[u["d"], u["p"]] = (u["s0"] + _dot_tn(u["uv"], u["bk"])) * u["gl"]
    for d, row in enumerate(ys):
        y_refs[d][...] = jnp.concatenate(row, axis=-1)


def _scan_table(tok, chunk):
    cols = []
    for s in range(tok.n_ctx + tok.n_lat):
        if s < tok.n_ctx:
            base, nc = s * tok.ctx_len // chunk, tok.ctx_len // chunk
        else:
            base = (tok.ctx_tokens + (s - tok.n_ctx) * tok.lat_len) // chunk
            nc = tok.lat_len // chunk
        for c in range(nc):
            cols.append((base + c, base + nc - 1 - c, int(s < tok.n_ctx), int(c == 0), int(c == nc - 1),
                         max(s - tok.n_ctx, 0), min(s, tok.n_ctx - 1)))
    return jnp.asarray(list(zip(*cols)), dtype=jnp.int32)


def _rwkv_post_kernel(yf_ref, yb_ref, bonus_ref, gate_ref, g_ref, b_ref, o_ref):
    y = yf_ref[...] + yb_ref[...]
    mean = _head_sum(y, 3) * (1.0 / B_HDIM)
    yc = y - mean
    var = _head_sum(yc * yc) * (1.0 / B_HDIM)
    y = yc * lax.rsqrt(var + LNX_EPS) * g_ref[...] + b_ref[...] + bonus_ref[...]
    o_ref[...] = (y * gate_ref[...]).astype(o_ref.dtype)


def _rwkv_post(yf, yb, bonus, gate, lnx_g, lnx_b):
    t = yf.shape[0]
    tm = TOKEN_TILE
    tile = pl.BlockSpec((tm, B_WIDTH), lambda i: (i, 0))
    vec = pl.BlockSpec((1, B_WIDTH), lambda i: (0, 0))
    return pl.pallas_call(
        _rwkv_post_kernel,
        grid=(t // tm,),
        in_specs=[tile, tile, tile, tile, vec, vec],
        out_specs=tile,
        out_shape=jax.ShapeDtypeStruct((t, B_WIDTH), BF16),
        compiler_params=_cparams("parallel"),
        name="rwkv_post",
    )(yf, yb, bonus, gate, lnx_g, lnx_b)


def _hgrn_block(qraw, fraw, v, lb, st_ref, o_ref, reverse):
    n = qraw.shape[0]
    sub = HG_SUB
    fg = lb + (1.0 - lb) * _sigmoid(fraw)
    logf = jnp.log(fg)
    kg = 1.0 - fg
    q = _silu(qraw)
    ri, ci = _iota((n, n), 0), _iota((n, n), 1)
    same = (ri // sub) == (ci // sub)
    upto = same & ((ci >= ri) if reverse else (ci <= ri))
    bcum = _exact_left(upto.astype(BF16), logf, 3)
    half = 8
    hi = _iota((half, 1), 0)
    order = range(n // sub - 1, -1, -1) if reverse else range(n // sub)
    out_rows = [None] * (n // sub)
    for sc in order:
        last = sc * sub if reverse else (sc + 1) * sub - 1
        outs = []
        for h in range(C_HEADS):
            cols = slice(h * C_KDIM, (h + 1) * C_KDIM)
            rows = slice(sc * sub, (sc + 1) * sub)
            qs, ks, vs, bs = q[rows, cols], kg[rows, cols], v[rows, cols], bcum[rows, cols]
            b_end = bcum[last:last + 1, cols]
            intra = [jnp.zeros((half, C_KDIM), F32) for _ in range(sub // half)]
            for j in range(sub):
                for hf in range(sub // half):
                    rs = slice(hf * half, (hf + 1) * half)
                    diff = bs[rs] - bs[j:j + 1, :]
                    if hf == j // half:
                        live = (hi + hf * half <= j) if reverse else (hi + hf * half >= j)
                        diff = jnp.where(live, diff, -1e30)
                    elif (hf > j // half) == reverse:
                        continue
                    att = jnp.sum(qs[rs] * jnp.exp(diff) * ks[j:j + 1, :], axis=-1, keepdims=True)
                    intra[hf] = intra[hf] + att * vs[j:j + 1, :]
            o = _dot_nt((qs * jnp.exp(bs)).astype(BF16), st_ref[h].astype(BF16))
            outs.append(o + jnp.concatenate(intra, axis=0))
            ke = (ks * jnp.exp(b_end - bs)).astype(BF16)
            st_ref[h] = st_ref[h] * jnp.exp(b_end) + _dot_tn(vs.astype(BF16), ke)
            if h % (C_HEADS // 2) == C_HEADS // 2 - 1:
                yield
        out_rows[sc] = jnp.concatenate(outs, axis=-1)
    o_ref[...] = jnp.concatenate(out_rows, axis=0)


def _mix_scan_kernel(tab_ref, atf_ref, rtf_ref, btf_ref, ktf_ref, vf_ref, glf_ref,
                     atb_ref, rtb_ref, btb_ref, ktb_ref, vb_ref, glb_ref, rw0_ref,
                     qf_ref, ff_ref, if_ref, qb_ref, fb_ref, ib_ref, lb_ref, hg0_ref,
                     yf_ref, yb_ref, rwfin_ref, of_ref, ob_ref, hgfin_ref, s_ref, st_ref):
    step = pl.program_id(0)
    hd = B_HDIM

    is_ctx = tab_ref[2, step] == 1
    first = tab_ref[3, step] == 1
    last = tab_ref[4, step] == 1

    @pl.when(first & is_ctx)
    def _():
        s_ref[...] = jnp.zeros_like(s_ref)
        st_ref[...] = jnp.zeros_like(st_ref)

    @pl.when(first & jnp.logical_not(is_ctx))
    def _():
        z = jnp.zeros((hd, hd), F32)
        for d in range(2):
            for p in range(B_HEADS // 2):
                s_ref[d, p] = jnp.concatenate(
                    [jnp.concatenate([rw0_ref[d, 2 * p], z], axis=1),
                     jnp.concatenate([z, rw0_ref[d, 2 * p + 1]], axis=1)], axis=0)
            for h in range(C_HEADS):
                st_ref[d, h] = hg0_ref[d, h].T

    pending = [
        _rwkv_chunks(
            [(atf_ref[...], rtf_ref[...], btf_ref[...], ktf_ref[...], vf_ref[...], glf_ref[...], False),
             (atb_ref[...], rtb_ref[...], btb_ref[...], ktb_ref[...], vb_ref[...], glb_ref[...], True)],
            s_ref, (yf_ref, yb_ref)),
        _hgrn_block(qf_ref[...], ff_ref[...], if_ref[...], lb_ref[0:1, :], st_ref.at[0], of_ref, False),
        _hgrn_block(qb_ref[...], fb_ref[...], ib_ref[...], lb_ref[1:2, :], st_ref.at[1], ob_ref, True),
    ]
    while pending:
        for gen in list(pending):
            if next(gen, pending) is pending:
                pending.remove(gen)

    @pl.when(last & is_ctx)
    def _():
        for d in range(2):
            for p in range(B_HEADS // 2):
                s = s_ref[d, p]
                rwfin_ref[d, 2 * p] = s[:hd, :hd]
                rwfin_ref[d, 2 * p + 1] = s[hd:, hd:]
            for h in range(C_HEADS):
                hgfin_ref[d, h] = st_ref[d, h].T


def _mix_scan(tok, layer, at, rt, bt, kt, v, gl, rw0, zc, lb, hg0):
    cl = RW_CHUNK
    assert HG_BLOCK == cl
    tab = _scan_table(tok, cl)
    fwd = lambda col: (lambda s, t: (t[0, s], col))
    bwd = lambda col: (lambda s, t: (t[1, s], col))
    fwd2 = lambda s, t: (0, t[0, s], 0)
    bwd2 = lambda s, t: (1, t[1, s], 0)
    blk = lambda im: pl.BlockSpec((cl, B_WIDTH), im)
    blk2 = lambda im: pl.BlockSpec((None, cl, B_WIDTH), im)
    glf = pl.BlockSpec((None, None, 1, B_WIDTH), lambda s, t: (0, t[0, s], 0, 0))
    glb = pl.BlockSpec((None, None, 1, B_WIDTH), lambda s, t: (1, t[1, s], 0, 0))
    rw_in = pl.BlockSpec((None, None, 2, B_HEADS, B_HDIM, B_HDIM), lambda s, t: (t[5, s], layer, 0, 0, 0, 0))
    hg_in = pl.BlockSpec((None, None, 2, C_HEADS, C_KDIM, C_KDIM), lambda s, t: (t[5, s], layer, 0, 0, 0, 0))
    rw_out = pl.BlockSpec((None, 2, B_HEADS, B_HDIM, B_HDIM), lambda s, t: (t[6, s], 0, 0, 0, 0))
    hg_out = pl.BlockSpec((None, 2, C_HEADS, C_KDIM, C_KDIM), lambda s, t: (t[6, s], 0, 0, 0, 0))
    cblk = lambda im: pl.BlockSpec((cl, C_WIDTH), im)
    yshape = jax.ShapeDtypeStruct((tok.total, B_WIDTH), F32)
    oshape = jax.ShapeDtypeStruct((tok.total, C_WIDTH), F32)
    return pl.pallas_call(
        _mix_scan_kernel,
        grid_spec=pltpu.PrefetchScalarGridSpec(
            num_scalar_prefetch=1,
            grid=(tab.shape[1],),
            in_specs=[blk2(fwd2), blk2(fwd2), blk2(fwd2), blk2(fwd2), blk(fwd(0)), glf,
                      blk2(bwd2), blk2(bwd2), blk2(bwd2), blk2(bwd2), blk(bwd(0)), glb, rw_in,
                      cblk(fwd(0)), cblk(fwd(1)), cblk(fwd(3)), cblk(bwd(0)), cblk(bwd(2)), cblk(bwd(3)),
                      pl.BlockSpec((2, C_WIDTH), lambda s, t: (0, 0)), hg_in],
            out_specs=[blk(fwd(0)), blk(bwd(0)), rw_out, cblk(fwd(0)), cblk(bwd(0)), hg_out],
            scratch_shapes=[pltpu.VMEM((2, B_HEADS // 2, LANES, LANES), F32),
                            pltpu.VMEM((2, C_HEADS, C_KDIM, C_KDIM), F32)]),
        out_shape=[yshape, yshape, jax.ShapeDtypeStruct((tok.n_ctx,) + rw0.shape[2:], F32),
                   oshape, oshape, jax.ShapeDtypeStruct((tok.n_ctx,) + hg0.shape[2:], F32)],
        compiler_params=_cparams("arbitrary"),
        name="rwkv_hgrn_scan",
    )(tab, at, rt, bt, kt, v, gl, at, rt, bt, kt, v, gl, rw0, zc, zc, zc, zc, zc, zc, lb, hg0)


def _hgrn_post_kernel(of_ref, ob_ref, g_ref, gn_ref, o_ref):
    o = of_ref[...] + ob_ref[...]
    gate = _silu(g_ref[...])
    outs = []
    for h in range(C_HEADS):
        cols = slice(h * C_KDIM, (h + 1) * C_KDIM)
        oh = o[:, cols]
        ms = jnp.mean(oh * oh, axis=-1, keepdims=True)
        outs.append(oh * lax.rsqrt(ms + EPS) * gn_ref[...] * gate[:, cols])
    o_ref[...] = jnp.concatenate(outs, axis=-1).astype(o_ref.dtype)


def _hgrn_post(of, ob, zc, gn):
    t = of.shape[0]
    tm = TOKEN_TILE
    tile = pl.BlockSpec((tm, C_WIDTH), lambda i: (i, 0))
    return pl.pallas_call(
        _hgrn_post_kernel,
        grid=(t // tm,),
        in_specs=[tile, tile, pl.BlockSpec((tm, C_WIDTH), lambda i: (i, 4)),
                  pl.BlockSpec((1, C_KDIM), lambda i: (0, 0))],
        out_specs=tile,
        out_shape=jax.ShapeDtypeStruct((t, C_WIDTH), BF16),
        compiler_params=_cparams("parallel"),
        name="hgrn2_post",
    )(of, ob, zc, gn)


def _mix_out_kernel(oa_ref, ob_ref, oc_ref, wa_ref, wb_ref, wc_ref, x_ref, gt_ref,
                    gn_ref, scn_ref, shn_ref, xo_ref, ho_ref):
    y = _dot(oa_ref[...], wa_ref[...]) + _dot(ob_ref[...], wb_ref[...]) + _dot(oc_ref[...], wc_ref[...])
    xn = x_ref[...] + gt_ref[...] * y
    xo_ref[...] = xn
    ho_ref[...] = _norm_mod(xn, gn_ref[...], scn_ref[...], shn_ref[...]).astype(ho_ref.dtype)


def _mix_out(tok, oa, ob, oc, wa, wb, wc, x, modr, layer, next_g):
    tm = TOKEN_TILE
    row = pl.BlockSpec((tm, D_MODEL), lambda i: (i, 0))
    full = lambda a: pl.BlockSpec(a.shape, lambda i: (0, 0))
    return pl.pallas_call(
        _mix_out_kernel,
        grid=(tok.total // tm,),
        in_specs=[pl.BlockSpec((tm, A_WIDTH), lambda i: (i, 0)),
                  pl.BlockSpec((tm, B_WIDTH), lambda i: (i, 0)),
                  pl.BlockSpec((tm, C_WIDTH), lambda i: (i, 0)),
                  full(wa), full(wb), full(wc), row,
                  _mod_spec(tok, tm, layer, 5, 1),
                  pl.BlockSpec((1, D_MODEL), lambda i: (0, 0)),
                  _mod_spec(tok, tm, layer, 7, 1), _mod_spec(tok, tm, layer, 6, 1)],
        out_specs=[row, row],
        out_shape=[jax.ShapeDtypeStruct((tok.total, D_MODEL), F32),
                   jax.ShapeDtypeStruct((tok.total, D_MODEL), BF16)],
        compiler_params=_cparams("parallel"),
        name="mix_out_proj",
    )(oa, ob, oc, wa, wb, wc, x, modr, next_g, modr, modr)


def kernel(x_prompt, x_sample, state_rwkv, state_hgrn, c, c_ctx, norm_g, w_mod, b_mod, ffn_w1, ffn_w3,
           ffn_w2, w_in, w_out, mlp_norm_g, mlp_ws, mlp_bs, rwkv_mu, rwkv_w0, rwkv_w2, rwkv_a0, rwkv_a2,
           rwkv_g2, rwkv_kk, rwkv_ka, rwkv_rk, rwkv_lnx_g, rwkv_lnx_b, hgrn_lb, hgrn_gn, final_g):
    n_ctx, ctx_len, _ = x_prompt.shape
    n_lat, lat_len, _ = x_sample.shape
    depth = w_mod.shape[0]
    tok = _Tokens(n_ctx, ctx_len, n_lat, lat_len)
    assert n_lat + 1 <= 8 and ctx_len % PREP_TILE == 0 and lat_len % TOKEN_TILE == 0

    cond8 = jnp.zeros((8, D_MODEL), F32).at[0].set(c_ctx).at[1:1 + n_lat].set(c)
    mod = _mod_rows(cond8, w_mod, b_mod)
    modr = mod.reshape(depth, 8, N_MOD, D_MODEL).transpose(0, 2, 1, 3)[:, :, :, None, :]

    sm = jax.nn.softmax(hgrn_lb.astype(F32), axis=0)
    lower = jnp.cumsum(sm, axis=0) - sm[0]

    x = jnp.concatenate([x_prompt.reshape(-1, D_MODEL), x_sample.reshape(-1, D_MODEL)], axis=0)
    zero_sc = jnp.zeros((1, D_MODEL), F32)
    zero_spec = pl.BlockSpec((1, D_MODEL), lambda i, j: (0, 0))
    tm = TOKEN_TILE

    h = _normmod(tok, x, norm_g[0, 0][None], modr, 0, 1, 0)
    bw = lambda a: a.astype(BF16)
    w1b, w3b, w2b = bw(ffn_w1), bw(ffn_w3), bw(ffn_w2)
    rw_states, hg_states = [], []
    for l in range(depth):
        x, h = _ffn(tok, h, x, modr, l, 0, 2, w1b, w3b, w2b,
                    norm_g[l, 1][None], (modr, _mod_spec(tok, tm, l, 4, 2)),
                    (modr, _mod_spec(tok, tm, l, 3, 2)), False)

        zb = _project(h, w_in, l, A_COLS, B_COLS, 896)
        zc = _project(h, w_in, l, A_COLS + B_COLS, C_COLS, 768)

        oa = _gmlp(h, w_in, l, mlp_norm_g[l], mlp_ws[l], mlp_bs[l])

        at, rt, bt, kt, v, gl, gate, bonus = _rwkv_prep(
            tok, zb, rwkv_mu[l][None], rwkv_w0[l], rwkv_w2[l], rwkv_a0[l], rwkv_a2[l], rwkv_g2[l],
            rwkv_kk[l][None], rwkv_ka[l][None], rwkv_rk[l].reshape(1, B_WIDTH))
        yf, yb, s_rw, of, obw, s_hg = _mix_scan(tok, l, at, rt, bt, kt, v, gl, state_rwkv, zc, lower[l],
                                                state_hgrn)
        ob = _rwkv_post(yf, yb, bonus, gate,
                        rwkv_lnx_g[l].reshape(1, B_WIDTH), rwkv_lnx_b[l].reshape(1, B_WIDTH))
        oc = _hgrn_post(of, obw, zc, hgrn_gn[l][None])
        rw_states.append(s_rw)
        hg_states.append(s_hg)

        w_out_l = bw(w_out[l])
        x, h = _mix_out(tok, oa, ob, oc, w_out_l[:A_WIDTH], w_out_l[A_WIDTH:A_WIDTH + B_WIDTH],
                        w_out_l[A_WIDTH + B_WIDTH:], x, modr, l, norm_g[l, 2][None])

        if l + 1 < depth:
            x, h = _ffn(tok, h, x, modr, l, 1, 8, w1b, w3b, w2b,
                        norm_g[l + 1, 0][None], (modr, _mod_spec(tok, tm, l + 1, 1, 2)),
                        (modr, _mod_spec(tok, tm, l + 1, 0, 2)), False)
        else:
            y_prompt, y_sample = _ffn(tok, h, x, modr, l, 1, 8, w1b, w3b, w2b,
                                      final_g[None], (zero_sc, zero_spec), (zero_sc, zero_spec), True)

    return (y_prompt.reshape(x_prompt.shape), y_sample.reshape(x_sample.shape),
            jnp.stack(rw_states, axis=1), jnp.stack(hg_states, axis=1))
```

```python
import functools
import math

import jax
import jax.numpy as jnp
from jax import lax
from jax.experimental import pallas as pl
from jax.experimental.pallas import tpu as pltpu

F32 = jnp.float32
BF16 = jnp.bfloat16

D_MODEL = 2048
D_FF = 5632
N_MOD = 9
EPS = 1e-6
LNX_EPS = 64e-5

CHUNK_MLP = 128
A_WIDTH = 512
A_GROUPS = 4
A_GDIM = 128
B_HDIM = 64
B_WIDTH = 768
B_HEADS = 12
LORA = 64
GATE_LORA = 128
B_COLS = 3 * B_WIDTH + 4 * LORA + GATE_LORA
C_KDIM = 128
C_WIDTH = 768
C_HEADS = 6
C_COLS = 5 * C_WIDTH
A_COLS = 2 * A_WIDTH

LANES = 128
VMEM_LIMIT = 56 * 1024 * 1024

TOKEN_TILE = 512
FF_TILE = 512
PREP_TILE = 256
RW_CHUNK = 64
HG_BLOCK = 64
HG_SUB = 16


def _cparams(*sem):
    return pltpu.CompilerParams(dimension_semantics=sem, vmem_limit_bytes=VMEM_LIMIT)


def _dot(a, b):
    return jnp.dot(a, b, preferred_element_type=F32)


def _dot_nt(a, b):
    return lax.dot_general(a, b, (((1,), (1,)), ((), ())), preferred_element_type=F32)


def _dot_tn(a, b):
    return lax.dot_general(a, b, (((0,), (0,)), ((), ())), preferred_element_type=F32)


def _split(x, parts):
    out = []
    for _ in range(parts - 1):
        hi = x.astype(BF16)
        out.append(hi)
        x = x - hi.astype(F32)
    out.append(x.astype(BF16))
    return out


def _exact_left(m, x, parts):
    acc = None
    for p in _split(x, parts):
        t = _dot(m, p)
        acc = t if acc is None else acc + t
    return acc


def _exact_right(x, m, parts):
    acc = None
    for p in _split(x, parts):
        t = _dot(p, m)
        acc = t if acc is None else acc + t
    return acc


def _iota(shape, dim):
    return lax.broadcasted_iota(jnp.int32, shape, dim)


def _sigmoid(x):
    return jax.nn.sigmoid(x)


def _silu(x):
    return x * jax.nn.sigmoid(x)


def _norm_mod(x, g, sc, sh):
    ms = jnp.mean(x * x, axis=-1, keepdims=True)
    return (x * lax.rsqrt(ms + EPS) * g) * (1.0 + sc) + sh


def _mod_kernel(cond_ref, w_ref, b_ref, o_ref):
    s = _silu(cond_ref[...]).astype(BF16)
    o_ref[...] = _dot(s, w_ref[...].astype(BF16)) + b_ref[...]


def _mod_rows(cond8, w_mod, b_mod):
    depth, _, n = w_mod.shape
    tn = 1024
    return pl.pallas_call(
        _mod_kernel,
        grid=(depth, n // tn),
        in_specs=[
            pl.BlockSpec((8, D_MODEL), lambda l, j: (0, 0)),
            pl.BlockSpec((None, D_MODEL, tn), lambda l, j: (l, 0, j)),
            pl.BlockSpec((None, 1, tn), lambda l, j: (l, 0, j)),
        ],
        out_specs=pl.BlockSpec((None, 8, tn), lambda l, j: (l, 0, j)),
        out_shape=jax.ShapeDtypeStruct((depth, 8, n), F32),
        compiler_params=_cparams("parallel", "parallel"),
        name="adaln_rows",
    )(cond8, w_mod, b_mod.reshape(depth, 1, n))


class _Tokens:
    def __init__(self, n_ctx, ctx_len, n_lat, lat_len):
        self.n_ctx, self.ctx_len, self.n_lat, self.lat_len = n_ctx, ctx_len, n_lat, lat_len
        self.ctx_tokens = n_ctx * ctx_len
        self.total = self.ctx_tokens + n_lat * lat_len

    def mod_row(self, tile, i):
        nct = self.ctx_tokens // tile
        per_lat = self.lat_len // tile
        return jnp.where(i < nct, 0, 1 + (i - nct) // per_lat)


def _mod_spec(tok, tile, layer, which, grid_rank):
    def imap(*idx):
        return (layer, which, tok.mod_row(tile, idx[0]), 0, 0)
    del grid_rank
    return pl.BlockSpec((None, None, None, 1, D_MODEL), imap)


def _normmod_kernel(x_ref, g_ref, sc_ref, sh_ref, h_ref):
    h_ref[...] = _norm_mod(x_ref[...], g_ref[...], sc_ref[...], sh_ref[...]).astype(h_ref.dtype)


def _normmod(tok, x, norm_g_row, modr, layer, sc_i, sh_i):
    tm = TOKEN_TILE
    return pl.pallas_call(
        _normmod_kernel,
        grid=(tok.total // tm,),
        in_specs=[
            pl.BlockSpec((tm, D_MODEL), lambda i: (i, 0)),
            pl.BlockSpec((1, D_MODEL), lambda i: (0, 0)),
            _mod_spec(tok, tm, layer, sc_i, 1),
            _mod_spec(tok, tm, layer, sh_i, 1),
        ],
        out_specs=pl.BlockSpec((tm, D_MODEL), lambda i: (i, 0)),
        out_shape=jax.ShapeDtypeStruct((tok.total, D_MODEL), BF16),
        compiler_params=_cparams("parallel"),
        name="first_norm",
    )(x, norm_g_row, modr, modr)


def _ffn_kernel(h_ref, x_ref, gt_ref, w1_ref, w3_ref, w2_ref, gn_ref, scn_ref, shn_ref,
                out_a_ref, out_b_ref, acc_ref, *, n_ff, n_ctx_tiles):
    i = pl.program_id(0)
    j = pl.program_id(1)

    @pl.when(j == 0)
    def _():
        acc_ref[...] = jnp.zeros_like(acc_ref)

    h = h_ref[...]
    a = _dot(h, w1_ref[...])
    b = _dot(h, w3_ref[...])
    p = (_silu(a) * b).astype(BF16)
    acc_ref[...] += _dot(p, w2_ref[...])

    @pl.when(j == n_ff - 1)
    def _():
        xn = x_ref[...] + 0.5 * gt_ref[...] * acc_ref[...]
        hn = _norm_mod(xn, gn_ref[...], scn_ref[...], shn_ref[...])
        if n_ctx_tiles is None:
            out_a_ref[...] = xn
            out_b_ref[...] = hn.astype(out_b_ref.dtype)
        else:
            @pl.when(i < n_ctx_tiles)
            def _():
                out_a_ref[...] = hn

            @pl.when(i >= n_ctx_tiles)
            def _():
                out_b_ref[...] = hn


def _ffn(tok, h, x, modr, layer, which, gate_i, w1, w3, w2, next_g, next_sc, next_sh, final):
    tm, tf = TOKEN_TILE, FF_TILE
    n_ff = D_FF // tf
    nct = tok.ctx_tokens // tm
    row = pl.BlockSpec((tm, D_MODEL), lambda i, j: (i, 0))
    if final:
        out_specs = [pl.BlockSpec((tm, D_MODEL), lambda i, j: (jnp.minimum(i, nct - 1), 0)),
                     pl.BlockSpec((tm, D_MODEL), lambda i, j: (jnp.maximum(i - nct, 0), 0))]
        out_shape = [jax.ShapeDtypeStruct((tok.ctx_tokens, D_MODEL), F32),
                     jax.ShapeDtypeStruct((tok.total - tok.ctx_tokens, D_MODEL), F32)]
    else:
        out_specs = [row, row]
        out_shape = [jax.ShapeDtypeStruct((tok.total, D_MODEL), F32),
                     jax.ShapeDtypeStruct((tok.total, D_MODEL), BF16)]
    return pl.pallas_call(
        functools.partial(_ffn_kernel, n_ff=n_ff, n_ctx_tiles=nct if final else None),
        grid=(tok.total // tm, n_ff),
        in_specs=[
            row, row,
            _mod_spec(tok, tm, layer, gate_i, 2),
            pl.BlockSpec((None, None, D_MODEL, tf), lambda i, j: (layer, which, 0, j)),
            pl.BlockSpec((None, None, D_MODEL, tf), lambda i, j: (layer, which, 0, j)),
            pl.BlockSpec((None, None, tf, D_MODEL), lambda i, j: (layer, which, j, 0)),
            pl.BlockSpec((1, D_MODEL), lambda i, j: (0, 0)),
            next_sc[1], next_sh[1],
        ],
        out_specs=out_specs,
        out_shape=out_shape,
        scratch_shapes=[pltpu.VMEM((tm, D_MODEL), F32)],
        compiler_params=_cparams("arbitrary", "arbitrary"),
        name="swiglu_half_step",
    )(h, x, modr, w1, w3, w2, next_g, next_sc[0], next_sh[0])


def _project_kernel(h_ref, w_ref, o_ref, wb_ref):
    @pl.when(pl.program_id(1) == 0)
    def _():
        wb_ref[...] = w_ref[...].astype(BF16)

    o_ref[...] = _dot(h_ref[...], wb_ref[...])


def _project(h, w_in, layer, col0, n, tn):
    t, k = h.shape
    tm = 2 * TOKEN_TILE
    assert t % tm == 0 and n % tn == 0
    return pl.pallas_call(
        _project_kernel,
        grid=(n // tn, t // tm),
        in_specs=[pl.BlockSpec((tm, k), lambda j, i: (i, 0)),
                  pl.BlockSpec((pl.Squeezed(), pl.Element(k), pl.Element(tn)),
                               lambda j, i: (layer, 0, pl.multiple_of(col0 + j * tn, LANES)))],
        out_specs=pl.BlockSpec((tm, tn), lambda j, i: (i, j)),
        out_shape=jax.ShapeDtypeStruct((t, n), F32),
        scratch_shapes=[pltpu.VMEM((k, tn), BF16)],
        compiler_params=_cparams("parallel", "arbitrary"),
        name="mix_in_proj",
    )(h, w_in)


def _gmlp_kernel(h_ref, w_ref, ng_ref, ws_ref, bs_ref, o_ref, wb_ref):
    @pl.when(pl.program_id(0) == 0)
    def _():
        wb_ref[...] = w_ref[...].astype(BF16)

    z = jax.nn.gelu(_dot(h_ref[...], wb_ref[...]))
    for c in range(z.shape[0] // CHUNK_MLP):
        rows = slice(c * CHUNK_MLP, (c + 1) * CHUNK_MLP)
        outs = []
        for g in range(A_GROUPS):
            u = z[rows, g * A_GDIM:(g + 1) * A_GDIM]
            v = z[rows, A_WIDTH + g * A_GDIM:A_WIDTH + (g + 1) * A_GDIM]
            ms = jnp.mean(v * v, axis=-1, keepdims=True)
            v = v * lax.rsqrt(ms + EPS) * ng_ref[g:g + 1, :]
            mixed = _dot(ws_ref[g], v.astype(BF16)) + bs_ref[g]
            outs.append(u * mixed)
        o_ref[rows, :] = jnp.concatenate(outs, axis=-1).astype(o_ref.dtype)


def _gmlp(h, w_in, layer, ng, ws, bs):
    t, k = h.shape
    tm = TOKEN_TILE
    return pl.pallas_call(
        _gmlp_kernel,
        grid=(t // tm,),
        in_specs=[
            pl.BlockSpec((tm, k), lambda i: (i, 0)),
            pl.BlockSpec((pl.Squeezed(), pl.Element(k), pl.Element(A_COLS)), lambda i: (layer, 0, 0)),
            pl.BlockSpec((A_GROUPS, A_GDIM), lambda i: (0, 0)),
            pl.BlockSpec((A_GROUPS, CHUNK_MLP, CHUNK_MLP), lambda i: (0, 0, 0)),
            pl.BlockSpec((A_GROUPS, CHUNK_MLP, 1), lambda i: (0, 0, 0)),
        ],
        out_specs=pl.BlockSpec((tm, A_WIDTH), lambda i: (i, 0)),
        out_shape=jax.ShapeDtypeStruct((t, A_WIDTH), BF16),
        scratch_shapes=[pltpu.VMEM((k, A_COLS), BF16)],
        compiler_params=_cparams("arbitrary"),
        name="gmlp_chunk_mix",
    )(h, w_in, ng, ws.astype(BF16), bs[:, :, None])


def _head_ones():
    r = _iota((LANES, LANES), 0) // B_HDIM
    c = _iota((LANES, LANES), 1) // B_HDIM
    return (r == c).astype(BF16)


def _head_sum(x, parts=2):
    ones = _head_ones()
    outs = []
    for s in range(x.shape[-1] // LANES):
        outs.append(_exact_right(x[:, s * LANES:(s + 1) * LANES], ones, parts))
    return jnp.concatenate(outs, axis=-1)


def _rwkv_prep_kernel(z_ref, zp_ref, zn_ref, mu_ref, w0_ref, w2_ref, a0_ref, a2_ref, g2_ref,
                      kkw_ref, kaw_ref, rk_ref,
                      at_ref, rt_ref, bt_ref, kt_ref, v_ref, gl_ref, gate_ref, bonus_ref,
                      *, n_ctx_tiles, ctx_tiles_per_seq, lat_tiles_per_seq):
    i = pl.program_id(0)
    pos = jnp.where(i < n_ctx_tiles, i % ctx_tiles_per_seq, (i - n_ctx_tiles) % lat_tiles_per_seq)
    last = jnp.where(i < n_ctx_tiles, ctx_tiles_per_seq - 1, lat_tiles_per_seq - 1)
    z = z_ref[...]
    t = z.shape[0]
    row = _iota((t, 1), 0)
    halo_prev = jnp.where(pos != 0, zp_ref[7:8, :], 0.0)
    halo_next = jnp.where(pos != last, zn_ref[0:1, :], 0.0)
    prev = jnp.where(row == 0, halo_prev, pltpu.roll(z, 1, 0))
    nxt = jnp.where(row == t - 1, halo_next, pltpu.roll(z, t - 1, 0))
    z = z + (0.5 * (prev + nxt) - z) * mu_ref[...]

    w = B_WIDTH
    r, k, v = z[:, 0:w], z[:, w:2 * w], z[:, 2 * w:3 * w]
    wd = z[:, 3 * w:3 * w + 2 * LORA]
    ad = z[:, 3 * w + 2 * LORA:3 * w + 4 * LORA]
    gd = z[:, 3 * w + 4 * LORA:]
    twd = jnp.tanh(wd).astype(BF16)
    adb = ad.astype(BF16)

    kk = k * kkw_ref[...]
    nrm = jnp.sqrt(_head_sum(kk * kk))
    kk = kk / jnp.maximum(nrm, 1e-12)

    v_ref[...] = v.astype(BF16)
    gate_ref[...] = _dot(_sigmoid(gd).astype(BF16), g2_ref[...])
    cl = RW_CHUNK
    ri, ci = _iota((t, t), 0), _iota((t, t), 1)
    same_chunk = (ri // cl) == (ci // cl)
    ksum = jnp.zeros_like(k)
    for d in range(2):
        w_raw = w0_ref[d:d + 1, :] + _dot(twd[:, d * LORA:(d + 1) * LORA], w2_ref[d])
        lw = -math.exp(-0.5) * _sigmoid(w_raw)
        a = _sigmoid(a0_ref[d:d + 1, :] + _dot(adb[:, d * LORA:(d + 1) * LORA], a2_ref[d]))
        kd = k * (1.0 + (a - 1.0) * kaw_ref[...])
        ksum = ksum + kd
        upto = same_chunk & ((ci >= ri) if d == 1 else (ci <= ri))
        g = _exact_left(upto.astype(BF16), lw, 2)
        ieg = jnp.exp(-g)
        at_ref[d] = (-kk * jnp.exp(g - lw)).astype(BF16)
        rt_ref[d] = (r * jnp.exp(g)).astype(BF16)
        bt_ref[d] = (kk * a * ieg).astype(BF16)
        kt_ref[d] = (kd * ieg).astype(BF16)
        for c in range(t // cl):
            end = c * cl if d == 1 else (c + 1) * cl - 1
            gl_ref[d, c] = jnp.exp(g[end:end + 1, :])
    bonus_ref[...] = _head_sum(r * ksum * rk_ref[...]) * v


def _rwkv_prep(tok, zb, mu, w0, w2, a0, a2, g2, kkw, kaw, rk):
    tp = PREP_TILE
    n_tiles = tok.total // tp
    rows8 = tok.total // 8
    per = tp // 8
    full = lambda shape: pl.BlockSpec(shape, lambda i: (0,) * len(shape))
    cpt = tp // RW_CHUNK
    tile = pl.BlockSpec((tp, B_WIDTH), lambda i: (i, 0))
    tile2 = pl.BlockSpec((2, tp, B_WIDTH), lambda i: (0, i, 0))
    gl_spec = pl.BlockSpec((2, cpt, 1, B_WIDTH), lambda i: (0, i, 0, 0))
    o1 = jax.ShapeDtypeStruct((tok.total, B_WIDTH), F32)
    o1b = jax.ShapeDtypeStruct((tok.total, B_WIDTH), BF16)
    o2b = jax.ShapeDtypeStruct((2, tok.total, B_WIDTH), BF16)
    ogl = jax.ShapeDtypeStruct((2, tok.total // RW_CHUNK, 1, B_WIDTH), F32)
    kern = functools.partial(_rwkv_prep_kernel, n_ctx_tiles=tok.ctx_tokens // tp,
                             ctx_tiles_per_seq=tok.ctx_len // tp, lat_tiles_per_seq=tok.lat_len // tp)
    return pl.pallas_call(
        kern,
        grid=(n_tiles,),
        in_specs=[
            pl.BlockSpec((tp, B_COLS), lambda i: (i, 0)),
            pl.BlockSpec((8, B_COLS), lambda i: (jnp.maximum(i * per - 1, 0), 0)),
            pl.BlockSpec((8, B_COLS), lambda i: (jnp.minimum((i + 1) * per, rows8 - 1), 0)),
            full((1, B_COLS)), full((2, B_WIDTH)), full((2, LORA, B_WIDTH)), full((2, B_WIDTH)),
            full((2, LORA, B_WIDTH)), full((GATE_LORA, B_WIDTH)),
            full((1, B_WIDTH)), full((1, B_WIDTH)), full((1, B_WIDTH)),
        ],
        out_specs=[tile2, tile2, tile2, tile2, tile, gl_spec, tile, tile],
        out_shape=[o2b, o2b, o2b, o2b, o1b, ogl, o1, o1],
        compiler_params=_cparams("parallel"),
        name="rwkv_prep",
    )(zb, zb, zb, mu, w0, w2.astype(BF16), a0, a2.astype(BF16), g2.astype(BF16), kkw, kaw, rk)


def _pair_blockdiag(x):
    lane = _iota(x.shape, 1)
    zero = jnp.zeros_like(x)
    return jnp.concatenate([jnp.where(lane < B_HDIM, x, zero), jnp.where(lane >= B_HDIM, x, zero)], axis=0)


def _rwkv_chunks(streams, s_ref, y_refs):
    c = streams[0][0].shape[0]
    n_pairs = B_HEADS // 2
    rr, cc = _iota((2 * c, 2 * c), 0), _iota((2 * c, 2 * c), 1)
    same = (rr // c) == (cc // c)
    tt, ss = rr % c, cc % c
    eye = (rr == cc).astype(F32)
    pair_masks = []
    n = 1
    while n < c:
        pair_masks.append(same & ((tt // (2 * n)) == (ss // (2 * n))) & ((tt // n) != (ss // n)))
        n *= 2

    units = []
    for d, (a_t, r_t, b_t, k_t, vb, gl, reverse) in enumerate(streams):
        before = same & ((ss > tt) if reverse else (ss < tt))
        upto = same & ((ss >= tt) if reverse else (ss <= tt))
        for p in range(n_pairs):
            sl = slice(p * LANES, (p + 1) * LANES)
            units.append(dict(d=d, p=p, before=before, upto=upto, gl=gl[:, sl],
                              ar=jnp.concatenate([_pair_blockdiag(a_t[:, sl]), _pair_blockdiag(r_t[:, sl])], 0),
                              bk=jnp.concatenate([_pair_blockdiag(b_t[:, sl]), _pair_blockdiag(k_t[:, sl])], 0),
                              v=_pair_blockdiag(vb[:, sl])))
    m = 2 * c
    for u in units:
        u["s0"] = s_ref[u["d"], u["p"]]
        u["gram"] = _dot_nt(u["ar"], jnp.concatenate([u["bk"], u["s0"].astype(BF16)], axis=0))
    yield
    for u in units:
        u["ws"] = u["gram"][:, 2 * m:]
        g = u["gram"]
        u["lmat"] = jnp.where(u["before"], g[:m, :m], 0.0)
        aak = jnp.where(u["before"], g[:m, m:2 * m], 0.0).astype(BF16)
        u["rbk"] = jnp.concatenate([jnp.where(u["upto"], g[m:, :m], 0.0),
                                    jnp.where(u["upto"], g[m:, m:2 * m], 0.0)], axis=1).astype(BF16)
        u["x"] = u["ws"][:m] + _dot(aak, u["v"])
    yield
    for lvl, pm in enumerate(pair_masks):
        for u in units:
            link = jnp.where(pm, u["lmat"], 0.0)
            if lvl == 0:
                u["tb"] = (eye + link).astype(BF16)
            else:
                u["tmp"] = _dot(link.astype(BF16), u["tb"]).astype(BF16)
        if lvl > 0:
            yield
            for u in units:
                u["tb"] = (u["tb"].astype(F32) + _dot(u["tb"], u["tmp"])).astype(BF16)
            yield
    for u in units:
        u["uv"] = jnp.concatenate([_dot(u["tb"], u["x"].astype(BF16)).astype(BF16), u["v"]], 0)
    yield
    ys = [[None] * n_pairs for _ in streams]
    for u in units:
        y = u["ws"][m:] + _dot(u["rbk"], u["uv"])
        ys[u["d"]][u["p"]] = y[:c] + y[c:]
        s_ref[u["d"], u["p"]] = (u["s0"] + _dot_tn(u["uv"], u["bk"])) * u["gl"]
    for d, row in enumerate(ys):
        y_refs[d][...] = jnp.concatenate(row, axis=-1)


def _scan_table(tok, chunk):
    cols = []
    for s in range(tok.n_ctx + tok.n_lat):
        if s < tok.n_ctx:
            base, nc = s * tok.ctx_len // chunk, tok.ctx_len // chunk
        else:
            base = (tok.ctx_tokens + (s - tok.n_ctx) * tok.lat_len) // chunk
            nc = tok.lat_len // chunk
        for c in range(nc):
            cols.append((base + c, base + nc - 1 - c, int(s < tok.n_ctx), int(c == 0), int(c == nc - 1),
                         max(s - tok.n_ctx, 0), min(s, tok.n_ctx - 1)))
    return jnp.asarray(list(zip(*cols)), dtype=jnp.int32)


def _rwkv_post_kernel(yf_ref, yb_ref, bonus_ref, gate_ref, g_ref, b_ref, o_ref):
    y = yf_ref[...] + yb_ref[...]
    mean = _head_sum(y, 3) * (1.0 / B_HDIM)
    yc = y - mean
    var = _head_sum(yc * yc) * (1.0 / B_HDIM)
    y = yc * lax.rsqrt(var + LNX_EPS) * g_ref[...] + b_ref[...] + bonus_ref[...]
    o_ref[...] = (y * gate_ref[...]).astype(o_ref.dtype)


def _rwkv_post(yf, yb, bonus, gate, lnx_g, lnx_b):
    t = yf.shape[0]
    tm = TOKEN_TILE
    tile = pl.BlockSpec((tm, B_WIDTH), lambda i: (i, 0))
    vec = pl.BlockSpec((1, B_WIDTH), lambda i: (0, 0))
    return pl.pallas_call(
        _rwkv_post_kernel,
        grid=(t // tm,),
        in_specs=[tile, tile, tile, tile, vec, vec],
        out_specs=tile,
        out_shape=jax.ShapeDtypeStruct((t, B_WIDTH), BF16),
        compiler_params=_cparams("parallel"),
        name="rwkv_post",
    )(yf, yb, bonus, gate, lnx_g, lnx_b)


def _hgrn_block(qraw, fraw, v, lb, st_ref, o_ref, reverse):
    n = qraw.shape[0]
    sub = HG_SUB
    fg = lb + (1.0 - lb) * _sigmoid(fraw)
    logf = jnp.log(fg)
    kg = 1.0 - fg
    q = _silu(qraw)
    ri, ci = _iota((n, n), 0), _iota((n, n), 1)
    same = (ri // sub) == (ci // sub)
    upto = same & ((ci >= ri) if reverse else (ci <= ri))
    bcum = _exact_left(upto.astype(BF16), logf, 3)
    half = 8
    hi = _iota((half, 1), 0)
    order = range(n // sub - 1, -1, -1) if reverse else range(n // sub)
    out_rows = [None] * (n // sub)
    for sc in order:
        last = sc * sub if reverse else (sc + 1) * sub - 1
        outs = []
        for h in range(C_HEADS):
            cols = slice(h * C_KDIM, (h + 1) * C_KDIM)
            rows = slice(sc * sub, (sc + 1) * sub)
            qs, ks, vs, bs = q[rows, cols], kg[rows, cols], v[rows, cols], bcum[rows, cols]
            fs = fg[rows, cols]
            b_end = bcum[last:last + 1, cols]
            hq = [jnp.zeros((half, C_KDIM), F32) for _ in range(sub // half)]
            intra = [jnp.zeros((half, C_KDIM), F32) for _ in range(sub // half)]
            for j in (range(sub) if reverse else range(sub - 1, -1, -1)):
                jh = j // half
                hq[jh] = jnp.where(hi == j % half, qs[jh * half:(jh + 1) * half], hq[jh])
                entered = range(jh + 1) if reverse else range(jh, sub // half)
                for hf in entered:
                    att = jnp.sum(hq[hf] * ks[j:j + 1, :], axis=-1, keepdims=True)
                    intra[hf] = intra[hf] + att * vs[j:j + 1, :]
                    hq[hf] = hq[hf] * fs[j:j + 1, :]
            o = _dot_nt((qs * jnp.exp(bs)).astype(BF16), st_ref[h].astype(BF16))
            outs.append(o + jnp.concatenate(intra, axis=0))
            ke = (ks * jnp.exp(b_end - bs)).astype(BF16)
            st_ref[h] = st_ref[h] * jnp.exp(b_end) + _dot_tn(vs.astype(BF16), ke)
            if h % (C_HEADS // 2) == C_HEADS // 2 - 1:
                yield
        out_rows[sc] = jnp.concatenate(outs, axis=-1)
    o_ref[...] = jnp.concatenate(out_rows, axis=0)


def _mix_scan_kernel(tab_ref, atf_ref, rtf_ref, btf_ref, ktf_ref, vf_ref, glf_ref,
                     atb_ref, rtb_ref, btb_ref, ktb_ref, vb_ref, glb_ref, rw0_ref,
                     qf_ref, ff_ref, if_ref, qb_ref, fb_ref, ib_ref, lb_ref, hg0_ref,
                     yf_ref, yb_ref, rwfin_ref, of_ref, ob_ref, hgfin_ref, s_ref, st_ref):
    step = pl.program_id(0)
    hd = B_HDIM

    is_ctx = tab_ref[2, step] == 1
    first = tab_ref[3, step] == 1
    last = tab_ref[4, step] == 1

    @pl.when(first & is_ctx)
    def _():
        s_ref[...] = jnp.zeros_like(s_ref)
        st_ref[...] = jnp.zeros_like(st_ref)

    @pl.when(first & jnp.logical_not(is_ctx))
    def _():
        z = jnp.zeros((hd, hd), F32)
        for d in range(2):
            for p in range(B_HEADS // 2):
                s_ref[d, p] = jnp.concatenate(
                    [jnp.concatenate([rw0_ref[d, 2 * p], z], axis=1),
                     jnp.concatenate([z, rw0_ref[d, 2 * p + 1]], axis=1)], axis=0)
            for h in range(C_HEADS):
                st_ref[d, h] = hg0_ref[d, h].T

    pending = [
        _rwkv_chunks(
            [(atf_ref[...], rtf_ref[...], btf_ref[...], ktf_ref[...], vf_ref[...], glf_ref[...], False),
             (atb_ref[...], rtb_ref[...], btb_ref[...], ktb_ref[...], vb_ref[...], glb_ref[...], True)],
            s_ref, (yf_ref, yb_ref)),
        _hgrn_block(qf_ref[...], ff_ref[...], if_ref[...], lb_ref[0:1, :], st_ref.at[0], of_ref, False),
        _hgrn_block(qb_ref[...], fb_ref[...], ib_ref[...], lb_ref[1:2, :], st_ref.at[1], ob_ref, True),
    ]
    while pending:
        for gen in list(pending):
            if next(gen, pending) is pending:
                pending.remove(gen)

    @pl.when(last & is_ctx)
    def _():
        for d in range(2):
            for p in range(B_HEADS // 2):
                s = s_ref[d, p]
                rwfin_ref[d, 2 * p] = s[:hd, :hd]
                rwfin_ref[d, 2 * p + 1] = s[hd:, hd:]
            for h in range(C_HEADS):
                hgfin_ref[d, h] = st_ref[d, h].T


def _mix_scan(tok, layer, at, rt, bt, kt, v, gl, rw0, zc, lb, hg0):
    cl = RW_CHUNK
    assert HG_BLOCK == cl
    tab = _scan_table(tok, cl)
    fwd = lambda col: (lambda s, t: (t[0, s], col))
    bwd = lambda col: (lambda s, t: (t[1, s], col))
    fwd2 = lambda s, t: (0, t[0, s], 0)
    bwd2 = lambda s, t: (1, t[1, s], 0)
    blk = lambda im: pl.BlockSpec((cl, B_WIDTH), im)
    blk2 = lambda im: pl.BlockSpec((None, cl, B_WIDTH), im)
    glf = pl.BlockSpec((None, None, 1, B_WIDTH), lambda s, t: (0, t[0, s], 0, 0))
    glb = pl.BlockSpec((None, None, 1, B_WIDTH), lambda s, t: (1, t[1, s], 0, 0))
    rw_in = pl.BlockSpec((None, None, 2, B_HEADS, B_HDIM, B_HDIM), lambda s, t: (t[5, s], layer, 0, 0, 0, 0))
    hg_in = pl.BlockSpec((None, None, 2, C_HEADS, C_KDIM, C_KDIM), lambda s, t: (t[5, s], layer, 0, 0, 0, 0))
    rw_out = pl.BlockSpec((None, 2, B_HEADS, B_HDIM, B_HDIM), lambda s, t: (t[6, s], 0, 0, 0, 0))
    hg_out = pl.BlockSpec((None, 2, C_HEADS, C_KDIM, C_KDIM), lambda s, t: (t[6, s], 0, 0, 0, 0))
    cblk = lambda im: pl.BlockSpec((cl, C_WIDTH), im)
    yshape = jax.ShapeDtypeStruct((tok.total, B_WIDTH), F32)
    oshape = jax.ShapeDtypeStruct((tok.total, C_WIDTH), F32)
    return pl.pallas_call(
        _mix_scan_kernel,
        grid_spec=pltpu.PrefetchScalarGridSpec(
            num_scalar_prefetch=1,
            grid=(tab.shape[1],),
            in_specs=[blk2(fwd2), blk2(fwd2), blk2(fwd2), blk2(fwd2), blk(fwd(0)), glf,
                      blk2(bwd2), blk2(bwd2), blk2(bwd2), blk2(bwd2), blk(bwd(0)), glb, rw_in,
                      cblk(fwd(0)), cblk(fwd(1)), cblk(fwd(3)), cblk(bwd(0)), cblk(bwd(2)), cblk(bwd(3)),
                      pl.BlockSpec((2, C_WIDTH), lambda s, t: (0, 0)), hg_in],
            out_specs=[blk(fwd(0)), blk(bwd(0)), rw_out, cblk(fwd(0)), cblk(bwd(0)), hg_out],
            scratch_shapes=[pltpu.VMEM((2, B_HEADS // 2, LANES, LANES), F32),
                            pltpu.VMEM((2, C_HEADS, C_KDIM, C_KDIM), F32)]),
        out_shape=[yshape, yshape, jax.ShapeDtypeStruct((tok.n_ctx,) + rw0.shape[2:], F32),
                   oshape, oshape, jax.ShapeDtypeStruct((tok.n_ctx,) + hg0.shape[2:], F32)],
        compiler_params=_cparams("arbitrary"),
        name="rwkv_hgrn_scan",
    )(tab, at, rt, bt, kt, v, gl, at, rt, bt, kt, v, gl, rw0, zc, zc, zc, zc, zc, zc, lb, hg0)


def _hgrn_post_kernel(of_ref, ob_ref, g_ref, gn_ref, o_ref):
    o = of_ref[...] + ob_ref[...]
    gate = _silu(g_ref[...])
    outs = []
    for h in range(C_HEADS):
        cols = slice(h * C_KDIM, (h + 1) * C_KDIM)
        oh = o[:, cols]
        ms = jnp.mean(oh * oh, axis=-1, keepdims=True)
        outs.append(oh * lax.rsqrt(ms + EPS) * gn_ref[...] * gate[:, cols])
    o_ref[...] = jnp.concatenate(outs, axis=-1).astype(o_ref.dtype)


def _hgrn_post(of, ob, zc, gn):
    t = of.shape[0]
    tm = TOKEN_TILE
    tile = pl.BlockSpec((tm, C_WIDTH), lambda i: (i, 0))
    return pl.pallas_call(
        _hgrn_post_kernel,
        grid=(t // tm,),
        in_specs=[tile, tile, pl.BlockSpec((tm, C_WIDTH), lambda i: (i, 4)),
                  pl.BlockSpec((1, C_KDIM), lambda i: (0, 0))],
        out_specs=tile,
        out_shape=jax.ShapeDtypeStruct((t, C_WIDTH), BF16),
        compiler_params=_cparams("parallel"),
        name="hgrn2_post",
    )(of, ob, zc, gn)


def _mix_out_kernel(oa_ref, ob_ref, oc_ref, wa_ref, wb_ref, wc_ref, x_ref, gt_ref,
                    gn_ref, scn_ref, shn_ref, xo_ref, ho_ref):
    y = _dot(oa_ref[...], wa_ref[...]) + _dot(ob_ref[...], wb_ref[...]) + _dot(oc_ref[...], wc_ref[...])
    xn = x_ref[...] + gt_ref[...] * y
    xo_ref[...] = xn
    ho_ref[...] = _norm_mod(xn, gn_ref[...], scn_ref[...], shn_ref[...]).astype(ho_ref.dtype)


def _mix_out(tok, oa, ob, oc, wa, wb, wc, x, modr, layer, next_g):
    tm = TOKEN_TILE
    row = pl.BlockSpec((tm, D_MODEL), lambda i: (i, 0))
    full = lambda a: pl.BlockSpec(a.shape, lambda i: (0, 0))
    return pl.pallas_call(
        _mix_out_kernel,
        grid=(tok.total // tm,),
        in_specs=[pl.BlockSpec((tm, A_WIDTH), lambda i: (i, 0)),
                  pl.BlockSpec((tm, B_WIDTH), lambda i: (i, 0)),
                  pl.BlockSpec((tm, C_WIDTH), lambda i: (i, 0)),
                  full(wa), full(wb), full(wc), row,
                  _mod_spec(tok, tm, layer, 5, 1),
                  pl.BlockSpec((1, D_MODEL), lambda i: (0, 0)),
                  _mod_spec(tok, tm, layer, 7, 1), _mod_spec(tok, tm, layer, 6, 1)],
        out_specs=[row, row],
        out_shape=[jax.ShapeDtypeStruct((tok.total, D_MODEL), F32),
                   jax.ShapeDtypeStruct((tok.total, D_MODEL), BF16)],
        compiler_params=_cparams("parallel"),
        name="mix_out_proj",
    )(oa, ob, oc, wa, wb, wc, x, modr, next_g, modr, modr)


def kernel(x_prompt, x_sample, state_rwkv, state_hgrn, c, c_ctx, norm_g, w_mod, b_mod, ffn_w1, ffn_w3,
           ffn_w2, w_in, w_out, mlp_norm_g, mlp_ws, mlp_bs, rwkv_mu, rwkv_w0, rwkv_w2, rwkv_a0, rwkv_a2,
           rwkv_g2, rwkv_kk, rwkv_ka, rwkv_rk, rwkv_lnx_g, rwkv_lnx_b, hgrn_lb, hgrn_gn, final_g):
    n_ctx, ctx_len, _ = x_prompt.shape
    n_lat, lat_len, _ = x_sample.shape
    depth = w_mod.shape[0]
    tok = _Tokens(n_ctx, ctx_len, n_lat, lat_len)
    assert n_lat + 1 <= 8 and ctx_len % PREP_TILE == 0 and lat_len % TOKEN_TILE == 0

    cond8 = jnp.zeros((8, D_MODEL), F32).at[0].set(c_ctx).at[1:1 + n_lat].set(c)
    mod = _mod_rows(cond8, w_mod, b_mod)
    modr = mod.reshape(depth, 8, N_MOD, D_MODEL).transpose(0, 2, 1, 3)[:, :, :, None, :]

    sm = jax.nn.softmax(hgrn_lb.astype(F32), axis=0)
    lower = jnp.cumsum(sm, axis=0) - sm[0]

    x = jnp.concatenate([x_prompt.reshape(-1, D_MODEL), x_sample.reshape(-1, D_MODEL)], axis=0)
    zero_sc = jnp.zeros((1, D_MODEL), F32)
    zero_spec = pl.BlockSpec((1, D_MODEL), lambda i, j: (0, 0))
    tm = TOKEN_TILE

    h = _normmod(tok, x, norm_g[0, 0][None], modr, 0, 1, 0)
    bw = lambda a: a.astype(BF16)
    w1b, w3b, w2b = bw(ffn_w1), bw(ffn_w3), bw(ffn_w2)
    rw_states, hg_states = [], []
    for l in range(depth):
        x, h = _ffn(tok, h, x, modr, l, 0, 2, w1b, w3b, w2b,
                    norm_g[l, 1][None], (modr, _mod_spec(tok, tm, l, 4, 2)),
                    (modr, _mod_spec(tok, tm, l, 3, 2)), False)

        zb = _project(h, w_in, l, A_COLS, B_COLS, 896)
        zc = _project(h, w_in, l, A_COLS + B_COLS, C_COLS, 768)

        oa = _gmlp(h, w_in, l, mlp_norm_g[l], mlp_ws[l], mlp_bs[l])

        at, rt, bt, kt, v, gl, gate, bonus = _rwkv_prep(
            tok, zb, rwkv_mu[l][None], rwkv_w0[l], rwkv_w2[l], rwkv_a0[l], rwkv_a2[l], rwkv_g2[l],
            rwkv_kk[l][None], rwkv_ka[l][None], rwkv_rk[l].reshape(1, B_WIDTH))
        yf, yb, s_rw, of, obw, s_hg = _mix_scan(tok, l, at, rt, bt, kt, v, gl, state_rwkv, zc, lower[l],
                                                state_hgrn)
        ob = _rwkv_post(yf, yb, bonus, gate,
                        rwkv_lnx_g[l].reshape(1, B_WIDTH), rwkv_lnx_b[l].reshape(1, B_WIDTH))
        oc = _hgrn_post(of, obw, zc, hgrn_gn[l][None])
        rw_states.append(s_rw)
        hg_states.append(s_hg)

        w_out_l = bw(w_out[l])
        x, h = _mix_out(tok, oa, ob, oc, w_out_l[:A_WIDTH], w_out_l[A_WIDTH:A_WIDTH + B_WIDTH],
                        w_out_l[A_WIDTH + B_WIDTH:], x, modr, l, norm_g[l, 2][None])

        if l + 1 < depth:
            x, h = _ffn(tok, h, x, modr, l, 1, 8, w1b, w3b, w2b,
                        norm_g[l + 1, 0][None], (modr, _mod_spec(tok, tm, l + 1, 1, 2)),
                        (modr, _mod_spec(tok, tm, l + 1, 0, 2)), False)
        else:
            y_prompt, y_sample = _ffn(tok, h, x, modr, l, 1, 8, w1b, w3b, w2b,
                                      final_g[None], (zero_sc, zero_spec), (zero_sc, zero_spec), True)

    return (y_prompt.reshape(x_prompt.shape), y_sample.reshape(x_sample.shape),
            jnp.stack(rw_states, axis=1), jnp.stack(hg_states, axis=1))
```

```python
import functools
import math

import jax
import jax.numpy as jnp
from jax import lax
from jax.experimental import pallas as pl
from jax.experimental.pallas import tpu as pltpu

F32 = jnp.float32
BF16 = jnp.bfloat16

D_MODEL = 2048
D_FF = 5632
N_MOD = 9
EPS = 1e-6
LNX_EPS = 64e-5

CHUNK_MLP = 128
A_WIDTH = 512
A_GROUPS = 4
A_GDIM = 128
B_HDIM = 64
B_WIDTH = 768
B_HEADS = 12
LORA = 64
GATE_LORA = 128
B_COLS = 3 * B_WIDTH + 4 * LORA + GATE_LORA
C_KDIM = 128
C_WIDTH = 768
C_HEADS = 6
C_COLS = 5 * C_WIDTH
A_COLS = 2 * A_WIDTH

LANES = 128
VMEM_LIMIT = 56 * 1024 * 1024

TOKEN_TILE = 512
FF_TILE = 512
PREP_TILE = 256
RW_CHUNK = 64
HG_BLOCK = 64
HG_SUB = 16


def _cparams(*sem):
    return pltpu.CompilerParams(dimension_semantics=sem, vmem_limit_bytes=VMEM_LIMIT)


def _dot(a, b):
    return jnp.dot(a, b, preferred_element_type=F32)


def _dot_nt(a, b):
    return lax.dot_general(a, b, (((1,), (1,)), ((), ())), preferred_element_type=F32)


def _dot_tn(a, b):
    return lax.dot_general(a, b, (((0,), (0,)), ((), ())), preferred_element_type=F32)


def _split(x, parts):
    out = []
    for _ in range(parts - 1):
        hi = x.astype(BF16)
        out.append(hi)
        x = x - hi.astype(F32)
    out.append(x.astype(BF16))
    return out


def _exact_left(m, x, parts):
    acc = None
    for p in _split(x, parts):
        t = _dot(m, p)
        acc = t if acc is None else acc + t
    return acc


def _exact_right(x, m, parts):
    acc = None
    for p in _split(x, parts):
        t = _dot(p, m)
        acc = t if acc is None else acc + t
    return acc


def _iota(shape, dim):
    return lax.broadcasted_iota(jnp.int32, shape, dim)


def _sigmoid(x):
    return jax.nn.sigmoid(x)


def _silu(x):
    return x * jax.nn.sigmoid(x)


def _norm_mod(x, g, sc, sh):
    ms = jnp.mean(x * x, axis=-1, keepdims=True)
    return (x * lax.rsqrt(ms + EPS) * g) * (1.0 + sc) + sh


def _mod_kernel(cond_ref, w_ref, b_ref, o_ref):
    s = _silu(cond_ref[...]).astype(BF16)
    o_ref[...] = _dot(s, w_ref[...].astype(BF16)) + b_ref[...]


def _mod_rows(cond8, w_mod, b_mod):
    depth, _, n = w_mod.shape
    tn = 1024
    return pl.pallas_call(
        _mod_kernel,
        grid=(depth, n // tn),
        in_specs=[
            pl.BlockSpec((8, D_MODEL), lambda l, j: (0, 0)),
            pl.BlockSpec((None, D_MODEL, tn), lambda l, j: (l, 0, j)),
            pl.BlockSpec((None, 1, tn), lambda l, j: (l, 0, j)),
        ],
        out_specs=pl.BlockSpec((None, 8, tn), lambda l, j: (l, 0, j)),
        out_shape=jax.ShapeDtypeStruct((depth, 8, n), F32),
        compiler_params=_cparams("parallel", "parallel"),
        name="adaln_rows",
    )(cond8, w_mod, b_mod.reshape(depth, 1, n))


class _Tokens:
    def __init__(self, n_ctx, ctx_len, n_lat, lat_len):
        self.n_ctx, self.ctx_len, self.n_lat, self.lat_len = n_ctx, ctx_len, n_lat, lat_len
        self.ctx_tokens = n_ctx * ctx_len
        self.total = self.ctx_tokens + n_lat * lat_len

    def mod_row(self, tile, i):
        nct = self.ctx_tokens // tile
        per_lat = self.lat_len // tile
        return jnp.where(i < nct, 0, 1 + (i - nct) // per_lat)


def _mod_spec(tok, tile, layer, which, grid_rank):
    def imap(*idx):
        return (layer, which, tok.mod_row(tile, idx[0]), 0, 0)
    del grid_rank
    return pl.BlockSpec((None, None, None, 1, D_MODEL), imap)


def _normmod_kernel(x_ref, g_ref, sc_ref, sh_ref, h_ref):
    h_ref[...] = _norm_mod(x_ref[...], g_ref[...], sc_ref[...], sh_ref[...]).astype(h_ref.dtype)


def _normmod(tok, x, norm_g_row, modr, layer, sc_i, sh_i):
    tm = TOKEN_TILE
    return pl.pallas_call(
        _normmod_kernel,
        grid=(tok.total // tm,),
        in_specs=[
            pl.BlockSpec((tm, D_MODEL), lambda i: (i, 0)),
            pl.BlockSpec((1, D_MODEL), lambda i: (0, 0)),
            _mod_spec(tok, tm, layer, sc_i, 1),
            _mod_spec(tok, tm, layer, sh_i, 1),
        ],
        out_specs=pl.BlockSpec((tm, D_MODEL), lambda i: (i, 0)),
        out_shape=jax.ShapeDtypeStruct((tok.total, D_MODEL), BF16),
        compiler_params=_cparams("parallel"),
        name="first_norm",
    )(x, norm_g_row, modr, modr)


def _ffn_kernel(h_ref, x_ref, gt_ref, w1_ref, w3_ref, w2_ref, gn_ref, scn_ref, shn_ref,
                out_a_ref, out_b_ref, acc_ref, *, n_ff, n_ctx_tiles):
    i = pl.program_id(0)
    j = pl.program_id(1)

    @pl.when(j == 0)
    def _():
        acc_ref[...] = jnp.zeros_like(acc_ref)

    h = h_ref[...]
    a = _dot(h, w1_ref[...])
    b = _dot(h, w3_ref[...])
    p = (_silu(a) * b).astype(BF16)
    acc_ref[...] += _dot(p, w2_ref[...])

    @pl.when(j == n_ff - 1)
    def _():
        xn = x_ref[...] + 0.5 * gt_ref[...] * acc_ref[...]
        hn = _norm_mod(xn, gn_ref[...], scn_ref[...], shn_ref[...])
        if n_ctx_tiles is None:
            out_a_ref[...] = xn
            out_b_ref[...] = hn.astype(out_b_ref.dtype)
        else:
            @pl.when(i < n_ctx_tiles)
            def _():
                out_a_ref[...] = hn

            @pl.when(i >= n_ctx_tiles)
            def _():
                out_b_ref[...] = hn


def _ffn(tok, h, x, modr, layer, gate_i, w1, w3, w2, next_g, next_sc, next_sh, final):
    tm, tf = TOKEN_TILE, FF_TILE
    n_ff = D_FF // tf
    nct = tok.ctx_tokens // tm
    row = pl.BlockSpec((tm, D_MODEL), lambda i, j: (i, 0))
    if final:
        out_specs = [pl.BlockSpec((tm, D_MODEL), lambda i, j: (jnp.minimum(i, nct - 1), 0)),
                     pl.BlockSpec((tm, D_MODEL), lambda i, j: (jnp.maximum(i - nct, 0), 0))]
        out_shape = [jax.ShapeDtypeStruct((tok.ctx_tokens, D_MODEL), F32),
                     jax.ShapeDtypeStruct((tok.total - tok.ctx_tokens, D_MODEL), F32)]
    else:
        out_specs = [row, row]
        out_shape = [jax.ShapeDtypeStruct((tok.total, D_MODEL), F32),
                     jax.ShapeDtypeStruct((tok.total, D_MODEL), BF16)]
    return pl.pallas_call(
        functools.partial(_ffn_kernel, n_ff=n_ff, n_ctx_tiles=nct if final else None),
        grid=(tok.total // tm, n_ff),
        in_specs=[
            row, row,
            _mod_spec(tok, tm, layer, gate_i, 2),
            pl.BlockSpec((D_MODEL, tf), lambda i, j: (0, j)),
            pl.BlockSpec((D_MODEL, tf), lambda i, j: (0, j)),
            pl.BlockSpec((tf, D_MODEL), lambda i, j: (j, 0)),
            pl.BlockSpec((1, D_MODEL), lambda i, j: (0, 0)),
            next_sc[1], next_sh[1],
        ],
        out_specs=out_specs,
        out_shape=out_shape,
        scratch_shapes=[pltpu.VMEM((tm, D_MODEL), F32)],
        compiler_params=_cparams("arbitrary", "arbitrary"),
        name="swiglu_half_step",
    )(h, x, modr, w1, w3, w2, next_g, next_sc[0], next_sh[0])


def _project_kernel(h_ref, w_ref, o_ref, wb_ref):
    @pl.when(pl.program_id(1) == 0)
    def _():
        wb_ref[...] = w_ref[...].astype(BF16)

    o_ref[...] = _dot(h_ref[...], wb_ref[...])


def _project(h, w_in, layer, col0, n, tn):
    t, k = h.shape
    tm = 2 * TOKEN_TILE
    assert t % tm == 0 and n % tn == 0
    return pl.pallas_call(
        _project_kernel,
        grid=(n // tn, t // tm),
        in_specs=[pl.BlockSpec((tm, k), lambda j, i: (i, 0)),
                  pl.BlockSpec((pl.Squeezed(), pl.Element(k), pl.Element(tn)),
                               lambda j, i: (layer, 0, pl.multiple_of(col0 + j * tn, LANES)))],
        out_specs=pl.BlockSpec((tm, tn), lambda j, i: (i, j)),
        out_shape=jax.ShapeDtypeStruct((t, n), F32),
        scratch_shapes=[pltpu.VMEM((k, tn), BF16)],
        compiler_params=_cparams("parallel", "arbitrary"),
        name="mix_in_proj",
    )(h, w_in)


def _gmlp_kernel(h_ref, w_ref, ng_ref, ws_ref, bs_ref, o_ref, wb_ref):
    @pl.when(pl.program_id(0) == 0)
    def _():
        wb_ref[...] = w_ref[...].astype(BF16)

    z = jax.nn.gelu(_dot(h_ref[...], wb_ref[...]))
    for c in range(z.shape[0] // CHUNK_MLP):
        rows = slice(c * CHUNK_MLP, (c + 1) * CHUNK_MLP)
        outs = []
        for g in range(A_GROUPS):
            u = z[rows, g * A_GDIM:(g + 1) * A_GDIM]
            v = z[rows, A_WIDTH + g * A_GDIM:A_WIDTH + (g + 1) * A_GDIM]
            ms = jnp.mean(v * v, axis=-1, keepdims=True)
            v = v * lax.rsqrt(ms + EPS) * ng_ref[g:g + 1, :]
            mixed = _dot(ws_ref[g], v.astype(BF16)) + bs_ref[g]
            outs.append(u * mixed)
        o_ref[rows, :] = jnp.concatenate(outs, axis=-1).astype(o_ref.dtype)


def _gmlp(h, w_in, layer, ng, ws, bs):
    t, k = h.shape
    tm = TOKEN_TILE
    return pl.pallas_call(
        _gmlp_kernel,
        grid=(t // tm,),
        in_specs=[
            pl.BlockSpec((tm, k), lambda i: (i, 0)),
            pl.BlockSpec((pl.Squeezed(), pl.Element(k), pl.Element(A_COLS)), lambda i: (layer, 0, 0)),
            pl.BlockSpec((A_GROUPS, A_GDIM), lambda i: (0, 0)),
            pl.BlockSpec((A_GROUPS, CHUNK_MLP, CHUNK_MLP), lambda i: (0, 0, 0)),
            pl.BlockSpec((A_GROUPS, CHUNK_MLP, 1), lambda i: (0, 0, 0)),
        ],
        out_specs=pl.BlockSpec((tm, A_WIDTH), lambda i: (i, 0)),
        out_shape=jax.ShapeDtypeStruct((t, A_WIDTH), BF16),
        scratch_shapes=[pltpu.VMEM((k, A_COLS), BF16)],
        compiler_params=_cparams("arbitrary"),
        name="gmlp_chunk_mix",
    )(h, w_in, ng, ws.astype(BF16), bs[:, :, None])


def _head_ones():
    r = _iota((LANES, LANES), 0) // B_HDIM
    c = _iota((LANES, LANES), 1) // B_HDIM
    return (r == c).astype(BF16)


def _head_sum(x, parts=2):
    ones = _head_ones()
    outs = []
    for s in range(x.shape[-1] // LANES):
        outs.append(_exact_right(x[:, s * LANES:(s + 1) * LANES], ones, parts))
    return jnp.concatenate(outs, axis=-1)


def _rwkv_prep_kernel(z_ref, zp_ref, zn_ref, mu_ref, w0_ref, w2_ref, a0_ref, a2_ref, g2_ref,
                      kkw_ref, kaw_ref, rk_ref,
                      at_ref, rt_ref, bt_ref, kt_ref, v_ref, gl_ref, gate_ref, bonus_ref,
                      *, n_ctx_tiles, ctx_tiles_per_seq, lat_tiles_per_seq):
    i = pl.program_id(0)
    pos = jnp.where(i < n_ctx_tiles, i % ctx_tiles_per_seq, (i - n_ctx_tiles) % lat_tiles_per_seq)
    last = jnp.where(i < n_ctx_tiles, ctx_tiles_per_seq - 1, lat_tiles_per_seq - 1)
    z = z_ref[...]
    t = z.shape[0]
    row = _iota((t, 1), 0)
    halo_prev = jnp.where(pos != 0, zp_ref[7:8, :], 0.0)
    halo_next = jnp.where(pos != last, zn_ref[0:1, :], 0.0)
    prev = jnp.where(row == 0, halo_prev, pltpu.roll(z, 1, 0))
    nxt = jnp.where(row == t - 1, halo_next, pltpu.roll(z, t - 1, 0))
    z = z + (0.5 * (prev + nxt) - z) * mu_ref[...]

    w = B_WIDTH
    r, k, v = z[:, 0:w], z[:, w:2 * w], z[:, 2 * w:3 * w]
    wd = z[:, 3 * w:3 * w + 2 * LORA]
    ad = z[:, 3 * w + 2 * LORA:3 * w + 4 * LORA]
    gd = z[:, 3 * w + 4 * LORA:]
    twd = jnp.tanh(wd).astype(BF16)
    adb = ad.astype(BF16)

    kk = k * kkw_ref[...]
    nrm = jnp.sqrt(_head_sum(kk * kk))
    kk = kk / jnp.maximum(nrm, 1e-12)

    v_ref[...] = v.astype(BF16)
    gate_ref[...] = _dot(_sigmoid(gd).astype(BF16), g2_ref[...])
    cl = RW_CHUNK
    ri, ci = _iota((t, t), 0), _iota((t, t), 1)
    same_chunk = (ri // cl) == (ci // cl)
    ksum = jnp.zeros_like(k)
    for d in range(2):
        w_raw = w0_ref[d:d + 1, :] + _dot(twd[:, d * LORA:(d + 1) * LORA], w2_ref[d])
        lw = -math.exp(-0.5) * _sigmoid(w_raw)
        a = _sigmoid(a0_ref[d:d + 1, :] + _dot(adb[:, d * LORA:(d + 1) * LORA], a2_ref[d]))
        kd = k * (1.0 + (a - 1.0) * kaw_ref[...])
        ksum = ksum + kd
        upto = same_chunk & ((ci >= ri) if d == 1 else (ci <= ri))
        g = _exact_left(upto.astype(BF16), lw, 2)
        ieg = jnp.exp(-g)
        at_ref[d] = (-kk * jnp.exp(g - lw)).astype(BF16)
        rt_ref[d] = (r * jnp.exp(g)).astype(BF16)
        bt_ref[d] = (kk * a * ieg).astype(BF16)
        kt_ref[d] = (kd * ieg).astype(BF16)
        for c in range(t // cl):
            end = c * cl if d == 1 else (c + 1) * cl - 1
            gl_ref[d, c] = jnp.exp(g[end:end + 1, :])
    bonus_ref[...] = _head_sum(r * ksum * rk_ref[...]) * v


def _rwkv_prep(tok, zb, mu, w0, w2, a0, a2, g2, kkw, kaw, rk):
    tp = PREP_TILE
    n_tiles = tok.total // tp
    rows8 = tok.total // 8
    per = tp // 8
    full = lambda shape: pl.BlockSpec(shape, lambda i: (0,) * len(shape))
    cpt = tp // RW_CHUNK
    tile = pl.BlockSpec((tp, B_WIDTH), lambda i: (i, 0))
    tile2 = pl.BlockSpec((2, tp, B_WIDTH), lambda i: (0, i, 0))
    gl_spec = pl.BlockSpec((2, cpt, 1, B_WIDTH), lambda i: (0, i, 0, 0))
    o1 = jax.ShapeDtypeStruct((tok.total, B_WIDTH), F32)
    o1b = jax.ShapeDtypeStruct((tok.total, B_WIDTH), BF16)
    o2b = jax.ShapeDtypeStruct((2, tok.total, B_WIDTH), BF16)
    ogl = jax.ShapeDtypeStruct((2, tok.total // RW_CHUNK, 1, B_WIDTH), F32)
    kern = functools.partial(_rwkv_prep_kernel, n_ctx_tiles=tok.ctx_tokens // tp,
                             ctx_tiles_per_seq=tok.ctx_len // tp, lat_tiles_per_seq=tok.lat_len // tp)
    return pl.pallas_call(
        kern,
        grid=(n_tiles,),
        in_specs=[
            pl.BlockSpec((tp, B_COLS), lambda i: (i, 0)),
            pl.BlockSpec((8, B_COLS), lambda i: (jnp.maximum(i * per - 1, 0), 0)),
            pl.BlockSpec((8, B_COLS), lambda i: (jnp.minimum((i + 1) * per, rows8 - 1), 0)),
            full((1, B_COLS)), full((2, B_WIDTH)), full((2, LORA, B_WIDTH)), full((2, B_WIDTH)),
            full((2, LORA, B_WIDTH)), full((GATE_LORA, B_WIDTH)),
            full((1, B_WIDTH)), full((1, B_WIDTH)), full((1, B_WIDTH)),
        ],
        out_specs=[tile2, tile2, tile2, tile2, tile, gl_spec, tile, tile],
        out_shape=[o2b, o2b, o2b, o2b, o1b, ogl, o1, o1],
        compiler_params=_cparams("parallel"),
        name="rwkv_prep",
    )(zb, zb, zb, mu, w0, w2.astype(BF16), a0, a2.astype(BF16), g2.astype(BF16), kkw, kaw, rk)


def _pair_blockdiag(x):
    lane = _iota(x.shape, 1)
    zero = jnp.zeros_like(x)
    return jnp.concatenate([jnp.where(lane < B_HDIM, x, zero), jnp.where(lane >= B_HDIM, x, zero)], axis=0)


def _rwkv_chunks(streams, s_ref, y_refs):
    c = streams[0][0].shape[0]
    n_pairs = B_HEADS // 2
    rr, cc = _iota((2 * c, 2 * c), 0), _iota((2 * c, 2 * c), 1)
    same = (rr // c) == (cc // c)
    tt, ss = rr % c, cc % c
    eye = (rr == cc).astype(F32)
    pair_masks = []
    n = 1
    while n < c:
        pair_masks.append(same & ((tt // (2 * n)) == (ss // (2 * n))) & ((tt // n) != (ss // n)))
        n *= 2

    units = []
    for d, (a_t, r_t, b_t, k_t, vb, gl, reverse) in enumerate(streams):
        before = same & ((ss > tt) if reverse else (ss < tt))
        upto = same & ((ss >= tt) if reverse else (ss <= tt))
        for p in range(n_pairs):
            sl = slice(p * LANES, (p + 1) * LANES)
            units.append(dict(d=d, p=p, before=before, upto=upto, gl=gl[:, sl],
                              ar=jnp.concatenate([_pair_blockdiag(a_t[:, sl]), _pair_blockdiag(r_t[:, sl])], 0),
                              bk=jnp.concatenate([_pair_blockdiag(b_t[:, sl]), _pair_blockdiag(k_t[:, sl])], 0),
                              v=_pair_blockdiag(vb[:, sl])))
    m = 2 * c
    for u in units:
        u["s0"] = s_ref[u["d"], u["p"]]
        u["gram"] = _dot_nt(u["ar"], jnp.concatenate([u["bk"], u["s0"].astype(BF16)], axis=0))
    yield
    for u in units:
        u["ws"] = u["gram"][:, 2 * m:]
        g = u["gram"]
        u["lmat"] = jnp.where(u["before"], g[:m, :m], 0.0)
        aak = jnp.where(u["before"], g[:m, m:2 * m], 0.0).astype(BF16)
        u["rbk"] = jnp.concatenate([jnp.where(u["upto"], g[m:, :m], 0.0),
                                    jnp.where(u["upto"], g[m:, m:2 * m], 0.0)], axis=1).astype(BF16)
        u["x"] = u["ws"][:m] + _dot(aak, u["v"])
    yield
    for lvl, pm in enumerate(pair_masks):
        for u in units:
            link = jnp.where(pm, u["lmat"], 0.0)
            if lvl == 0:
                u["tb"] = (eye + link).astype(BF16)
            else:
                u["tmp"] = _dot(link.astype(BF16), u["tb"]).astype(BF16)
        if lvl > 0:
            yield
            for u in units:
                u["tb"] = (u["tb"].astype(F32) + _dot(u["tb"], u["tmp"])).astype(BF16)
            yield
    for u in units:
        u["uv"] = jnp.concatenate([_dot(u["tb"], u["x"].astype(BF16)).astype(BF16), u["v"]], 0)
    yield
    ys = [[None] * n_pairs for _ in streams]
    for u in units:
        y = u["ws"][m:] + _dot(u["rbk"], u["uv"])
        ys[u["d"]][u["p"]] = y[:c] + y[c:]
        s_ref[u["d"], u["p"]] = (u["s0"] + _dot_tn(u["uv"], u["bk"])) * u["gl"]
    for d, row in enumerate(ys):
        y_refs[d][...] = jnp.concatenate(row, axis=-1)


def _scan_table(tok, chunk):
    cols = []
    for s in range(tok.n_ctx + tok.n_lat):
        if s < tok.n_ctx:
            base, nc = s * tok.ctx_len // chunk, tok.ctx_len // chunk
        else:
            base = (tok.ctx_tokens + (s - tok.n_ctx) * tok.lat_len) // chunk
            nc = tok.lat_len // chunk
        for c in range(nc):
            cols.append((base + c, base + nc - 1 - c, int(s < tok.n_ctx), int(c == 0), int(c == nc - 1),
                         max(s - tok.n_ctx, 0), min(s, tok.n_ctx - 1)))
    return jnp.asarray(list(zip(*cols)), dtype=jnp.int32)


def _rwkv_post_kernel(yf_ref, yb_ref, bonus_ref, gate_ref, g_ref, b_ref, o_ref):
    y = yf_ref[...] + yb_ref[...]
    mean = _head_sum(y, 3) * (1.0 / B_HDIM)
    yc = y - mean
    var = _head_sum(yc * yc) * (1.0 / B_HDIM)
    y = yc * lax.rsqrt(var + LNX_EPS) * g_ref[...] + b_ref[...] + bonus_ref[...]
    o_ref[...] = (y * gate_ref[...]).astype(o_ref.dtype)


def _rwkv_post(yf, yb, bonus, gate, lnx_g, lnx_b):
    t = yf.shape[0]
    tm = TOKEN_TILE
    tile = pl.BlockSpec((tm, B_WIDTH), lambda i: (i, 0))
    vec = pl.BlockSpec((1, B_WIDTH), lambda i: (0, 0))
    return pl.pallas_call(
        _rwkv_post_kernel,
        grid=(t // tm,),
        in_specs=[tile, tile, tile, tile, vec, vec],
        out_specs=tile,
        out_shape=jax.ShapeDtypeStruct((t, B_WIDTH), BF16),
        compiler_params=_cparams("parallel"),
        name="rwkv_post",
    )(yf, yb, bonus, gate, lnx_g, lnx_b)


def _hgrn_block(qraw, fraw, v, lb, st_ref, o_ref, reverse):
    n = qraw.shape[0]
    sub = HG_SUB
    fg = lb + (1.0 - lb) * _sigmoid(fraw)
    logf = jnp.log(fg)
    kg = 1.0 - fg
    q = _silu(qraw)
    ri, ci = _iota((n, n), 0), _iota((n, n), 1)
    same = (ri // sub) == (ci // sub)
    upto = same & ((ci >= ri) if reverse else (ci <= ri))
    bcum = _exact_left(upto.astype(BF16), logf, 3)
    half = 8
    hi = _iota((half, 1), 0)
    order = range(n // sub - 1, -1, -1) if reverse else range(n // sub)
    out_rows = [None] * (n // sub)
    for sc in order:
        last = sc * sub if reverse else (sc + 1) * sub - 1
        outs = []
        for h in range(C_HEADS):
            cols = slice(h * C_KDIM, (h + 1) * C_KDIM)
            rows = slice(sc * sub, (sc + 1) * sub)
            qs, ks, vs, bs = q[rows, cols], kg[rows, cols], v[rows, cols], bcum[rows, cols]
            fs = fg[rows, cols]
            b_end = bcum[last:last + 1, cols]
            hq = [jnp.zeros((half, C_KDIM), F32) for _ in range(sub // half)]
            intra = [jnp.zeros((half, C_KDIM), F32) for _ in range(sub // half)]
            for j in (range(sub) if reverse else range(sub - 1, -1, -1)):
                jh = j // half
                hq[jh] = jnp.where(hi == j % half, qs[jh * half:(jh + 1) * half], hq[jh])
                entered = range(jh + 1) if reverse else range(jh, sub // half)
                for hf in entered:
                    att = jnp.sum(hq[hf] * ks[j:j + 1, :], axis=-1, keepdims=True)
                    intra[hf] = intra[hf] + att * vs[j:j + 1, :]
                    hq[hf] = hq[hf] * fs[j:j + 1, :]
            o = _dot_nt((qs * jnp.exp(bs)).astype(BF16), st_ref[h].astype(BF16))
            outs.append(o + jnp.concatenate(intra, axis=0))
            ke = (ks * jnp.exp(b_end - bs)).astype(BF16)
            st_ref[h] = st_ref[h] * jnp.exp(b_end) + _dot_tn(vs.astype(BF16), ke)
            if h % (C_HEADS // 2) == C_HEADS // 2 - 1:
                yield
        out_rows[sc] = jnp.concatenate(outs, axis=-1)
    o_ref[...] = jnp.concatenate(out_rows, axis=0)


def _mix_scan_kernel(tab_ref, *refs, n_cast):
    (atf_ref, rtf_ref, btf_ref, ktf_ref, vf_ref, glf_ref,
     atb_ref, rtb_ref, btb_ref, ktb_ref, vb_ref, glb_ref, rw0_ref,
     qf_ref, ff_ref, if_ref, qb_ref, fb_ref, ib_ref, lb_ref, hg0_ref) = refs[:21]
    cast_in = refs[21:21 + n_cast]
    yf_ref, yb_ref, rwfin_ref, of_ref, ob_ref, hgfin_ref = refs[21 + n_cast:27 + n_cast]
    cast_out = refs[27 + n_cast:27 + 2 * n_cast]
    s_ref, st_ref = refs[27 + 2 * n_cast:]
    step = pl.program_id(0)
    hd = B_HDIM
    for src_ref, dst_ref in zip(cast_in, cast_out):
        dst_ref[...] = src_ref[...].astype(BF16)

    is_ctx = tab_ref[2, step] == 1
    first = tab_ref[3, step] == 1
    last = tab_ref[4, step] == 1

    @pl.when(first & is_ctx)
    def _():
        s_ref[...] = jnp.zeros_like(s_ref)
        st_ref[...] = jnp.zeros_like(st_ref)

    @pl.when(first & jnp.logical_not(is_ctx))
    def _():
        z = jnp.zeros((hd, hd), F32)
        for d in range(2):
            for p in range(B_HEADS // 2):
                s_ref[d, p] = jnp.concatenate(
                    [jnp.concatenate([rw0_ref[d, 2 * p], z], axis=1),
                     jnp.concatenate([z, rw0_ref[d, 2 * p + 1]], axis=1)], axis=0)
            for h in range(C_HEADS):
                st_ref[d, h] = hg0_ref[d, h].T

    pending = [
        _rwkv_chunks(
            [(atf_ref[...], rtf_ref[...], btf_ref[...], ktf_ref[...], vf_ref[...], glf_ref[...], False),
             (atb_ref[...], rtb_ref[...], btb_ref[...], ktb_ref[...], vb_ref[...], glb_ref[...], True)],
            s_ref, (yf_ref, yb_ref)),
        _hgrn_block(qf_ref[...], ff_ref[...], if_ref[...], lb_ref[0:1, :], st_ref.at[0], of_ref, False),
        _hgrn_block(qb_ref[...], fb_ref[...], ib_ref[...], lb_ref[1:2, :], st_ref.at[1], ob_ref, True),
    ]
    while pending:
        for gen in list(pending):
            if next(gen, pending) is pending:
                pending.remove(gen)

    @pl.when(last & is_ctx)
    def _():
        for d in range(2):
            for p in range(B_HEADS // 2):
                s = s_ref[d, p]
                rwfin_ref[d, 2 * p] = s[:hd, :hd]
                rwfin_ref[d, 2 * p + 1] = s[hd:, hd:]
            for h in range(C_HEADS):
                hgfin_ref[d, h] = st_ref[d, h].T


CAST_LANES = 1024


def _mix_scan(tok, layer, at, rt, bt, kt, v, gl, rw0, zc, lb, hg0, to_cast):
    cl = RW_CHUNK
    assert HG_BLOCK == cl
    tab = _scan_table(tok, cl)
    fwd = lambda col: (lambda s, t: (t[0, s], col))
    bwd = lambda col: (lambda s, t: (t[1, s], col))
    fwd2 = lambda s, t: (0, t[0, s], 0)
    bwd2 = lambda s, t: (1, t[1, s], 0)
    blk = lambda im: pl.BlockSpec((cl, B_WIDTH), im)
    blk2 = lambda im: pl.BlockSpec((None, cl, B_WIDTH), im)
    glf = pl.BlockSpec((None, None, 1, B_WIDTH), lambda s, t: (0, t[0, s], 0, 0))
    glb = pl.BlockSpec((None, None, 1, B_WIDTH), lambda s, t: (1, t[1, s], 0, 0))
    rw_in = pl.BlockSpec((None, None, 2, B_HEADS, B_HDIM, B_HDIM), lambda s, t: (t[5, s], layer, 0, 0, 0, 0))
    hg_in = pl.BlockSpec((None, None, 2, C_HEADS, C_KDIM, C_KDIM), lambda s, t: (t[5, s], layer, 0, 0, 0, 0))
    rw_out = pl.BlockSpec((None, 2, B_HEADS, B_HDIM, B_HDIM), lambda s, t: (t[6, s], 0, 0, 0, 0))
    hg_out = pl.BlockSpec((None, 2, C_HEADS, C_KDIM, C_KDIM), lambda s, t: (t[6, s], 0, 0, 0, 0))
    cblk = lambda im: pl.BlockSpec((cl, C_WIDTH), im)
    yshape = jax.ShapeDtypeStruct((tok.total, B_WIDTH), F32)
    oshape = jax.ShapeDtypeStruct((tok.total, C_WIDTH), F32)
    cast_ops, cast_in, cast_out, cast_shapes = [], [], [], []
    for w, wl, wk in to_cast:
        rows = w.shape[2] * w.shape[3] // CAST_LANES
        tile_rows = next(r for r in range(16, rows + 1, 16) if rows % r == 0 and rows // r <= tab.shape[1])
        n_tiles = rows // tile_rows
        tile = (tile_rows, CAST_LANES)
        cast_ops.append(w.reshape(w.shape[:2] + (n_tiles,) + tile))
        cast_in.append(pl.BlockSpec((None, None, None) + tile,
                                    lambda s, t, wl=wl, wk=wk, n=n_tiles: (wl, wk, jnp.minimum(s, n - 1), 0, 0)))
        cast_out.append(pl.BlockSpec((None,) + tile, lambda s, t, n=n_tiles: (jnp.minimum(s, n - 1), 0, 0)))
        cast_shapes.append(jax.ShapeDtypeStruct((n_tiles,) + tile, BF16))
    outs = pl.pallas_call(
        functools.partial(_mix_scan_kernel, n_cast=len(to_cast)),
        grid_spec=pltpu.PrefetchScalarGridSpec(
            num_scalar_prefetch=1,
            grid=(tab.shape[1],),
            in_specs=[blk2(fwd2), blk2(fwd2), blk2(fwd2), blk2(fwd2), blk(fwd(0)), glf,
                      blk2(bwd2), blk2(bwd2), blk2(bwd2), blk2(bwd2), blk(bwd(0)), glb, rw_in,
                      cblk(fwd(0)), cblk(fwd(1)), cblk(fwd(3)), cblk(bwd(0)), cblk(bwd(2)), cblk(bwd(3)),
                      pl.BlockSpec((2, C_WIDTH), lambda s, t: (0, 0)), hg_in] + cast_in,
            out_specs=[blk(fwd(0)), blk(bwd(0)), rw_out, cblk(fwd(0)), cblk(bwd(0)), hg_out] + cast_out,
            scratch_shapes=[pltpu.VMEM((2, B_HEADS // 2, LANES, LANES), F32),
                            pltpu.VMEM((2, C_HEADS, C_KDIM, C_KDIM), F32)]),
        out_shape=[yshape, yshape, jax.ShapeDtypeStruct((tok.n_ctx,) + rw0.shape[2:], F32),
                   oshape, oshape, jax.ShapeDtypeStruct((tok.n_ctx,) + hg0.shape[2:], F32)] + cast_shapes,
        compiler_params=_cparams("arbitrary"),
        name="rwkv_hgrn_scan",
    )(tab, at, rt, bt, kt, v, gl, at, rt, bt, kt, v, gl, rw0, zc, zc, zc, zc, zc, zc, lb, hg0, *cast_ops)
    casted = [o.reshape(w.shape[2:]) for o, (w, _, _) in zip(outs[6:], to_cast)]
    return tuple(outs[:6]) + (casted,)


def _hgrn_post_kernel(of_ref, ob_ref, g_ref, gn_ref, o_ref):
    o = of_ref[...] + ob_ref[...]
    gate = _silu(g_ref[...])
    outs = []
    for h in range(C_HEADS):
        cols = slice(h * C_KDIM, (h + 1) * C_KDIM)
        oh = o[:, cols]
        ms = jnp.mean(oh * oh, axis=-1, keepdims=True)
        outs.append(oh * lax.rsqrt(ms + EPS) * gn_ref[...] * gate[:, cols])
    o_ref[...] = jnp.concatenate(outs, axis=-1).astype(o_ref.dtype)


def _hgrn_post(of, ob, zc, gn):
    t = of.shape[0]
    tm = TOKEN_TILE
    tile = pl.BlockSpec((tm, C_WIDTH), lambda i: (i, 0))
    return pl.pallas_call(
        _hgrn_post_kernel,
        grid=(t // tm,),
        in_specs=[tile, tile, pl.BlockSpec((tm, C_WIDTH), lambda i: (i, 4)),
                  pl.BlockSpec((1, C_KDIM), lambda i: (0, 0))],
        out_specs=tile,
        out_shape=jax.ShapeDtypeStruct((t, C_WIDTH), BF16),
        compiler_params=_cparams("parallel"),
        name="hgrn2_post",
    )(of, ob, zc, gn)


def _mix_out_kernel(oa_ref, ob_ref, oc_ref, wa_ref, wb_ref, wc_ref, x_ref, gt_ref,
                    gn_ref, scn_ref, shn_ref, xo_ref, ho_ref):
    y = _dot(oa_ref[...], wa_ref[...]) + _dot(ob_ref[...], wb_ref[...]) + _dot(oc_ref[...], wc_ref[...])
    xn = x_ref[...] + gt_ref[...] * y
    xo_ref[...] = xn
    ho_ref[...] = _norm_mod(xn, gn_ref[...], scn_ref[...], shn_ref[...]).astype(ho_ref.dtype)


def _mix_out(tok, oa, ob, oc, wa, wb, wc, x, modr, layer, next_g):
    tm = TOKEN_TILE
    row = pl.BlockSpec((tm, D_MODEL), lambda i: (i, 0))
    full = lambda a: pl.BlockSpec(a.shape, lambda i: (0, 0))
    return pl.pallas_call(
        _mix_out_kernel,
        grid=(tok.total // tm,),
        in_specs=[pl.BlockSpec((tm, A_WIDTH), lambda i: (i, 0)),
                  pl.BlockSpec((tm, B_WIDTH), lambda i: (i, 0)),
                  pl.BlockSpec((tm, C_WIDTH), lambda i: (i, 0)),
                  full(wa), full(wb), full(wc), row,
                  _mod_spec(tok, tm, layer, 5, 1),
                  pl.BlockSpec((1, D_MODEL), lambda i: (0, 0)),
                  _mod_spec(tok, tm, layer, 7, 1), _mod_spec(tok, tm, layer, 6, 1)],
        out_specs=[row, row],
        out_shape=[jax.ShapeDtypeStruct((tok.total, D_MODEL), F32),
                   jax.ShapeDtypeStruct((tok.total, D_MODEL), BF16)],
        compiler_params=_cparams("parallel"),
        name="mix_out_proj",
    )(oa, ob, oc, wa, wb, wc, x, modr, next_g, modr, modr)


def kernel(x_prompt, x_sample, state_rwkv, state_hgrn, c, c_ctx, norm_g, w_mod, b_mod, ffn_w1, ffn_w3,
           ffn_w2, w_in, w_out, mlp_norm_g, mlp_ws, mlp_bs, rwkv_mu, rwkv_w0, rwkv_w2, rwkv_a0, rwkv_a2,
           rwkv_g2, rwkv_kk, rwkv_ka, rwkv_rk, rwkv_lnx_g, rwkv_lnx_b, hgrn_lb, hgrn_gn, final_g):
    n_ctx, ctx_len, _ = x_prompt.shape
    n_lat, lat_len, _ = x_sample.shape
    depth = w_mod.shape[0]
    tok = _Tokens(n_ctx, ctx_len, n_lat, lat_len)
    assert n_lat + 1 <= 8 and ctx_len % PREP_TILE == 0 and lat_len % TOKEN_TILE == 0

    cond8 = jnp.zeros((8, D_MODEL), F32).at[0].set(c_ctx).at[1:1 + n_lat].set(c)
    mod = _mod_rows(cond8, w_mod, b_mod)
    modr = mod.reshape(depth, 8, N_MOD, D_MODEL).transpose(0, 2, 1, 3)[:, :, :, None, :]

    sm = jax.nn.softmax(hgrn_lb.astype(F32), axis=0)
    lower = jnp.cumsum(sm, axis=0) - sm[0]

    x = jnp.concatenate([x_prompt.reshape(-1, D_MODEL), x_sample.reshape(-1, D_MODEL)], axis=0)
    zero_sc = jnp.zeros((1, D_MODEL), F32)
    zero_spec = pl.BlockSpec((1, D_MODEL), lambda i, j: (0, 0))
    tm = TOKEN_TILE

    h = _normmod(tok, x, norm_g[0, 0][None], modr, 0, 1, 0)
    bw = lambda a: a.astype(BF16)
    ffn_w = {(0, 0): (bw(ffn_w1[0, 0]), bw(ffn_w3[0, 0]), bw(ffn_w2[0, 0]))}
    rw_states, hg_states = [], []
    for l in range(depth):
        x, h = _ffn(tok, h, x, modr, l, 2, *ffn_w[(l, 0)],
                    norm_g[l, 1][None], (modr, _mod_spec(tok, tm, l, 4, 2)),
                    (modr, _mod_spec(tok, tm, l, 3, 2)), False)

        zb = _project(h, w_in, l, A_COLS, B_COLS, 896)
        zc = _project(h, w_in, l, A_COLS + B_COLS, C_COLS, 768)

        oa = _gmlp(h, w_in, l, mlp_norm_g[l], mlp_ws[l], mlp_bs[l])

        at, rt, bt, kt, v, gl, gate, bonus = _rwkv_prep(
            tok, zb, rwkv_mu[l][None], rwkv_w0[l], rwkv_w2[l], rwkv_a0[l], rwkv_a2[l], rwkv_g2[l],
            rwkv_kk[l][None], rwkv_ka[l][None], rwkv_rk[l].reshape(1, B_WIDTH))
        upcoming = [(l, 1)] + ([(l + 1, 0)] if l + 1 < depth else [])
        to_cast = [(w, fl, fk) for fl, fk in upcoming for w in (ffn_w1, ffn_w3, ffn_w2)]
        yf, yb, s_rw, of, obw, s_hg, casted = _mix_scan(tok, l, at, rt, bt, kt, v, gl, state_rwkv, zc, lower[l],
                                                        state_hgrn, to_cast)
        for n, key in enumerate(upcoming):
            ffn_w[key] = tuple(casted[3 * n:3 * n + 3])
        ob = _rwkv_post(yf, yb, bonus, gate,
                        rwkv_lnx_g[l].reshape(1, B_WIDTH), rwkv_lnx_b[l].reshape(1, B_WIDTH))
        oc = _hgrn_post(of, obw, zc, hgrn_gn[l][None])
        rw_states.append(s_rw)
        hg_states.append(s_hg)

        w_out_l = bw(w_out[l])
        x, h = _mix_out(tok, oa, ob, oc, w_out_l[:A_WIDTH], w_out_l[A_WIDTH:A_WIDTH + B_WIDTH],
                        w_out_l[A_WIDTH + B_WIDTH:], x, modr, l, norm_g[l, 2][None])

        if l + 1 < depth:
            x, h = _ffn(tok, h, x, modr, l, 8, *ffn_w[(l, 1)],
                        norm_g[l + 1, 0][None], (modr, _mod_spec(tok, tm, l + 1, 1, 2)),
                        (modr, _mod_spec(tok, tm, l + 1, 0, 2)), False)
        else:
            y_prompt, y_sample = _ffn(tok, h, x, modr, l, 8, *ffn_w[(l, 1)],
                                      final_g[None], (zero_sc, zero_spec), (zero_sc, zero_spec), True)

    return (y_prompt.reshape(x_prompt.shape), y_sample.reshape(x_sample.shape),
            jnp.stack(rw_states, axis=1), jnp.stack(hg_states, axis=1))
```

```python
import functools
import math

import jax
import jax.numpy as jnp
from jax import lax
from jax.experimental import pallas as pl
from jax.experimental.pallas import tpu as pltpu

F32 = jnp.float32
BF16 = jnp.bfloat16

D_MODEL = 2048
D_FF = 5632
N_MOD = 9
EPS = 1e-6
LNX_EPS = 64e-5

CHUNK_MLP = 128
A_WIDTH = 512
A_GROUPS = 4
A_GDIM = 128
B_HDIM = 64
B_WIDTH = 768
B_HEADS = 12
LORA = 64
GATE_LORA = 128
B_COLS = 3 * B_WIDTH + 4 * LORA + GATE_LORA
C_KDIM = 128
C_WIDTH = 768
C_HEADS = 6
C_COLS = 5 * C_WIDTH
A_COLS = 2 * A_WIDTH

LANES = 128
VMEM_LIMIT = 56 * 1024 * 1024

TOKEN_TILE = 512
FF_TILE = 512
PREP_TILE = 256
RW_CHUNK = 64
HG_BLOCK = 64
HG_SUB = 16


def _cparams(*sem):
    return pltpu.CompilerParams(dimension_semantics=sem, vmem_limit_bytes=VMEM_LIMIT)


def _dot(a, b):
    return jnp.dot(a, b, preferred_element_type=F32)


def _dot_nt(a, b):
    return lax.dot_general(a, b, (((1,), (1,)), ((), ())), preferred_element_type=F32)


def _dot_tn(a, b):
    return lax.dot_general(a, b, (((0,), (0,)), ((), ())), preferred_element_type=F32)


def _split(x, parts):
    out = []
    for _ in range(parts - 1):
        hi = x.astype(BF16)
        out.append(hi)
        x = x - hi.astype(F32)
    out.append(x.astype(BF16))
    return out


def _exact_left(m, x, parts):
    acc = None
    for p in _split(x, parts):
        t = _dot(m, p)
        acc = t if acc is None else acc + t
    return acc


def _exact_right(x, m, parts):
    acc = None
    for p in _split(x, parts):
        t = _dot(p, m)
        acc = t if acc is None else acc + t
    return acc


def _iota(shape, dim):
    return lax.broadcasted_iota(jnp.int32, shape, dim)


def _sigmoid(x):
    return jax.nn.sigmoid(x)


def _silu(x):
    return x * jax.nn.sigmoid(x)


def _norm_mod(x, g, sc, sh):
    ms = jnp.mean(x * x, axis=-1, keepdims=True)
    return (x * lax.rsqrt(ms + EPS) * g) * (1.0 + sc) + sh


def _mod_kernel(cond_ref, w_ref, b_ref, o_ref):
    s = _silu(cond_ref[...]).astype(BF16)
    o_ref[...] = _dot(s, w_ref[...].astype(BF16)) + b_ref[...]


def _mod_rows(cond8, w_mod, b_mod):
    depth, _, n = w_mod.shape
    tn = 1024
    return pl.pallas_call(
        _mod_kernel,
        grid=(depth, n // tn),
        in_specs=[
            pl.BlockSpec((8, D_MODEL), lambda l, j: (0, 0)),
            pl.BlockSpec((None, D_MODEL, tn), lambda l, j: (l, 0, j)),
            pl.BlockSpec((None, 1, tn), lambda l, j: (l, 0, j)),
        ],
        out_specs=pl.BlockSpec((None, 8, tn), lambda l, j: (l, 0, j)),
        out_shape=jax.ShapeDtypeStruct((depth, 8, n), F32),
        compiler_params=_cparams("parallel", "parallel"),
        name="adaln_rows",
    )(cond8, w_mod, b_mod.reshape(depth, 1, n))


class _Tokens:
    def __init__(self, n_ctx, ctx_len, n_lat, lat_len):
        self.n_ctx, self.ctx_len, self.n_lat, self.lat_len = n_ctx, ctx_len, n_lat, lat_len
        self.ctx_tokens = n_ctx * ctx_len
        self.total = self.ctx_tokens + n_lat * lat_len

    def mod_row(self, tile, i):
        nct = self.ctx_tokens // tile
        per_lat = self.lat_len // tile
        return jnp.where(i < nct, 0, 1 + (i - nct) // per_lat)


def _mod_spec(tok, tile, layer, which, grid_rank):
    def imap(*idx):
        return (layer, which, tok.mod_row(tile, idx[0]), 0, 0)
    del grid_rank
    return pl.BlockSpec((None, None, None, 1, D_MODEL), imap)


def _normmod_kernel(x_ref, g_ref, sc_ref, sh_ref, h_ref):
    h_ref[...] = _norm_mod(x_ref[...], g_ref[...], sc_ref[...], sh_ref[...]).astype(h_ref.dtype)


def _normmod(tok, x, norm_g_row, modr, layer, sc_i, sh_i):
    tm = TOKEN_TILE
    return pl.pallas_call(
        _normmod_kernel,
        grid=(tok.total // tm,),
        in_specs=[
            pl.BlockSpec((tm, D_MODEL), lambda i: (i, 0)),
            pl.BlockSpec((1, D_MODEL), lambda i: (0, 0)),
            _mod_spec(tok, tm, layer, sc_i, 1),
            _mod_spec(tok, tm, layer, sh_i, 1),
        ],
        out_specs=pl.BlockSpec((tm, D_MODEL), lambda i: (i, 0)),
        out_shape=jax.ShapeDtypeStruct((tok.total, D_MODEL), BF16),
        compiler_params=_cparams("parallel"),
        name="first_norm",
    )(x, norm_g_row, modr, modr)


def _ffn_kernel(h_ref, x_ref, gt_ref, w1_ref, w3_ref, w2_ref, gn_ref, scn_ref, shn_ref,
                out_a_ref, out_b_ref, acc_ref, *, n_ff, n_ctx_tiles):
    i = pl.program_id(0)
    j = pl.program_id(1)

    @pl.when(j == 0)
    def _():
        acc_ref[...] = jnp.zeros_like(acc_ref)

    h = h_ref[...]
    a = _dot(h, w1_ref[...])
    b = _dot(h, w3_ref[...])
    p = (_silu(a) * b).astype(BF16)
    acc_ref[...] += _dot(p, w2_ref[...])

    @pl.when(j == n_ff - 1)
    def _():
        xn = x_ref[...] + 0.5 * gt_ref[...] * acc_ref[...]
        hn = _norm_mod(xn, gn_ref[...], scn_ref[...], shn_ref[...])
        if n_ctx_tiles is None:
            out_a_ref[...] = xn
            out_b_ref[...] = hn.astype(out_b_ref.dtype)
        else:
            @pl.when(i < n_ctx_tiles)
            def _():
                out_a_ref[...] = hn

            @pl.when(i >= n_ctx_tiles)
            def _():
                out_b_ref[...] = hn


def _ffn(tok, h, x, modr, layer, gate_i, w1, w3, w2, next_g, next_sc, next_sh, final):
    tm, tf = TOKEN_TILE, FF_TILE
    n_ff = D_FF // tf
    nct = tok.ctx_tokens // tm
    row = pl.BlockSpec((tm, D_MODEL), lambda i, j: (i, 0))
    if final:
        out_specs = [pl.BlockSpec((tm, D_MODEL), lambda i, j: (jnp.minimum(i, nct - 1), 0)),
                     pl.BlockSpec((tm, D_MODEL), lambda i, j: (jnp.maximum(i - nct, 0), 0))]
        out_shape = [jax.ShapeDtypeStruct((tok.ctx_tokens, D_MODEL), F32),
                     jax.ShapeDtypeStruct((tok.total - tok.ctx_tokens, D_MODEL), F32)]
    else:
        out_specs = [row, row]
        out_shape = [jax.ShapeDtypeStruct((tok.total, D_MODEL), F32),
                     jax.ShapeDtypeStruct((tok.total, D_MODEL), BF16)]
    return pl.pallas_call(
        functools.partial(_ffn_kernel, n_ff=n_ff, n_ctx_tiles=nct if final else None),
        grid=(tok.total // tm, n_ff),
        in_specs=[
            row, row,
            _mod_spec(tok, tm, layer, gate_i, 2),
            pl.BlockSpec((D_MODEL, tf), lambda i, j: (0, j)),
            pl.BlockSpec((D_MODEL, tf), lambda i, j: (0, j)),
            pl.BlockSpec((tf, D_MODEL), lambda i, j: (j, 0)),
            pl.BlockSpec((1, D_MODEL), lambda i, j: (0, 0)),
            next_sc[1], next_sh[1],
        ],
        out_specs=out_specs,
        out_shape=out_shape,
        scratch_shapes=[pltpu.VMEM((tm, D_MODEL), F32)],
        compiler_params=_cparams("arbitrary", "arbitrary"),
        name="swiglu_half_step",
    )(h, x, modr, w1, w3, w2, next_g, next_sc[0], next_sh[0])


def _project_kernel(h_ref, w_ref, o_ref, wb_ref):
    @pl.when(pl.program_id(1) == 0)
    def _():
        wb_ref[...] = w_ref[...].astype(BF16)

    o_ref[...] = _dot(h_ref[...], wb_ref[...])


def _project(h, w_in, layer, col0, n, tn):
    t, k = h.shape
    tm = 2 * TOKEN_TILE
    assert t % tm == 0 and n % tn == 0
    return pl.pallas_call(
        _project_kernel,
        grid=(n // tn, t // tm),
        in_specs=[pl.BlockSpec((tm, k), lambda j, i: (i, 0)),
                  pl.BlockSpec((pl.Squeezed(), pl.Element(k), pl.Element(tn)),
                               lambda j, i: (layer, 0, pl.multiple_of(col0 + j * tn, LANES)))],
        out_specs=pl.BlockSpec((tm, tn), lambda j, i: (i, j)),
        out_shape=jax.ShapeDtypeStruct((t, n), F32),
        scratch_shapes=[pltpu.VMEM((k, tn), BF16)],
        compiler_params=_cparams("parallel", "arbitrary"),
        name="mix_in_proj",
    )(h, w_in)


def _gmlp_kernel(h_ref, w_ref, ng_ref, ws_ref, bs_ref, o_ref, wb_ref):
    @pl.when(pl.program_id(0) == 0)
    def _():
        wb_ref[...] = w_ref[...].astype(BF16)

    z = jax.nn.gelu(_dot(h_ref[...], wb_ref[...]))
    for c in range(z.shape[0] // CHUNK_MLP):
        rows = slice(c * CHUNK_MLP, (c + 1) * CHUNK_MLP)
        outs = []
        for g in range(A_GROUPS):
            u = z[rows, g * A_GDIM:(g + 1) * A_GDIM]
            v = z[rows, A_WIDTH + g * A_GDIM:A_WIDTH + (g + 1) * A_GDIM]
            ms = jnp.mean(v * v, axis=-1, keepdims=True)
            v = v * lax.rsqrt(ms + EPS) * ng_ref[g:g + 1, :]
            mixed = _dot(ws_ref[g], v.astype(BF16)) + bs_ref[g]
            outs.append(u * mixed)
        o_ref[rows, :] = jnp.concatenate(outs, axis=-1).astype(o_ref.dtype)


def _gmlp(h, w_in, layer, ng, ws, bs):
    t, k = h.shape
    tm = TOKEN_TILE
    return pl.pallas_call(
        _gmlp_kernel,
        grid=(t // tm,),
        in_specs=[
            pl.BlockSpec((tm, k), lambda i: (i, 0)),
            pl.BlockSpec((pl.Squeezed(), pl.Element(k), pl.Element(A_COLS)), lambda i: (layer, 0, 0)),
            pl.BlockSpec((A_GROUPS, A_GDIM), lambda i: (0, 0)),
            pl.BlockSpec((A_GROUPS, CHUNK_MLP, CHUNK_MLP), lambda i: (0, 0, 0)),
            pl.BlockSpec((A_GROUPS, CHUNK_MLP, 1), lambda i: (0, 0, 0)),
        ],
        out_specs=pl.BlockSpec((tm, A_WIDTH), lambda i: (i, 0)),
        out_shape=jax.ShapeDtypeStruct((t, A_WIDTH), BF16),
        scratch_shapes=[pltpu.VMEM((k, A_COLS), BF16)],
        compiler_params=_cparams("arbitrary"),
        name="gmlp_chunk_mix",
    )(h, w_in, ng, ws.astype(BF16), bs[:, :, None])


def _head_ones():
    r = _iota((LANES, LANES), 0) // B_HDIM
    c = _iota((LANES, LANES), 1) // B_HDIM
    return (r == c).astype(BF16)


def _head_sum(x, parts=2):
    ones = _head_ones()
    outs = []
    for s in range(x.shape[-1] // LANES):
        outs.append(_exact_right(x[:, s * LANES:(s + 1) * LANES], ones, parts))
    return jnp.concatenate(outs, axis=-1)


def _rwkv_prep_kernel(z_ref, zp_ref, zn_ref, mu_ref, w0_ref, w2_ref, a0_ref, a2_ref, g2_ref,
                      kkw_ref, kaw_ref, rk_ref,
                      at_ref, rt_ref, bt_ref, kt_ref, v_ref, gl_ref, gate_ref, bonus_ref,
                      *, n_ctx_tiles, ctx_tiles_per_seq, lat_tiles_per_seq):
    i = pl.program_id(0)
    pos = jnp.where(i < n_ctx_tiles, i % ctx_tiles_per_seq, (i - n_ctx_tiles) % lat_tiles_per_seq)
    last = jnp.where(i < n_ctx_tiles, ctx_tiles_per_seq - 1, lat_tiles_per_seq - 1)
    z = z_ref[...]
    t = z.shape[0]
    row = _iota((t, 1), 0)
    halo_prev = jnp.where(pos != 0, zp_ref[7:8, :], 0.0)
    halo_next = jnp.where(pos != last, zn_ref[0:1, :], 0.0)
    prev = jnp.where(row == 0, halo_prev, pltpu.roll(z, 1, 0))
    nxt = jnp.where(row == t - 1, halo_next, pltpu.roll(z, t - 1, 0))
    z = z + (0.5 * (prev + nxt) - z) * mu_ref[...]

    w = B_WIDTH
    r, k, v = z[:, 0:w], z[:, w:2 * w], z[:, 2 * w:3 * w]
    wd = z[:, 3 * w:3 * w + 2 * LORA]
    ad = z[:, 3 * w + 2 * LORA:3 * w + 4 * LORA]
    gd = z[:, 3 * w + 4 * LORA:]
    twd = jnp.tanh(wd).astype(BF16)
    adb = ad.astype(BF16)

    kk = k * kkw_ref[...]
    nrm = jnp.sqrt(_head_sum(kk * kk))
    kk = kk / jnp.maximum(nrm, 1e-12)

    v_ref[...] = v.astype(BF16)
    gate_ref[...] = _dot(_sigmoid(gd).astype(BF16), g2_ref[...])
    cl = RW_CHUNK
    ri, ci = _iota((t, t), 0), _iota((t, t), 1)
    same_chunk = (ri // cl) == (ci // cl)
    ksum = jnp.zeros_like(k)
    for d in range(2):
        w_raw = w0_ref[d:d + 1, :] + _dot(twd[:, d * LORA:(d + 1) * LORA], w2_ref[d])
        lw = -math.exp(-0.5) * _sigmoid(w_raw)
        a = _sigmoid(a0_ref[d:d + 1, :] + _dot(adb[:, d * LORA:(d + 1) * LORA], a2_ref[d]))
        kd = k * (1.0 + (a - 1.0) * kaw_ref[...])
        ksum = ksum + kd
        upto = same_chunk & ((ci >= ri) if d == 1 else (ci <= ri))
        g = _exact_left(upto.astype(BF16), lw, 2)
        ieg = jnp.exp(-g)
        at_ref[d] = (-kk * jnp.exp(g - lw)).astype(BF16)
        rt_ref[d] = (r * jnp.exp(g)).astype(BF16)
        bt_ref[d] = (kk * a * ieg).astype(BF16)
        kt_ref[d] = (kd * ieg).astype(BF16)
        for c in range(t // cl):
            end = c * cl if d == 1 else (c + 1) * cl - 1
            gl_ref[d, c] = jnp.exp(g[end:end + 1, :])
    bonus_ref[...] = _head_sum(r * ksum * rk_ref[...]) * v


def _rwkv_prep(tok, zb, mu, w0, w2, a0, a2, g2, kkw, kaw, rk):
    tp = PREP_TILE
    n_tiles = tok.total // tp
    rows8 = tok.total // 8
    per = tp // 8
    full = lambda shape: pl.BlockSpec(shape, lambda i: (0,) * len(shape))
    cpt = tp // RW_CHUNK
    tile = pl.BlockSpec((tp, B_WIDTH), lambda i: (i, 0))
    tile2 = pl.BlockSpec((2, tp, B_WIDTH), lambda i: (0, i, 0))
    gl_spec = pl.BlockSpec((2, cpt, 1, B_WIDTH), lambda i: (0, i, 0, 0))
    o1 = jax.ShapeDtypeStruct((tok.total, B_WIDTH), F32)
    o1b = jax.ShapeDtypeStruct((tok.total, B_WIDTH), BF16)
    o2b = jax.ShapeDtypeStruct((2, tok.total, B_WIDTH), BF16)
    ogl = jax.ShapeDtypeStruct((2, tok.total // RW_CHUNK, 1, B_WIDTH), F32)
    kern = functools.partial(_rwkv_prep_kernel, n_ctx_tiles=tok.ctx_tokens // tp,
                             ctx_tiles_per_seq=tok.ctx_len // tp, lat_tiles_per_seq=tok.lat_len // tp)
    return pl.pallas_call(
        kern,
        grid=(n_tiles,),
        in_specs=[
            pl.BlockSpec((tp, B_COLS), lambda i: (i, 0)),
            pl.BlockSpec((8, B_COLS), lambda i: (jnp.maximum(i * per - 1, 0), 0)),
            pl.BlockSpec((8, B_COLS), lambda i: (jnp.minimum((i + 1) * per, rows8 - 1), 0)),
            full((1, B_COLS)), full((2, B_WIDTH)), full((2, LORA, B_WIDTH)), full((2, B_WIDTH)),
            full((2, LORA, B_WIDTH)), full((GATE_LORA, B_WIDTH)),
            full((1, B_WIDTH)), full((1, B_WIDTH)), full((1, B_WIDTH)),
        ],
        out_specs=[tile2, tile2, tile2, tile2, tile, gl_spec, tile, tile],
        out_shape=[o2b, o2b, o2b, o2b, o1b, ogl, o1, o1],
        compiler_params=_cparams("parallel"),
        name="rwkv_prep",
    )(zb, zb, zb, mu, w0, w2.astype(BF16), a0, a2.astype(BF16), g2.astype(BF16), kkw, kaw, rk)


def _pair_blockdiag(x):
    lane = _iota(x.shape, 1)
    zero = jnp.zeros_like(x)
    return jnp.concatenate([jnp.where(lane < B_HDIM, x, zero), jnp.where(lane >= B_HDIM, x, zero)], axis=0)


def _rwkv_chunks(streams, s_ref, y_refs):
    c = streams[0][0].shape[0]
    n_pairs = B_HEADS // 2
    rr, cc = _iota((2 * c, 2 * c), 0), _iota((2 * c, 2 * c), 1)
    same = (rr // c) == (cc // c)
    tt, ss = rr % c, cc % c
    eye = (rr == cc).astype(F32)
    pair_masks = []
    n = 1
    while n < c:
        pair_masks.append(same & ((tt // (2 * n)) == (ss // (2 * n))) & ((tt // n) != (ss // n)))
        n *= 2

    units = []
    for d, (a_t, r_t, b_t, k_t, vb, gl, reverse) in enumerate(streams):
        before = same & ((ss > tt) if reverse else (ss < tt))
        upto = same & ((ss >= tt) if reverse else (ss <= tt))
        for p in range(n_pairs):
            sl = slice(p * LANES, (p + 1) * LANES)
            units.append(dict(d=d, p=p, before=before, upto=upto, gl=gl[:, sl],
                              ar=jnp.concatenate([_pair_blockdiag(a_t[:, sl]), _pair_blockdiag(r_t[:, sl])], 0),
                              bk=jnp.concatenate([_pair_blockdiag(b_t[:, sl]), _pair_blockdiag(k_t[:, sl])], 0),
                              v=_pair_blockdiag(vb[:, sl])))
    m = 2 * c
    for u in units:
        u["s0"] = s_ref[u["d"], u["p"]]
        u["gram"] = _dot_nt(u["ar"], jnp.concatenate([u["bk"], u["s0"].astype(BF16)], axis=0))
    yield
    for u in units:
        u["ws"] = u["gram"][:, 2 * m:]
        g = u["gram"]
        u["lmat"] = jnp.where(u["before"], g[:m, :m], 0.0)
        aak = jnp.where(u["before"], g[:m, m:2 * m], 0.0).astype(BF16)
        u["rbk"] = jnp.concatenate([jnp.where(u["upto"], g[m:, :m], 0.0),
                                    jnp.where(u["upto"], g[m:, m:2 * m], 0.0)], axis=1).astype(BF16)
        u["x"] = u["ws"][:m] + _dot(aak, u["v"])
    yield
    for lvl, pm in enumerate(pair_masks):
        for u in units:
            link = jnp.where(pm, u["lmat"], 0.0)
            if lvl == 0:
                u["tb"] = (eye + link).astype(BF16)
            else:
                u["tmp"] = _dot(link.astype(BF16), u["tb"]).astype(BF16)
        if lvl > 0:
            yield
            for u in units:
                u["tb"] = (u["tb"].astype(F32) + _dot(u["tb"], u["tmp"])).astype(BF16)
            yield
    for u in units:
        u["uv"] = jnp.concatenate([_dot(u["tb"], u["x"].astype(BF16)).astype(BF16), u["v"]], 0)
    yield
    ys = [[None] * n_pairs for _ in streams]
    for u in units:
        y = u["ws"][m:] + _dot(u["rbk"], u["uv"])
        ys[u["d"]][u["p"]] = y[:c] + y[c:]
        s_ref[u["d"], u["p"]] = (u["s0"] + _dot_tn(u["uv"], u["bk"])) * u["gl"]
    for d, row in enumerate(ys):
        y_refs[d][...] = jnp.concatenate(row, axis=-1)


def _scan_table(tok, chunk):
    cols = []
    for s in range(tok.n_ctx + tok.n_lat):
        if s < tok.n_ctx:
            base, nc = s * tok.ctx_len // chunk, tok.ctx_len // chunk
        else:
            base = (tok.ctx_tokens + (s - tok.n_ctx) * tok.lat_len) // chunk
            nc = tok.lat_len // chunk
        for c in range(nc):
            cols.append((base + c, base + nc - 1 - c, int(s < tok.n_ctx), int(c == 0), int(c == nc - 1),
                         max(s - tok.n_ctx, 0), min(s, tok.n_ctx - 1)))
    return jnp.asarray(list(zip(*cols)), dtype=jnp.int32)


def _rwkv_post_kernel(yf_ref, yb_ref, bonus_ref, gate_ref, g_ref, b_ref, o_ref):
    y = yf_ref[...] + yb_ref[...]
    mean = _head_sum(y, 3) * (1.0 / B_HDIM)
    yc = y - mean
    var = _head_sum(yc * yc) * (1.0 / B_HDIM)
    y = yc * lax.rsqrt(var + LNX_EPS) * g_ref[...] + b_ref[...] + bonus_ref[...]
    o_ref[...] = (y * gate_ref[...]).astype(o_ref.dtype)


def _rwkv_post(yf, yb, bonus, gate, lnx_g, lnx_b):
    t = yf.shape[0]
    tm = TOKEN_TILE
    tile = pl.BlockSpec((tm, B_WIDTH), lambda i: (i, 0))
    vec = pl.BlockSpec((1, B_WIDTH), lambda i: (0, 0))
    return pl.pallas_call(
        _rwkv_post_kernel,
        grid=(t // tm,),
        in_specs=[tile, tile, tile, tile, vec, vec],
        out_specs=tile,
        out_shape=jax.ShapeDtypeStruct((t, B_WIDTH), BF16),
        compiler_params=_cparams("parallel"),
        name="rwkv_post",
    )(yf, yb, bonus, gate, lnx_g, lnx_b)


def _hgrn_block(qraw, fraw, v, lb, st_ref, o_ref, reverse):
    n = qraw.shape[0]
    sub = HG_SUB
    fg = lb + (1.0 - lb) * _sigmoid(fraw)
    logf = jnp.log(fg)
    kg = 1.0 - fg
    q = _silu(qraw)
    ri, ci = _iota((n, n), 0), _iota((n, n), 1)
    same = (ri // sub) == (ci // sub)
    upto = same & ((ci >= ri) if reverse else (ci <= ri))
    bcum = _exact_left(upto.astype(BF16), logf, 3)
    half = 8
    hi = _iota((half, 1), 0)
    order = range(n // sub - 1, -1, -1) if reverse else range(n // sub)
    out_rows = [None] * (n // sub)
    for sc in order:
        last = sc * sub if reverse else (sc + 1) * sub - 1
        outs = []
        for h in range(C_HEADS):
            cols = slice(h * C_KDIM, (h + 1) * C_KDIM)
            rows = slice(sc * sub, (sc + 1) * sub)
            qs, ks, vs, bs = q[rows, cols], kg[rows, cols], v[rows, cols], bcum[rows, cols]
            fs = fg[rows, cols]
            b_end = bcum[last:last + 1, cols]
            hq = [jnp.zeros((half, C_KDIM), F32) for _ in range(sub // half)]
            intra = [jnp.zeros((half, C_KDIM), F32) for _ in range(sub // half)]
            for j in (range(sub) if reverse else range(sub - 1, -1, -1)):
                jh = j // half
                hq[jh] = jnp.where(hi == j % half, qs[jh * half:(jh + 1) * half], hq[jh])
                entered = range(jh + 1) if reverse else range(jh, sub // half)
                for hf in entered:
                    att = jnp.sum(hq[hf] * ks[j:j + 1, :], axis=-1, keepdims=True)
                    intra[hf] = intra[hf] + att * vs[j:j + 1, :]
                    hq[hf] = hq[hf] * fs[j:j + 1, :]
            o = _dot_nt((qs * jnp.exp(bs)).astype(BF16), st_ref[h].astype(BF16))
            outs.append(o + jnp.concatenate(intra, axis=0))
            ke = (ks * jnp.exp(b_end - bs)).astype(BF16)
            st_ref[h] = st_ref[h] * jnp.exp(b_end) + _dot_tn(vs.astype(BF16), ke)
            if h % (C_HEADS // 2) == C_HEADS // 2 - 1:
                yield
        out_rows[sc] = jnp.concatenate(outs, axis=-1)
    o_ref[...] = jnp.concatenate(out_rows, axis=0)


def _mix_scan_kernel(tab_ref, *refs, n_cast):
    (atf_ref, rtf_ref, btf_ref, ktf_ref, vf_ref, glf_ref,
     atb_ref, rtb_ref, btb_ref, ktb_ref, vb_ref, glb_ref, rw0_ref,
     qf_ref, ff_ref, if_ref, qb_ref, fb_ref, ib_ref, lb_ref, hg0_ref) = refs[:21]
    cast_in = refs[21:21 + n_cast]
    yf_ref, yb_ref, rwfin_ref, of_ref, ob_ref, hgfin_ref = refs[21 + n_cast:27 + n_cast]
    cast_out = refs[27 + n_cast:27 + 2 * n_cast]
    s_ref, st_ref = refs[27 + 2 * n_cast:]
    step = pl.program_id(0)
    hd = B_HDIM
    for src_ref, dst_ref in zip(cast_in, cast_out):
        dst_ref[...] = src_ref[...].astype(BF16)

    is_ctx = tab_ref[2, step] == 1
    first = tab_ref[3, step] == 1
    last = tab_ref[4, step] == 1

    @pl.when(first & is_ctx)
    def _():
        s_ref[...] = jnp.zeros_like(s_ref)
        st_ref[...] = jnp.zeros_like(st_ref)

    @pl.when(first & jnp.logical_not(is_ctx))
    def _():
        z = jnp.zeros((hd, hd), F32)
        for d in range(2):
            for p in range(B_HEADS // 2):
                s_ref[d, p] = jnp.concatenate(
                    [jnp.concatenate([rw0_ref[d, 2 * p], z], axis=1),
                     jnp.concatenate([z, rw0_ref[d, 2 * p + 1]], axis=1)], axis=0)
            for h in range(C_HEADS):
                st_ref[d, h] = hg0_ref[d, h].T

    pending = [
        _rwkv_chunks(
            [(atf_ref[...], rtf_ref[...], btf_ref[...], ktf_ref[...], vf_ref[...], glf_ref[...], False),
             (atb_ref[...], rtb_ref[...], btb_ref[...], ktb_ref[...], vb_ref[...], glb_ref[...], True)],
            s_ref, (yf_ref, yb_ref)),
        _hgrn_block(qf_ref[...], ff_ref[...], if_ref[...], lb_ref[0:1, :], st_ref.at[0], of_ref, False),
        _hgrn_block(qb_ref[...], fb_ref[...], ib_ref[...], lb_ref[1:2, :], st_ref.at[1], ob_ref, True),
    ]
    while pending:
        for gen in list(pending):
            if next(gen, pending) is pending:
                pending.remove(gen)

    @pl.when(last & is_ctx)
    def _():
        for d in range(2):
            for p in range(B_HEADS // 2):
                s = s_ref[d, p]
                rwfin_ref[d, 2 * p] = s[:hd, :hd]
                rwfin_ref[d, 2 * p + 1] = s[hd:, hd:]
            for h in range(C_HEADS):
                hgfin_ref[d, h] = st_ref[d, h].T


def _mix_scan(tok, layer, at, rt, bt, kt, v, gl, rw0, zc, lb, hg0, to_cast):
    cl = RW_CHUNK
    assert HG_BLOCK == cl
    tab = _scan_table(tok, cl)
    fwd = lambda col: (lambda s, t: (t[0, s], col))
    bwd = lambda col: (lambda s, t: (t[1, s], col))
    fwd2 = lambda s, t: (0, t[0, s], 0)
    bwd2 = lambda s, t: (1, t[1, s], 0)
    blk = lambda im: pl.BlockSpec((cl, B_WIDTH), im)
    blk2 = lambda im: pl.BlockSpec((None, cl, B_WIDTH), im)
    glf = pl.BlockSpec((None, None, 1, B_WIDTH), lambda s, t: (0, t[0, s], 0, 0))
    glb = pl.BlockSpec((None, None, 1, B_WIDTH), lambda s, t: (1, t[1, s], 0, 0))
    rw_in = pl.BlockSpec((None, None, 2, B_HEADS, B_HDIM, B_HDIM), lambda s, t: (t[5, s], layer, 0, 0, 0, 0))
    hg_in = pl.BlockSpec((None, None, 2, C_HEADS, C_KDIM, C_KDIM), lambda s, t: (t[5, s], layer, 0, 0, 0, 0))
    rw_out = pl.BlockSpec((None, 2, B_HEADS, B_HDIM, B_HDIM), lambda s, t: (t[6, s], 0, 0, 0, 0))
    hg_out = pl.BlockSpec((None, 2, C_HEADS, C_KDIM, C_KDIM), lambda s, t: (t[6, s], 0, 0, 0, 0))
    cblk = lambda im: pl.BlockSpec((cl, C_WIDTH), im)
    yshape = jax.ShapeDtypeStruct((tok.total, B_WIDTH), F32)
    oshape = jax.ShapeDtypeStruct((tok.total, C_WIDTH), F32)
    cast_ops, cast_in, cast_out, cast_shapes = [], [], [], []
    for w, wl, wk in to_cast:
        rows, cols = w.shape[2:]
        tile_rows = next(r for r in range(16, rows + 1, 16) if rows % r == 0 and rows // r <= tab.shape[1])
        n_tiles = rows // tile_rows
        cast_ops.append(w)
        cast_in.append(pl.BlockSpec((None, None, tile_rows, cols),
                                    lambda s, t, wl=wl, wk=wk, n=n_tiles: (wl, wk, jnp.minimum(s, n - 1), 0)))
        cast_out.append(pl.BlockSpec((tile_rows, cols), lambda s, t, n=n_tiles: (jnp.minimum(s, n - 1), 0)))
        cast_shapes.append(jax.ShapeDtypeStruct((rows, cols), BF16))
    outs = pl.pallas_call(
        functools.partial(_mix_scan_kernel, n_cast=len(to_cast)),
        grid_spec=pltpu.PrefetchScalarGridSpec(
            num_scalar_prefetch=1,
            grid=(tab.shape[1],),
            in_specs=[blk2(fwd2), blk2(fwd2), blk2(fwd2), blk2(fwd2), blk(fwd(0)), glf,
                      blk2(bwd2), blk2(bwd2), blk2(bwd2), blk2(bwd2), blk(bwd(0)), glb, rw_in,
                      cblk(fwd(0)), cblk(fwd(1)), cblk(fwd(3)), cblk(bwd(0)), cblk(bwd(2)), cblk(bwd(3)),
                      pl.BlockSpec((2, C_WIDTH), lambda s, t: (0, 0)), hg_in] + cast_in,
            out_specs=[blk(fwd(0)), blk(bwd(0)), rw_out, cblk(fwd(0)), cblk(bwd(0)), hg_out] + cast_out,
            scratch_shapes=[pltpu.VMEM((2, B_HEADS // 2, LANES, LANES), F32),
                            pltpu.VMEM((2, C_HEADS, C_KDIM, C_KDIM), F32)]),
        out_shape=[yshape, yshape, jax.ShapeDtypeStruct((tok.n_ctx,) + rw0.shape[2:], F32),
                   oshape, oshape, jax.ShapeDtypeStruct((tok.n_ctx,) + hg0.shape[2:], F32)] + cast_shapes,
        compiler_params=_cparams("arbitrary"),
        name="rwkv_hgrn_scan",
    )(tab, at, rt, bt, kt, v, gl, at, rt, bt, kt, v, gl, rw0, zc, zc, zc, zc, zc, zc, lb, hg0, *cast_ops)
    return tuple(outs[:6]) + (list(outs[6:]),)


def _hgrn_post_kernel(of_ref, ob_ref, g_ref, gn_ref, o_ref):
    o = of_ref[...] + ob_ref[...]
    gate = _silu(g_ref[...])
    outs = []
    for h in range(C_HEADS):
        cols = slice(h * C_KDIM, (h + 1) * C_KDIM)
        oh = o[:, cols]
        ms = jnp.mean(oh * oh, axis=-1, keepdims=True)
        outs.append(oh * lax.rsqrt(ms + EPS) * gn_ref[...] * gate[:, cols])
    o_ref[...] = jnp.concatenate(outs, axis=-1).astype(o_ref.dtype)


def _hgrn_post(of, ob, zc, gn):
    t = of.shape[0]
    tm = TOKEN_TILE
    tile = pl.BlockSpec((tm, C_WIDTH), lambda i: (i, 0))
    return pl.pallas_call(
        _hgrn_post_kernel,
        grid=(t // tm,),
        in_specs=[tile, tile, pl.BlockSpec((tm, C_WIDTH), lambda i: (i, 4)),
                  pl.BlockSpec((1, C_KDIM), lambda i: (0, 0))],
        out_specs=tile,
        out_shape=jax.ShapeDtypeStruct((t, C_WIDTH), BF16),
        compiler_params=_cparams("parallel"),
        name="hgrn2_post",
    )(of, ob, zc, gn)


def _mix_out_kernel(oa_ref, ob_ref, oc_ref, wa_ref, wb_ref, wc_ref, x_ref, gt_ref,
                    gn_ref, scn_ref, shn_ref, xo_ref, ho_ref):
    y = _dot(oa_ref[...], wa_ref[...]) + _dot(ob_ref[...], wb_ref[...]) + _dot(oc_ref[...], wc_ref[...])
    xn = x_ref[...] + gt_ref[...] * y
    xo_ref[...] = xn
    ho_ref[...] = _norm_mod(xn, gn_ref[...], scn_ref[...], shn_ref[...]).astype(ho_ref.dtype)


def _mix_out(tok, oa, ob, oc, wa, wb, wc, x, modr, layer, next_g):
    tm = TOKEN_TILE
    row = pl.BlockSpec((tm, D_MODEL), lambda i: (i, 0))
    full = lambda a: pl.BlockSpec(a.shape, lambda i: (0, 0))
    return pl.pallas_call(
        _mix_out_kernel,
        grid=(tok.total // tm,),
        in_specs=[pl.BlockSpec((tm, A_WIDTH), lambda i: (i, 0)),
                  pl.BlockSpec((tm, B_WIDTH), lambda i: (i, 0)),
                  pl.BlockSpec((tm, C_WIDTH), lambda i: (i, 0)),
                  full(wa), full(wb), full(wc), row,
                  _mod_spec(tok, tm, layer, 5, 1),
                  pl.BlockSpec((1, D_MODEL), lambda i: (0, 0)),
                  _mod_spec(tok, tm, layer, 7, 1), _mod_spec(tok, tm, layer, 6, 1)],
        out_specs=[row, row],
        out_shape=[jax.ShapeDtypeStruct((tok.total, D_MODEL), F32),
                   jax.ShapeDtypeStruct((tok.total, D_MODEL), BF16)],
        compiler_params=_cparams("parallel"),
        name="mix_out_proj",
    )(oa, ob, oc, wa, wb, wc, x, modr, next_g, modr, modr)


def kernel(x_prompt, x_sample, state_rwkv, state_hgrn, c, c_ctx, norm_g, w_mod, b_mod, ffn_w1, ffn_w3,
           ffn_w2, w_in, w_out, mlp_norm_g, mlp_ws, mlp_bs, rwkv_mu, rwkv_w0, rwkv_w2, rwkv_a0, rwkv_a2,
           rwkv_g2, rwkv_kk, rwkv_ka, rwkv_rk, rwkv_lnx_g, rwkv_lnx_b, hgrn_lb, hgrn_gn, final_g):
    n_ctx, ctx_len, _ = x_prompt.shape
    n_lat, lat_len, _ = x_sample.shape
    depth = w_mod.shape[0]
    tok = _Tokens(n_ctx, ctx_len, n_lat, lat_len)
    assert n_lat + 1 <= 8 and ctx_len % PREP_TILE == 0 and lat_len % TOKEN_TILE == 0

    cond8 = jnp.zeros((8, D_MODEL), F32).at[0].set(c_ctx).at[1:1 + n_lat].set(c)
    mod = _mod_rows(cond8, w_mod, b_mod)
    modr = mod.reshape(depth, 8, N_MOD, D_MODEL).transpose(0, 2, 1, 3)[:, :, :, None, :]

    sm = jax.nn.softmax(hgrn_lb.astype(F32), axis=0)
    lower = jnp.cumsum(sm, axis=0) - sm[0]

    x = jnp.concatenate([x_prompt.reshape(-1, D_MODEL), x_sample.reshape(-1, D_MODEL)], axis=0)
    zero_sc = jnp.zeros((1, D_MODEL), F32)
    zero_spec = pl.BlockSpec((1, D_MODEL), lambda i, j: (0, 0))
    tm = TOKEN_TILE

    h = _normmod(tok, x, norm_g[0, 0][None], modr, 0, 1, 0)
    bw = lambda a: a.astype(BF16)
    ffn_w = {(0, 0): (bw(ffn_w1[0, 0]), bw(ffn_w3[0, 0]), bw(ffn_w2[0, 0]))}
    rw_states, hg_states = [], []
    for l in range(depth):
        x, h = _ffn(tok, h, x, modr, l, 2, *ffn_w[(l, 0)],
                    norm_g[l, 1][None], (modr, _mod_spec(tok, tm, l, 4, 2)),
                    (modr, _mod_spec(tok, tm, l, 3, 2)), False)

        zb = _project(h, w_in, l, A_COLS, B_COLS, 896)
        zc = _project(h, w_in, l, A_COLS + B_COLS, C_COLS, 768)

        oa = _gmlp(h, w_in, l, mlp_norm_g[l], mlp_ws[l], mlp_bs[l])

        at, rt, bt, kt, v, gl, gate, bonus = _rwkv_prep(
            tok, zb, rwkv_mu[l][None], rwkv_w0[l], rwkv_w2[l], rwkv_a0[l], rwkv_a2[l], rwkv_g2[l],
            rwkv_kk[l][None], rwkv_ka[l][None], rwkv_rk[l].reshape(1, B_WIDTH))
        upcoming = [(l, 1)] + ([(l + 1, 0)] if l + 1 < depth else [])
        to_cast = [(w, fl, fk) for fl, fk in upcoming for w in (ffn_w1, ffn_w3, ffn_w2)]
        yf, yb, s_rw, of, obw, s_hg, casted = _mix_scan(tok, l, at, rt, bt, kt, v, gl, state_rwkv, zc, lower[l],
                                                        state_hgrn, to_cast)
        for n, key in enumerate(upcoming):
            ffn_w[key] = tuple(casted[3 * n:3 * n + 3])
        ob = _rwkv_post(yf, yb, bonus, gate,
                        rwkv_lnx_g[l].reshape(1, B_WIDTH), rwkv_lnx_b[l].reshape(1, B_WIDTH))
        oc = _hgrn_post(of, obw, zc, hgrn_gn[l][None])
        rw_states.append(s_rw)
        hg_states.append(s_hg)

        w_out_l = bw(w_out[l])
        x, h = _mix_out(tok, oa, ob, oc, w_out_l[:A_WIDTH], w_out_l[A_WIDTH:A_WIDTH + B_WIDTH],
                        w_out_l[A_WIDTH + B_WIDTH:], x, modr, l, norm_g[l, 2][None])

        if l + 1 < depth:
            x, h = _ffn(tok, h, x, modr, l, 8, *ffn_w[(l, 1)],
                        norm_g[l + 1, 0][None], (modr, _mod_spec(tok, tm, l + 1, 1, 2)),
                        (modr, _mod_spec(tok, tm, l + 1, 0, 2)), False)
        else:
            y_prompt, y_sample = _ffn(tok, h, x, modr, l, 8, *ffn_w[(l, 1)],
                                      final_g[None], (zero_sc, zero_spec), (zero_sc, zero_spec), True)

    return (y_prompt.reshape(x_prompt.shape), y_sample.reshape(x_sample.shape),
            jnp.stack(rw_states, axis=1), jnp.stack(hg_states, axis=1))
```

```python
import functools
import math

import jax
import jax.numpy as jnp
from jax import lax
from jax.experimental import pallas as pl
from jax.experimental.pallas import tpu as pltpu

F32 = jnp.float32
BF16 = jnp.bfloat16

D_MODEL = 2048
D_FF = 5632
N_MOD = 9
EPS = 1e-6
LNX_EPS = 64e-5

CHUNK_MLP = 128
A_WIDTH = 512
A_GROUPS = 4
A_GDIM = 128
B_HDIM = 64
B_WIDTH = 768
B_HEADS = 12
LORA = 64
GATE_LORA = 128
B_COLS = 3 * B_WIDTH + 4 * LORA + GATE_LORA
C_KDIM = 128
C_WIDTH = 768
C_HEADS = 6
C_COLS = 5 * C_WIDTH
A_COLS = 2 * A_WIDTH

LANES = 128
VMEM_LIMIT = 56 * 1024 * 1024

TOKEN_TILE = 512
FF_TILE = 512
PREP_TILE = 256
RW_CHUNK = 64
HG_BLOCK = 64
HG_SUB = 16


def _cparams(*sem):
    return pltpu.CompilerParams(dimension_semantics=sem, vmem_limit_bytes=VMEM_LIMIT)


def _dot(a, b):
    return jnp.dot(a, b, preferred_element_type=F32)


def _dot_nt(a, b):
    return lax.dot_general(a, b, (((1,), (1,)), ((), ())), preferred_element_type=F32)


def _dot_tn(a, b):
    return lax.dot_general(a, b, (((0,), (0,)), ((), ())), preferred_element_type=F32)


def _split(x, parts):
    out = []
    for _ in range(parts - 1):
        hi = x.astype(BF16)
        out.append(hi)
        x = x - hi.astype(F32)
    out.append(x.astype(BF16))
    return out


def _exact_left(m, x, parts):
    acc = None
    for p in _split(x, parts):
        t = _dot(m, p)
        acc = t if acc is None else acc + t
    return acc


def _exact_right(x, m, parts):
    acc = None
    for p in _split(x, parts):
        t = _dot(p, m)
        acc = t if acc is None else acc + t
    return acc


def _iota(shape, dim):
    return lax.broadcasted_iota(jnp.int32, shape, dim)


def _sigmoid(x):
    return jax.nn.sigmoid(x)


def _silu(x):
    return x * jax.nn.sigmoid(x)


def _norm_mod(x, g, sc, sh):
    ms = jnp.mean(x * x, axis=-1, keepdims=True)
    return (x * lax.rsqrt(ms + EPS) * g) * (1.0 + sc) + sh


def _mod_kernel(cond_ref, w_ref, b_ref, o_ref):
    s = _silu(cond_ref[...]).astype(BF16)
    o_ref[...] = _dot(s, w_ref[...].astype(BF16)) + b_ref[...]


def _mod_rows(cond8, w_mod, b_mod):
    depth, _, n = w_mod.shape
    tn = 1024
    return pl.pallas_call(
        _mod_kernel,
        grid=(depth, n // tn),
        in_specs=[
            pl.BlockSpec((8, D_MODEL), lambda l, j: (0, 0)),
            pl.BlockSpec((None, D_MODEL, tn), lambda l, j: (l, 0, j)),
            pl.BlockSpec((None, 1, tn), lambda l, j: (l, 0, j)),
        ],
        out_specs=pl.BlockSpec((None, 8, tn), lambda l, j: (l, 0, j)),
        out_shape=jax.ShapeDtypeStruct((depth, 8, n), F32),
        compiler_params=_cparams("parallel", "parallel"),
        name="adaln_rows",
    )(cond8, w_mod, b_mod.reshape(depth, 1, n))


class _Tokens:
    def __init__(self, n_ctx, ctx_len, n_lat, lat_len):
        self.n_ctx, self.ctx_len, self.n_lat, self.lat_len = n_ctx, ctx_len, n_lat, lat_len
        self.ctx_tokens = n_ctx * ctx_len
        self.total = self.ctx_tokens + n_lat * lat_len

    def mod_row(self, tile, i):
        nct = self.ctx_tokens // tile
        per_lat = self.lat_len // tile
        return jnp.where(i < nct, 0, 1 + (i - nct) // per_lat)


def _mod_spec(tok, tile, layer, which, grid_rank):
    def imap(*idx):
        return (layer, which, tok.mod_row(tile, idx[0]), 0, 0)
    del grid_rank
    return pl.BlockSpec((None, None, None, 1, D_MODEL), imap)


def _normmod_kernel(x_ref, g_ref, sc_ref, sh_ref, h_ref):
    h_ref[...] = _norm_mod(x_ref[...], g_ref[...], sc_ref[...], sh_ref[...]).astype(h_ref.dtype)


def _normmod(tok, x, norm_g_row, modr, layer, sc_i, sh_i):
    tm = TOKEN_TILE
    return pl.pallas_call(
        _normmod_kernel,
        grid=(tok.total // tm,),
        in_specs=[
            pl.BlockSpec((tm, D_MODEL), lambda i: (i, 0)),
            pl.BlockSpec((1, D_MODEL), lambda i: (0, 0)),
            _mod_spec(tok, tm, layer, sc_i, 1),
            _mod_spec(tok, tm, layer, sh_i, 1),
        ],
        out_specs=pl.BlockSpec((tm, D_MODEL), lambda i: (i, 0)),
        out_shape=jax.ShapeDtypeStruct((tok.total, D_MODEL), BF16),
        compiler_params=_cparams("parallel"),
        name="first_norm",
    )(x, norm_g_row, modr, modr)


def _ffn_kernel(h_ref, x_ref, gt_ref, w1_ref, w3_ref, w2_ref, gn_ref, scn_ref, shn_ref,
                out_a_ref, out_b_ref, acc_ref, *, n_ff, n_ctx_tiles):
    i = pl.program_id(0)
    j = pl.program_id(1)

    @pl.when(j == 0)
    def _():
        acc_ref[...] = jnp.zeros_like(acc_ref)

    h = h_ref[...]
    a = _dot(h, w1_ref[...])
    b = _dot(h, w3_ref[...])
    p = (_silu(a) * b).astype(BF16)
    acc_ref[...] += _dot(p, w2_ref[...])

    @pl.when(j == n_ff - 1)
    def _():
        xn = x_ref[...] + 0.5 * gt_ref[...] * acc_ref[...]
        hn = _norm_mod(xn, gn_ref[...], scn_ref[...], shn_ref[...])
        if n_ctx_tiles is None:
            out_a_ref[...] = xn
            out_b_ref[...] = hn.astype(out_b_ref.dtype)
        else:
            @pl.when(i < n_ctx_tiles)
            def _():
                out_a_ref[...] = hn

            @pl.when(i >= n_ctx_tiles)
            def _():
                out_b_ref[...] = hn


def _ffn(tok, h, x, modr, layer, gate_i, w1, w3, w2, next_g, next_sc, next_sh, final):
    tm, tf = TOKEN_TILE, FF_TILE
    n_ff = D_FF // tf
    nct = tok.ctx_tokens // tm
    row = pl.BlockSpec((tm, D_MODEL), lambda i, j: (i, 0))
    if final:
        out_specs = [pl.BlockSpec((tm, D_MODEL), lambda i, j: (jnp.minimum(i, nct - 1), 0)),
                     pl.BlockSpec((tm, D_MODEL), lambda i, j: (jnp.maximum(i - nct, 0), 0))]
        out_shape = [jax.ShapeDtypeStruct((tok.ctx_tokens, D_MODEL), F32),
                     jax.ShapeDtypeStruct((tok.total - tok.ctx_tokens, D_MODEL), F32)]
    else:
        out_specs = [row, row]
        out_shape = [jax.ShapeDtypeStruct((tok.total, D_MODEL), F32),
                     jax.ShapeDtypeStruct((tok.total, D_MODEL), BF16)]
    return pl.pallas_call(
        functools.partial(_ffn_kernel, n_ff=n_ff, n_ctx_tiles=nct if final else None),
        grid=(tok.total // tm, n_ff),
        in_specs=[
            row, row,
            _mod_spec(tok, tm, layer, gate_i, 2),
            pl.BlockSpec((D_MODEL, tf), lambda i, j: (0, j)),
            pl.BlockSpec((D_MODEL, tf), lambda i, j: (0, j)),
            pl.BlockSpec((tf, D_MODEL), lambda i, j: (j, 0)),
            pl.BlockSpec((1, D_MODEL), lambda i, j: (0, 0)),
            next_sc[1], next_sh[1],
        ],
        out_specs=out_specs,
        out_shape=out_shape,
        scratch_shapes=[pltpu.VMEM((tm, D_MODEL), F32)],
        compiler_params=_cparams("arbitrary", "arbitrary"),
        name="swiglu_half_step",
    )(h, x, modr, w1, w3, w2, next_g, next_sc[0], next_sh[0])


def _project_kernel(h_ref, w_ref, o_ref, wb_ref):
    @pl.when(pl.program_id(1) == 0)
    def _():
        wb_ref[...] = w_ref[...].astype(BF16)

    o_ref[...] = _dot(h_ref[...], wb_ref[...])


def _project(h, w_in, layer, col0, n, tn):
    t, k = h.shape
    tm = 2 * TOKEN_TILE
    assert t % tm == 0 and n % tn == 0
    return pl.pallas_call(
        _project_kernel,
        grid=(n // tn, t // tm),
        in_specs=[pl.BlockSpec((tm, k), lambda j, i: (i, 0)),
                  pl.BlockSpec((pl.Squeezed(), pl.Element(k), pl.Element(tn)),
                               lambda j, i: (layer, 0, pl.multiple_of(col0 + j * tn, LANES)))],
        out_specs=pl.BlockSpec((tm, tn), lambda j, i: (i, j)),
        out_shape=jax.ShapeDtypeStruct((t, n), F32),
        scratch_shapes=[pltpu.VMEM((k, tn), BF16)],
        compiler_params=_cparams("parallel", "arbitrary"),
        name="mix_in_proj",
    )(h, w_in)


def _gmlp_kernel(h_ref, w_ref, ng_ref, ws_ref, bs_ref, o_ref, wb_ref):
    @pl.when(pl.program_id(0) == 0)
    def _():
        wb_ref[...] = w_ref[...].astype(BF16)

    z = jax.nn.gelu(_dot(h_ref[...], wb_ref[...]))
    for c in range(z.shape[0] // CHUNK_MLP):
        rows = slice(c * CHUNK_MLP, (c + 1) * CHUNK_MLP)
        outs = []
        for g in range(A_GROUPS):
            u = z[rows, g * A_GDIM:(g + 1) * A_GDIM]
            v = z[rows, A_WIDTH + g * A_GDIM:A_WIDTH + (g + 1) * A_GDIM]
            ms = jnp.mean(v * v, axis=-1, keepdims=True)
            v = v * lax.rsqrt(ms + EPS) * ng_ref[g:g + 1, :]
            mixed = _dot(ws_ref[g], v.astype(BF16)) + bs_ref[g]
            outs.append(u * mixed)
        o_ref[rows, :] = jnp.concatenate(outs, axis=-1).astype(o_ref.dtype)


def _gmlp(h, w_in, layer, ng, ws, bs):
    t, k = h.shape
    tm = TOKEN_TILE
    return pl.pallas_call(
        _gmlp_kernel,
        grid=(t // tm,),
        in_specs=[
            pl.BlockSpec((tm, k), lambda i: (i, 0)),
            pl.BlockSpec((pl.Squeezed(), pl.Element(k), pl.Element(A_COLS)), lambda i: (layer, 0, 0)),
            pl.BlockSpec((A_GROUPS, A_GDIM), lambda i: (0, 0)),
            pl.BlockSpec((A_GROUPS, CHUNK_MLP, CHUNK_MLP), lambda i: (0, 0, 0)),
            pl.BlockSpec((A_GROUPS, CHUNK_MLP, 1), lambda i: (0, 0, 0)),
        ],
        out_specs=pl.BlockSpec((tm, A_WIDTH), lambda i: (i, 0)),
        out_shape=jax.ShapeDtypeStruct((t, A_WIDTH), BF16),
        scratch_shapes=[pltpu.VMEM((k, A_COLS), BF16)],
        compiler_params=_cparams("arbitrary"),
        name="gmlp_chunk_mix",
    )(h, w_in, ng, ws.astype(BF16), bs[:, :, None])


def _head_ones():
    r = _iota((LANES, LANES), 0) // B_HDIM
    c = _iota((LANES, LANES), 1) // B_HDIM
    return (r == c).astype(BF16)


def _head_sum(x, parts=2):
    ones = _head_ones()
    outs = []
    for s in range(x.shape[-1] // LANES):
        outs.append(_exact_right(x[:, s * LANES:(s + 1) * LANES], ones, parts))
    return jnp.concatenate(outs, axis=-1)


def _rwkv_prep_kernel(z_ref, zp_ref, zn_ref, mu_ref, w0_ref, w2_ref, a0_ref, a2_ref, g2_ref,
                      kkw_ref, kaw_ref, rk_ref,
                      at_ref, rt_ref, bt_ref, kt_ref, v_ref, gl_ref, gate_ref, bonus_ref,
                      *, n_ctx_tiles, ctx_tiles_per_seq, lat_tiles_per_seq):
    i = pl.program_id(0)
    pos = jnp.where(i < n_ctx_tiles, i % ctx_tiles_per_seq, (i - n_ctx_tiles) % lat_tiles_per_seq)
    last = jnp.where(i < n_ctx_tiles, ctx_tiles_per_seq - 1, lat_tiles_per_seq - 1)
    z = z_ref[...]
    t = z.shape[0]
    row = _iota((t, 1), 0)
    halo_prev = jnp.where(pos != 0, zp_ref[7:8, :], 0.0)
    halo_next = jnp.where(pos != last, zn_ref[0:1, :], 0.0)
    prev = jnp.where(row == 0, halo_prev, pltpu.roll(z, 1, 0))
    nxt = jnp.where(row == t - 1, halo_next, pltpu.roll(z, t - 1, 0))
    z = z + (0.5 * (prev + nxt) - z) * mu_ref[...]

    w = B_WIDTH
    r, k, v = z[:, 0:w], z[:, w:2 * w], z[:, 2 * w:3 * w]
    wd = z[:, 3 * w:3 * w + 2 * LORA]
    ad = z[:, 3 * w + 2 * LORA:3 * w + 4 * LORA]
    gd = z[:, 3 * w + 4 * LORA:]
    twd = jnp.tanh(wd).astype(BF16)
    adb = ad.astype(BF16)

    kk = k * kkw_ref[...]
    nrm = jnp.sqrt(_head_sum(kk * kk))
    kk = kk / jnp.maximum(nrm, 1e-12)

    v_ref[...] = v.astype(BF16)
    gate_ref[...] = _dot(_sigmoid(gd).astype(BF16), g2_ref[...])
    cl = RW_CHUNK
    ri, ci = _iota((t, t), 0), _iota((t, t), 1)
    same_chunk = (ri // cl) == (ci // cl)
    ksum = jnp.zeros_like(k)
    for d in range(2):
        w_raw = w0_ref[d:d + 1, :] + _dot(twd[:, d * LORA:(d + 1) * LORA], w2_ref[d])
        lw = -math.exp(-0.5) * _sigmoid(w_raw)
        a = _sigmoid(a0_ref[d:d + 1, :] + _dot(adb[:, d * LORA:(d + 1) * LORA], a2_ref[d]))
        kd = k * (1.0 + (a - 1.0) * kaw_ref[...])
        ksum = ksum + kd
        upto = same_chunk & ((ci >= ri) if d == 1 else (ci <= ri))
        g = _exact_left(upto.astype(BF16), lw, 2)
        ieg = jnp.exp(-g)
        at_ref[d] = (-kk * jnp.exp(g - lw)).astype(BF16)
        rt_ref[d] = (r * jnp.exp(g)).astype(BF16)
        bt_ref[d] = (kk * a * ieg).astype(BF16)
        kt_ref[d] = (kd * ieg).astype(BF16)
        for c in range(t // cl):
            end = c * cl if d == 1 else (c + 1) * cl - 1
            gl_ref[d, c] = jnp.exp(g[end:end + 1, :])
    bonus_ref[...] = _head_sum(r * ksum * rk_ref[...]) * v


def _rwkv_prep(tok, zb, mu, w0, w2, a0, a2, g2, kkw, kaw, rk):
    tp = PREP_TILE
    n_tiles = tok.total // tp
    rows8 = tok.total // 8
    per = tp // 8
    full = lambda shape: pl.BlockSpec(shape, lambda i: (0,) * len(shape))
    cpt = tp // RW_CHUNK
    tile = pl.BlockSpec((tp, B_WIDTH), lambda i: (i, 0))
    tile2 = pl.BlockSpec((2, tp, B_WIDTH), lambda i: (0, i, 0))
    gl_spec = pl.BlockSpec((2, cpt, 1, B_WIDTH), lambda i: (0, i, 0, 0))
    o1 = jax.ShapeDtypeStruct((tok.total, B_WIDTH), F32)
    o1b = jax.ShapeDtypeStruct((tok.total, B_WIDTH), BF16)
    o2b = jax.ShapeDtypeStruct((2, tok.total, B_WIDTH), BF16)
    ogl = jax.ShapeDtypeStruct((2, tok.total // RW_CHUNK, 1, B_WIDTH), F32)
    kern = functools.partial(_rwkv_prep_kernel, n_ctx_tiles=tok.ctx_tokens // tp,
                             ctx_tiles_per_seq=tok.ctx_len // tp, lat_tiles_per_seq=tok.lat_len // tp)
    return pl.pallas_call(
        kern,
        grid=(n_tiles,),
        in_specs=[
            pl.BlockSpec((tp, B_COLS), lambda i: (i, 0)),
            pl.BlockSpec((8, B_COLS), lambda i: (jnp.maximum(i * per - 1, 0), 0)),
            pl.BlockSpec((8, B_COLS), lambda i: (jnp.minimum((i + 1) * per, rows8 - 1), 0)),
            full((1, B_COLS)), full((2, B_WIDTH)), full((2, LORA, B_WIDTH)), full((2, B_WIDTH)),
            full((2, LORA, B_WIDTH)), full((GATE_LORA, B_WIDTH)),
            full((1, B_WIDTH)), full((1, B_WIDTH)), full((1, B_WIDTH)),
        ],
        out_specs=[tile2, tile2, tile2, tile2, tile, gl_spec, tile, tile],
        out_shape=[o2b, o2b, o2b, o2b, o1b, ogl, o1, o1],
        compiler_params=_cparams("parallel"),
        name="rwkv_prep",
    )(zb, zb, zb, mu, w0, w2.astype(BF16), a0, a2.astype(BF16), g2.astype(BF16), kkw, kaw, rk)


def _pair_blockdiag(x):
    lane = _iota(x.shape, 1)
    zero = jnp.zeros_like(x)
    return jnp.concatenate([jnp.where(lane < B_HDIM, x, zero), jnp.where(lane >= B_HDIM, x, zero)], axis=0)


def _rwkv_chunks(streams, s_ref, y_refs):
    c = streams[0][0].shape[0]
    n_pairs = B_HEADS // 2
    rr, cc = _iota((2 * c, 2 * c), 0), _iota((2 * c, 2 * c), 1)
    same = (rr // c) == (cc // c)
    tt, ss = rr % c, cc % c
    eye = (rr == cc).astype(F32)
    pair_masks = []
    n = 1
    while n < c:
        pair_masks.append(same & ((tt // (2 * n)) == (ss // (2 * n))) & ((tt // n) != (ss // n)))
        n *= 2

    units = []
    for d, (a_t, r_t, b_t, k_t, vb, gl, reverse) in enumerate(streams):
        before = same & ((ss > tt) if reverse else (ss < tt))
        upto = same & ((ss >= tt) if reverse else (ss <= tt))
        for p in range(n_pairs):
            sl = slice(p * LANES, (p + 1) * LANES)
            units.append(dict(d=d, p=p, before=before, upto=upto, gl=gl[:, sl],
                              ar=jnp.concatenate([_pair_blockdiag(a_t[:, sl]), _pair_blockdiag(r_t[:, sl])], 0),
                              bk=jnp.concatenate([_pair_blockdiag(b_t[:, sl]), _pair_blockdiag(k_t[:, sl])], 0),
                              v=_pair_blockdiag(vb[:, sl])))
    m = 2 * c
    for u in units:
        u["s0"] = s_ref[u["d"], u["p"]]
        u["gram"] = _dot_nt(u["ar"], jnp.concatenate([u["bk"], u["s0"].astype(BF16)], axis=0))
    yield
    for u in units:
        u["ws"] = u["gram"][:, 2 * m:]
        g = u["gram"]
        u["lmat"] = jnp.where(u["before"], g[:m, :m], 0.0)
        aak = jnp.where(u["before"], g[:m, m:2 * m], 0.0).astype(BF16)
        u["rbk"] = jnp.concatenate([jnp.where(u["upto"], g[m:, :m], 0.0),
                                    jnp.where(u["upto"], g[m:, m:2 * m], 0.0)], axis=1).astype(BF16)
        u["x"] = u["ws"][:m] + _dot(aak, u["v"])
    yield
    for lvl, pm in enumerate(pair_masks):
        for u in units:
            link = jnp.where(pm, u["lmat"], 0.0)
            if lvl == 0:
                u["tb"] = (eye + link).astype(BF16)
            else:
                u["tmp"] = _dot(link.astype(BF16), u["tb"]).astype(BF16)
        if lvl > 0:
            yield
            for u in units:
                u["tb"] = (u["tb"].astype(F32) + _dot(u["tb"], u["tmp"])).astype(BF16)
            yield
    for u in units:
        u["uv"] = jnp.concatenate([_dot(u["tb"], u["x"].astype(BF16)).astype(BF16), u["v"]], 0)
    yield
    ys = [[None] * n_pairs for _ in streams]
    for u in units:
        y = u["ws"][m:] + _dot(u["rbk"], u["uv"])
        ys[u["d"]][u["p"]] = y[:c] + y[c:]
        s_ref[u["d"], u["p"]] = (u["s0"] + _dot_tn(u["uv"], u["bk"])) * u["gl"]
    for d, row in enumerate(ys):
        y_refs[d][...] = jnp.concatenate(row, axis=-1)


def _scan_table(tok, chunk):
    cols = []
    for s in range(tok.n_ctx + tok.n_lat):
        if s < tok.n_ctx:
            base, nc = s * tok.ctx_len // chunk, tok.ctx_len // chunk
        else:
            base = (tok.ctx_tokens + (s - tok.n_ctx) * tok.lat_len) // chunk
            nc = tok.lat_len // chunk
        for c in range(nc):
            cols.append((base + c, base + nc - 1 - c, int(s < tok.n_ctx), int(c == 0), int(c == nc - 1),
                         max(s - tok.n_ctx, 0), min(s, tok.n_ctx - 1)))
    return jnp.asarray(list(zip(*cols)), dtype=jnp.int32)


def _rwkv_post_kernel(yf_ref, yb_ref, bonus_ref, gate_ref, g_ref, b_ref, o_ref):
    y = yf_ref[...] + yb_ref[...]
    mean = _head_sum(y, 3) * (1.0 / B_HDIM)
    yc = y - mean
    var = _head_sum(yc * yc) * (1.0 / B_HDIM)
    y = yc * lax.rsqrt(var + LNX_EPS) * g_ref[...] + b_ref[...] + bonus_ref[...]
    o_ref[...] = (y * gate_ref[...]).astype(o_ref.dtype)


def _hgrn_block(qraw, fraw, v, lb, st_ref, o_ref, reverse):
    n = qraw.shape[0]
    sub = HG_SUB
    fg = lb + (1.0 - lb) * _sigmoid(fraw)
    logf = jnp.log(fg)
    kg = 1.0 - fg
    q = _silu(qraw)
    ri, ci = _iota((n, n), 0), _iota((n, n), 1)
    same = (ri // sub) == (ci // sub)
    upto = same & ((ci >= ri) if reverse else (ci <= ri))
    bcum = _exact_left(upto.astype(BF16), logf, 3)
    half = 8
    hi = _iota((half, 1), 0)
    order = range(n // sub - 1, -1, -1) if reverse else range(n // sub)
    out_rows = [None] * (n // sub)
    for sc in order:
        last = sc * sub if reverse else (sc + 1) * sub - 1
        outs = []
        for h in range(C_HEADS):
            cols = slice(h * C_KDIM, (h + 1) * C_KDIM)
            rows = slice(sc * sub, (sc + 1) * sub)
            qs, ks, vs, bs = q[rows, cols], kg[rows, cols], v[rows, cols], bcum[rows, cols]
            fs = fg[rows, cols]
            b_end = bcum[last:last + 1, cols]
            hq = [jnp.zeros((half, C_KDIM), F32) for _ in range(sub // half)]
            intra = [jnp.zeros((half, C_KDIM), F32) for _ in range(sub // half)]
            for j in (range(sub) if reverse else range(sub - 1, -1, -1)):
                jh = j // half
                hq[jh] = jnp.where(hi == j % half, qs[jh * half:(jh + 1) * half], hq[jh])
                entered = range(jh + 1) if reverse else range(jh, sub // half)
                for hf in entered:
                    att = jnp.sum(hq[hf] * ks[j:j + 1, :], axis=-1, keepdims=True)
                    intra[hf] = intra[hf] + att * vs[j:j + 1, :]
                    hq[hf] = hq[hf] * fs[j:j + 1, :]
            o = _dot_nt((qs * jnp.exp(bs)).astype(BF16), st_ref[h].astype(BF16))
            outs.append(o + jnp.concatenate(intra, axis=0))
            ke = (ks * jnp.exp(b_end - bs)).astype(BF16)
            st_ref[h] = st_ref[h] * jnp.exp(b_end) + _dot_tn(vs.astype(BF16), ke)
            if h % (C_HEADS // 2) == C_HEADS // 2 - 1:
                yield
        out_rows[sc] = jnp.concatenate(outs, axis=-1)
    o_ref[...] = jnp.concatenate(out_rows, axis=0)


def _mix_scan_kernel(tab_ref, *refs, n_cast):
    (atf_ref, rtf_ref, btf_ref, ktf_ref, vf_ref, glf_ref,
     atb_ref, rtb_ref, btb_ref, ktb_ref, vb_ref, glb_ref, rw0_ref,
     qf_ref, ff_ref, if_ref, qb_ref, fb_ref, ib_ref, lb_ref, hg0_ref) = refs[:21]
    cast_in = refs[21:21 + n_cast]
    yf_ref, yb_ref, rwfin_ref, of_ref, ob_ref, hgfin_ref = refs[21 + n_cast:27 + n_cast]
    cast_out = refs[27 + n_cast:27 + 2 * n_cast]
    s_ref, st_ref = refs[27 + 2 * n_cast:]
    step = pl.program_id(0)
    hd = B_HDIM
    for src_ref, dst_ref in zip(cast_in, cast_out):
        dst_ref[...] = src_ref[...].astype(BF16)

    is_ctx = tab_ref[2, step] == 1
    first = tab_ref[3, step] == 1
    last = tab_ref[4, step] == 1

    @pl.when(first & is_ctx)
    def _():
        s_ref[...] = jnp.zeros_like(s_ref)
        st_ref[...] = jnp.zeros_like(st_ref)

    @pl.when(first & jnp.logical_not(is_ctx))
    def _():
        z = jnp.zeros((hd, hd), F32)
        for d in range(2):
            for p in range(B_HEADS // 2):
                s_ref[d, p] = jnp.concatenate(
                    [jnp.concatenate([rw0_ref[d, 2 * p], z], axis=1),
                     jnp.concatenate([z, rw0_ref[d, 2 * p + 1]], axis=1)], axis=0)
            for h in range(C_HEADS):
                st_ref[d, h] = hg0_ref[d, h].T

    pending = [
        _rwkv_chunks(
            [(atf_ref[...], rtf_ref[...], btf_ref[...], ktf_ref[...], vf_ref[...], glf_ref[...], False),
             (atb_ref[...], rtb_ref[...], btb_ref[...], ktb_ref[...], vb_ref[...], glb_ref[...], True)],
            s_ref, (yf_ref, yb_ref)),
        _hgrn_block(qf_ref[...], ff_ref[...], if_ref[...], lb_ref[0:1, :], st_ref.at[0], of_ref, False),
        _hgrn_block(qb_ref[...], fb_ref[...], ib_ref[...], lb_ref[1:2, :], st_ref.at[1], ob_ref, True),
    ]
    while pending:
        for gen in list(pending):
            if next(gen, pending) is pending:
                pending.remove(gen)

    @pl.when(last & is_ctx)
    def _():
        for d in range(2):
            for p in range(B_HEADS // 2):
                s = s_ref[d, p]
                rwfin_ref[d, 2 * p] = s[:hd, :hd]
                rwfin_ref[d, 2 * p + 1] = s[hd:, hd:]
            for h in range(C_HEADS):
                hgfin_ref[d, h] = st_ref[d, h].T


def _mix_scan(tok, layer, at, rt, bt, kt, v, gl, rw0, zc, lb, hg0, to_cast):
    cl = RW_CHUNK
    assert HG_BLOCK == cl
    tab = _scan_table(tok, cl)
    fwd = lambda col: (lambda s, t: (t[0, s], col))
    bwd = lambda col: (lambda s, t: (t[1, s], col))
    fwd2 = lambda s, t: (0, t[0, s], 0)
    bwd2 = lambda s, t: (1, t[1, s], 0)
    blk = lambda im: pl.BlockSpec((cl, B_WIDTH), im)
    blk2 = lambda im: pl.BlockSpec((None, cl, B_WIDTH), im)
    glf = pl.BlockSpec((None, None, 1, B_WIDTH), lambda s, t: (0, t[0, s], 0, 0))
    glb = pl.BlockSpec((None, None, 1, B_WIDTH), lambda s, t: (1, t[1, s], 0, 0))
    rw_in = pl.BlockSpec((None, None, 2, B_HEADS, B_HDIM, B_HDIM), lambda s, t: (t[5, s], layer, 0, 0, 0, 0))
    hg_in = pl.BlockSpec((None, None, 2, C_HEADS, C_KDIM, C_KDIM), lambda s, t: (t[5, s], layer, 0, 0, 0, 0))
    rw_out = pl.BlockSpec((None, 2, B_HEADS, B_HDIM, B_HDIM), lambda s, t: (t[6, s], 0, 0, 0, 0))
    hg_out = pl.BlockSpec((None, 2, C_HEADS, C_KDIM, C_KDIM), lambda s, t: (t[6, s], 0, 0, 0, 0))
    cblk = lambda im: pl.BlockSpec((cl, C_WIDTH), im)
    yshape = jax.ShapeDtypeStruct((tok.total, B_WIDTH), F32)
    oshape = jax.ShapeDtypeStruct((tok.total, C_WIDTH), F32)
    cast_ops, cast_in, cast_out, cast_shapes = [], [], [], []
    for w, wl, wk in to_cast:
        rows, cols = w.shape[2:]
        tile_rows = next(r for r in range(16, rows + 1, 16) if rows % r == 0 and rows // r <= tab.shape[1])
        n_tiles = rows // tile_rows
        cast_ops.append(w)
        cast_in.append(pl.BlockSpec((None, None, tile_rows, cols),
                                    lambda s, t, wl=wl, wk=wk, n=n_tiles: (wl, wk, jnp.minimum(s, n - 1), 0)))
        cast_out.append(pl.BlockSpec((tile_rows, cols), lambda s, t, n=n_tiles: (jnp.minimum(s, n - 1), 0)))
        cast_shapes.append(jax.ShapeDtypeStruct((rows, cols), BF16))
    outs = pl.pallas_call(
        functools.partial(_mix_scan_kernel, n_cast=len(to_cast)),
        grid_spec=pltpu.PrefetchScalarGridSpec(
            num_scalar_prefetch=1,
            grid=(tab.shape[1],),
            in_specs=[blk2(fwd2), blk2(fwd2), blk2(fwd2), blk2(fwd2), blk(fwd(0)), glf,
                      blk2(bwd2), blk2(bwd2), blk2(bwd2), blk2(bwd2), blk(bwd(0)), glb, rw_in,
                      cblk(fwd(0)), cblk(fwd(1)), cblk(fwd(3)), cblk(bwd(0)), cblk(bwd(2)), cblk(bwd(3)),
                      pl.BlockSpec((2, C_WIDTH), lambda s, t: (0, 0)), hg_in] + cast_in,
            out_specs=[blk(fwd(0)), blk(bwd(0)), rw_out, cblk(fwd(0)), cblk(bwd(0)), hg_out] + cast_out,
            scratch_shapes=[pltpu.VMEM((2, B_HEADS // 2, LANES, LANES), F32),
                            pltpu.VMEM((2, C_HEADS, C_KDIM, C_KDIM), F32)]),
        out_shape=[yshape, yshape, jax.ShapeDtypeStruct((tok.n_ctx,) + rw0.shape[2:], F32),
                   oshape, oshape, jax.ShapeDtypeStruct((tok.n_ctx,) + hg0.shape[2:], F32)] + cast_shapes,
        compiler_params=_cparams("arbitrary"),
        name="rwkv_hgrn_scan",
    )(tab, at, rt, bt, kt, v, gl, at, rt, bt, kt, v, gl, rw0, zc, zc, zc, zc, zc, zc, lb, hg0, *cast_ops)
    return tuple(outs[:6]) + (list(outs[6:]),)


def _hgrn_post_kernel(of_ref, ob_ref, g_ref, gn_ref, o_ref):
    o = of_ref[...] + ob_ref[...]
    gate = _silu(g_ref[...])
    outs = []
    for h in range(C_HEADS):
        cols = slice(h * C_KDIM, (h + 1) * C_KDIM)
        oh = o[:, cols]
        ms = jnp.mean(oh * oh, axis=-1, keepdims=True)
        outs.append(oh * lax.rsqrt(ms + EPS) * gn_ref[...] * gate[:, cols])
    o_ref[...] = jnp.concatenate(outs, axis=-1).astype(o_ref.dtype)


def _mix_post_kernel(yf_ref, yb_ref, bonus_ref, gate_ref, lg_ref, lb_ref, of_ref, ob_ref, g_ref, gn_ref,
                     orw_ref, ohg_ref):
    _rwkv_post_kernel(yf_ref, yb_ref, bonus_ref, gate_ref, lg_ref, lb_ref, orw_ref)
    _hgrn_post_kernel(of_ref, ob_ref, g_ref, gn_ref, ohg_ref)


def _mix_post(yf, yb, bonus, gate, lnx_g, lnx_b, of, ob, zc, gn):
    t = yf.shape[0]
    tm = TOKEN_TILE
    tile = pl.BlockSpec((tm, B_WIDTH), lambda i: (i, 0))
    vec = pl.BlockSpec((1, B_WIDTH), lambda i: (0, 0))
    ctile = pl.BlockSpec((tm, C_WIDTH), lambda i: (i, 0))
    return pl.pallas_call(
        _mix_post_kernel,
        grid=(t // tm,),
        in_specs=[tile, tile, tile, tile, vec, vec,
                  ctile, ctile, pl.BlockSpec((tm, C_WIDTH), lambda i: (i, 4)),
                  pl.BlockSpec((1, C_KDIM), lambda i: (0, 0))],
        out_specs=[tile, ctile],
        out_shape=[jax.ShapeDtypeStruct((t, B_WIDTH), BF16), jax.ShapeDtypeStruct((t, C_WIDTH), BF16)],
        compiler_params=_cparams("parallel"),
        name="mix_post",
    )(yf, yb, bonus, gate, lnx_g, lnx_b, of, ob, zc, gn)


def _mix_out_kernel(oa_ref, ob_ref, oc_ref, wa_ref, wb_ref, wc_ref, x_ref, gt_ref,
                    gn_ref, scn_ref, shn_ref, xo_ref, ho_ref):
    y = _dot(oa_ref[...], wa_ref[...]) + _dot(ob_ref[...], wb_ref[...]) + _dot(oc_ref[...], wc_ref[...])
    xn = x_ref[...] + gt_ref[...] * y
    xo_ref[...] = xn
    ho_ref[...] = _norm_mod(xn, gn_ref[...], scn_ref[...], shn_ref[...]).astype(ho_ref.dtype)


def _mix_out(tok, oa, ob, oc, wa, wb, wc, x, modr, layer, next_g):
    tm = TOKEN_TILE
    row = pl.BlockSpec((tm, D_MODEL), lambda i: (i, 0))
    full = lambda a: pl.BlockSpec(a.shape, lambda i: (0, 0))
    return pl.pallas_call(
        _mix_out_kernel,
        grid=(tok.total // tm,),
        in_specs=[pl.BlockSpec((tm, A_WIDTH), lambda i: (i, 0)),
                  pl.BlockSpec((tm, B_WIDTH), lambda i: (i, 0)),
                  pl.BlockSpec((tm, C_WIDTH), lambda i: (i, 0)),
                  full(wa), full(wb), full(wc), row,
                  _mod_spec(tok, tm, layer, 5, 1),
                  pl.BlockSpec((1, D_MODEL), lambda i: (0, 0)),
                  _mod_spec(tok, tm, layer, 7, 1), _mod_spec(tok, tm, layer, 6, 1)],
        out_specs=[row, row],
        out_shape=[jax.ShapeDtypeStruct((tok.total, D_MODEL), F32),
                   jax.ShapeDtypeStruct((tok.total, D_MODEL), BF16)],
        compiler_params=_cparams("parallel"),
        name="mix_out_proj",
    )(oa, ob, oc, wa, wb, wc, x, modr, next_g, modr, modr)


def kernel(x_prompt, x_sample, state_rwkv, state_hgrn, c, c_ctx, norm_g, w_mod, b_mod, ffn_w1, ffn_w3,
           ffn_w2, w_in, w_out, mlp_norm_g, mlp_ws, mlp_bs, rwkv_mu, rwkv_w0, rwkv_w2, rwkv_a0, rwkv_a2,
           rwkv_g2, rwkv_kk, rwkv_ka, rwkv_rk, rwkv_lnx_g, rwkv_lnx_b, hgrn_lb, hgrn_gn, final_g):
    n_ctx, ctx_len, _ = x_prompt.shape
    n_lat, lat_len, _ = x_sample.shape
    depth = w_mod.shape[0]
    tok = _Tokens(n_ctx, ctx_len, n_lat, lat_len)
    assert n_lat + 1 <= 8 and ctx_len % PREP_TILE == 0 and lat_len % TOKEN_TILE == 0

    cond8 = jnp.zeros((8, D_MODEL), F32).at[0].set(c_ctx).at[1:1 + n_lat].set(c)
    mod = _mod_rows(cond8, w_mod, b_mod)
    modr = mod.reshape(depth, 8, N_MOD, D_MODEL).transpose(0, 2, 1, 3)[:, :, :, None, :]

    sm = jax.nn.softmax(hgrn_lb.astype(F32), axis=0)
    lower = jnp.cumsum(sm, axis=0) - sm[0]

    x = jnp.concatenate([x_prompt.reshape(-1, D_MODEL), x_sample.reshape(-1, D_MODEL)], axis=0)
    zero_sc = jnp.zeros((1, D_MODEL), F32)
    zero_spec = pl.BlockSpec((1, D_MODEL), lambda i, j: (0, 0))
    tm = TOKEN_TILE

    h = _normmod(tok, x, norm_g[0, 0][None], modr, 0, 1, 0)
    bw = lambda a: a.astype(BF16)
    ffn_w = {(0, 0): (bw(ffn_w1[0, 0]), bw(ffn_w3[0, 0]), bw(ffn_w2[0, 0]))}
    rw_states, hg_states = [], []
    for l in range(depth):
        x, h = _ffn(tok, h, x, modr, l, 2, *ffn_w[(l, 0)],
                    norm_g[l, 1][None], (modr, _mod_spec(tok, tm, l, 4, 2)),
                    (modr, _mod_spec(tok, tm, l, 3, 2)), False)

        zb = _project(h, w_in, l, A_COLS, B_COLS, 896)
        zc = _project(h, w_in, l, A_COLS + B_COLS, C_COLS, 768)

        oa = _gmlp(h, w_in, l, mlp_norm_g[l], mlp_ws[l], mlp_bs[l])

        at, rt, bt, kt, v, gl, gate, bonus = _rwkv_prep(
            tok, zb, rwkv_mu[l][None], rwkv_w0[l], rwkv_w2[l], rwkv_a0[l], rwkv_a2[l], rwkv_g2[l],
            rwkv_kk[l][None], rwkv_ka[l][None], rwkv_rk[l].reshape(1, B_WIDTH))
        upcoming = [(l, 1)] + ([(l + 1, 0)] if l + 1 < depth else [])
        to_cast = [(w, fl, fk) for fl, fk in upcoming for w in (ffn_w1, ffn_w3, ffn_w2)]
        yf, yb, s_rw, of, obw, s_hg, casted = _mix_scan(tok, l, at, rt, bt, kt, v, gl, state_rwkv, zc, lower[l],
                                                        state_hgrn, to_cast)
        for n, key in enumerate(upcoming):
            ffn_w[key] = tuple(casted[3 * n:3 * n + 3])
        ob, oc = _mix_post(yf, yb, bonus, gate, rwkv_lnx_g[l].reshape(1, B_WIDTH),
                           rwkv_lnx_b[l].reshape(1, B_WIDTH), of, obw, zc, hgrn_gn[l][None])
        rw_states.append(s_rw)
        hg_states.append(s_hg)

        w_out_l = bw(w_out[l])
        x, h = _mix_out(tok, oa, ob, oc, w_out_l[:A_WIDTH], w_out_l[A_WIDTH:A_WIDTH + B_WIDTH],
                        w_out_l[A_WIDTH + B_WIDTH:], x, modr, l, norm_g[l, 2][None])

        if l + 1 < depth:
            x, h = _ffn(tok, h, x, modr, l, 8, *ffn_w[(l, 1)],
                        norm_g[l + 1, 0][None], (modr, _mod_spec(tok, tm, l + 1, 1, 2)),
                        (modr, _mod_spec(tok, tm, l + 1, 0, 2)), False)
        else:
            y_prompt, y_sample = _ffn(tok, h, x, modr, l, 8, *ffn_w[(l, 1)],
                                      final_g[None], (zero_sc, zero_spec), (zero_sc, zero_spec), True)

    return (y_prompt.reshape(x_prompt.shape), y_sample.reshape(x_sample.shape),
            jnp.stack(rw_states, axis=1), jnp.stack(hg_states, axis=1))
```
